```python
import math
import jax, jax.numpy as jnp
from jax import lax
import numpy as np

D_MODEL = 1024
BATCH = 4
SEQ = 8192
DEPTH = 2

CTX_LEN = 256
GRID_W = 64
ROPE_DIM = 64
ROPE_BASE = 10000.0
Q_BLOCK = 128
A_HEADS = 4
A_QK_DIM = ROPE_DIM
A_V_DIM = 2 * A_QK_DIM
A_WIDTH = A_HEADS * A_V_DIM
B_WIDTH = D_MODEL - A_WIDTH
CONV_W = 3
EVEN_IN = 3 * A_WIDTH + 3 * B_WIDTH
EVEN_OUT = A_WIDTH + B_WIDTH
C_HEAD_DIM = ROPE_DIM
C_HEADS = D_MODEL // C_HEAD_DIM
C_KV_HEADS = 2
C_WINDOW = 128
ODD_IN = (C_HEADS + 2 * C_KV_HEADS) * C_HEAD_DIM
ODD_OUT = C_HEADS * C_HEAD_DIM
N_EXPERTS = 32
TOP_K = 4
D_FF = D_MODEL
SWIGLU_ALPHA = 1.702
SWIGLU_LIMIT = 7.0
MOE_BLOCK = 512
LN_EPS = 1e-5
RMS_EPS = 1e-5
NEG_INF = -1e30
N_EVEN = (DEPTH + 1) // 2
N_ODD = DEPTH // 2
DN_ALPHA = (2 * DEPTH) ** 0.25
DN_BETA = (8 * DEPTH) ** -0.25

kernel_name = 'hybrid_diffattn_shortconv_swa_moe_dit'


def layer_norm(x, g, b):
    xf = x.astype(jnp.float32)
    mu = jnp.mean(xf, -1, keepdims=True)
    var = jnp.mean(jnp.square(xf - mu), -1, keepdims=True)
    return ((xf - mu) * lax.rsqrt(var + LN_EPS) * g.astype(jnp.float32) + b.astype(jnp.float32)).astype(x.dtype)


def rope_tables(n):
    rows_n = n // GRID_W
    rows = jnp.repeat(jnp.arange(rows_n, dtype=jnp.float32), GRID_W)
    cols = jnp.tile(jnp.arange(GRID_W, dtype=jnp.float32), rows_n)
    axis_dim = ROPE_DIM // 2
    inv = ROPE_BASE ** (-jnp.arange(0, axis_dim, 2, dtype=jnp.float32) / axis_dim)
    ar, ac = rows[:, None] * inv, cols[:, None] * inv
    return (jnp.cos(ar), jnp.sin(ar), jnp.cos(ac), jnp.sin(ac))


def _rot_half(v, cos, sin):
    v1, v2 = jnp.split(v, 2, axis=-1)
    return jnp.concatenate([v1 * cos - v2 * sin, v2 * cos + v1 * sin], -1)


def apply_rope2d(t, tabs):
    shape = (t.shape[1],) + (1,) * (t.ndim - 3) + (ROPE_DIM // 4,)
    cr, sr, cc, sc = [a.reshape(shape) for a in tabs]
    tr, tcol = jnp.split(t.astype(jnp.float32), 2, -1)
    return jnp.concatenate([_rot_half(tr, cr, sr), _rot_half(tcol, cc, sc)], -1).astype(t.dtype)


def diff_attend(qb, k_all, v_all, lam):
    s = jnp.einsum('bqhmd,bkhmd->bhmqk', qb, k_all).astype(jnp.float32) * (A_QK_DIM ** -0.5)
    p = jax.nn.softmax(s, axis=-1)
    w = (p[:, :, 0] - lam * p[:, :, 1]).astype(v_all.dtype)
    return jnp.einsum('bhqk,bkhe->bqhe', w, v_all)


def diff_head_norm(o, g, lam_init):
    of = o.astype(jnp.float32)
    of = of * lax.rsqrt(jnp.mean(of * of, -1, keepdims=True) + RMS_EPS)
    of = of * g.astype(jnp.float32) * (1.0 - lam_init)
    return of.astype(o.dtype).reshape(o.shape[:2] + (-1,))


def short_conv(u, w):
    half = CONV_W // 2
    n = u.shape[1]
    up = jnp.pad(u, ((0, 0), (half, half), (0, 0)))
    return sum(up[:, j:j + n] * w[j] for j in range(CONV_W))


def even_mixer(h_lat, h_ctx, w_in, w_out, lam_p, subln_g, conv_w, lam_init, tabs, ctx_out):
    b, s, _ = h_lat.shape
    cuts = [A_WIDTH, 2 * A_WIDTH, 3 * A_WIDTH, 3 * A_WIDTH + B_WIDTH, 3 * A_WIDTH + 2 * B_WIDTH]
    qk_shape = lambda t: t.reshape(t.shape[:2] + (A_HEADS, 2, A_QK_DIM))
    v_shape = lambda t: t.reshape(t.shape[:2] + (A_HEADS, A_V_DIM))
    q, k, v, bg, cg, xin = jnp.split(h_lat @ w_in, cuts, axis=-1)
    q = apply_rope2d(qk_shape(q), tabs)
    k = apply_rope2d(qk_shape(k), tabs)
    v = v_shape(v)
    if ctx_out:
        qc, kc, vc, bgc, cgc, xinc = jnp.split(h_ctx @ w_in, cuts, axis=-1)
    else:
        kc, vc = jnp.split(h_ctx @ w_in[:, A_WIDTH:3 * A_WIDTH], 2, axis=-1)
    kc, vc = qk_shape(kc), v_shape(vc)
    lf = lam_p.astype(jnp.float32)
    lam = jnp.exp(jnp.sum(lf[0] * lf[1])) - jnp.exp(jnp.sum(lf[2] * lf[3])) + lam_init
    k_all = jnp.concatenate([k, kc], 1)
    v_all = jnp.concatenate([v, vc], 1)
    nb = s // Q_BLOCK
    qb = q.reshape((b, nb, Q_BLOCK) + q.shape[2:]).swapaxes(0, 1)
    o = lax.map(lambda t: diff_attend(t, k_all, v_all, lam), qb)
    o = o.swapaxes(0, 1).reshape(b, s, A_HEADS, A_V_DIM)
    a_lat = diff_head_norm(o, subln_g, lam_init)
    b_lat = bg * short_conv(cg * xin, conv_w)
    y_lat = jnp.concatenate([a_lat, b_lat], -1) @ w_out
    if not ctx_out:
        return y_lat, None
    a_ctx = diff_head_norm(diff_attend(qk_shape(qc), kc, vc, lam), subln_g, lam_init)
    b_ctx = bgc * short_conv(cgc * xinc, conv_w)
    y_ctx = jnp.concatenate([a_ctx, b_ctx], -1) @ w_out
    return y_lat, y_ctx


def odd_mixer(h_lat, h_ctx, w_in, w_out, sinks, tabs, ctx_out):
    b, s, _ = h_lat.shape
    g = C_HEADS // C_KV_HEADS
    qw, kw = C_HEADS * C_HEAD_DIM, C_KV_HEADS * C_HEAD_DIM
    q_shape = lambda t: t.reshape(t.shape[:2] + (C_KV_HEADS, g, C_HEAD_DIM))
    kv_shape = lambda t: t.reshape(t.shape[:2] + (C_KV_HEADS, C_HEAD_DIM))
    q, k, v = jnp.split(h_lat @ w_in, [qw, qw + kw], axis=-1)
    q = apply_rope2d(q_shape(q), tabs)
    k = apply_rope2d(kv_shape(k), tabs)
    v = kv_shape(v)
    if ctx_out:
        qc, kc, vc = jnp.split(h_ctx @ w_in, [qw, qw + kw], axis=-1)
    else:
        kc, vc = jnp.split(h_ctx @ w_in[:, qw:], [kw], axis=-1)
    kc, vc = kv_shape(kc), kv_shape(vc)
    sink = sinks.astype(jnp.float32).reshape(C_KV_HEADS, g)[None, :, :, None, None]
    scale = C_HEAD_DIM ** -0.5
    nb = s // Q_BLOCK
    band_len = Q_BLOCK + 2 * C_WINDOW

    def band(t):
        tp = jnp.pad(t, ((0, 0), (C_WINDOW, C_WINDOW), (0, 0), (0, 0)))
        tb = tp.reshape((b, nb + 2, Q_BLOCK) + t.shape[2:])
        return jnp.concatenate([tb[:, :-2], tb[:, 1:-1], tb[:, 2:]], 2).swapaxes(0, 1)

    blk = jnp.arange(nb)[:, None, None] * Q_BLOCK
    qpos = blk + jnp.arange(Q_BLOCK)[None, :, None]
    kpos = blk - C_WINDOW + jnp.arange(band_len)[None, None, :]
    mask = (jnp.abs(kpos - qpos) <= C_WINDOW) & (kpos >= 0) & (kpos < s)

    def scores(qb, kb):
        return jnp.einsum('bqkgd,bskd->bkgqs', qb, kb).astype(jnp.float32) * scale

    def sink_softmax(logits):
        snk = jnp.broadcast_to(sink, logits.shape[:-1] + (1,))
        return jax.nn.softmax(jnp.concatenate([logits, snk], -1), axis=-1)[..., :-1]

    def mix(p, vb):
        return jnp.einsum('bkgqs,bskd->bqkgd', p.astype(vb.dtype), vb)

    def attend_block(args):
        qb, kb, vb, mb = args
        s_lat = jnp.where(mb, scores(qb, kb), NEG_INF)
        p = sink_softmax(jnp.concatenate([s_lat, scores(qb, kc)], -1))
        return mix(p[..., :band_len], vb) + mix(p[..., band_len:], vc)

    qb = q.reshape((b, nb, Q_BLOCK) + q.shape[2:]).swapaxes(0, 1)
    o = lax.map(attend_block, (qb, band(k), band(v), mask))
    y_lat = o.swapaxes(0, 1).reshape(b, s, ODD_OUT) @ w_out
    if not ctx_out:
        return y_lat, None
    oc = mix(sink_softmax(scores(q_shape(qc), kc)), vc)
    y_ctx = oc.reshape(b, -1, ODD_OUT) @ w_out
    return y_lat, y_ctx


def clamped_swiglu(gate, lin):
    gate = jnp.minimum(gate, SWIGLU_LIMIT)
    lin = jnp.clip(lin, -SWIGLU_LIMIT, SWIGLU_LIMIT)
    return gate * jax.nn.sigmoid(SWIGLU_ALPHA * gate) * (lin + 1.0)


def moe(h, router_w, router_b, w_gu, b_gu, w_down, b_down):
    n, d = h.shape
    logits = (h @ router_w + router_b).astype(jnp.float32)
    top_v, top_i = lax.top_k(logits, TOP_K)
    gates = jax.nn.softmax(top_v, axis=-1)
    nk = n * TOP_K
    exp_ids = top_i.reshape(nk)
    tok_ids = jnp.repeat(jnp.arange(n, dtype=jnp.int32), TOP_K)
    order = jnp.argsort(exp_ids)
    sorted_e = exp_ids[order]
    counts = jnp.bincount(exp_ids, length=N_EXPERTS)
    padded = (counts + MOE_BLOCK - 1) // MOE_BLOCK * MOE_BLOCK
    pad_end = jnp.cumsum(padded)
    pad_start = pad_end - padded
    grp_start = jnp.cumsum(counts) - counts
    dest = pad_start[sorted_e] + jnp.arange(nk, dtype=jnp.int32) - grp_start[sorted_e]
    n_blocks = -(-nk // MOE_BLOCK) + N_EXPERTS
    n_rows = n_blocks * MOE_BLOCK
    row_tok = jnp.full((n_rows,), n, jnp.int32).at[dest].set(tok_ids[order])
    row_w = jnp.zeros((n_rows,), jnp.float32).at[dest].set(gates.reshape(nk)[order])
    blk_e = jnp.minimum(jnp.searchsorted(pad_end, jnp.arange(n_blocks) * MOE_BLOCK, side='right'),
                        N_EXPERTS - 1)
    h_pad = jnp.concatenate([h, jnp.zeros((1, d), h.dtype)], 0)

    def expert_block(args):
        rows, wts, e = args
        xb = h_pad[rows]
        gate, lin = jnp.split(xb @ w_gu[e] + b_gu[e], 2, axis=-1)
        y = clamped_swiglu(gate, lin) @ w_down[e] + b_down[e]
        return y * wts[:, None].astype(y.dtype)

    y_rows = lax.map(expert_block, (row_tok.reshape(n_blocks, MOE_BLOCK),
                                    row_w.reshape(n_blocks, MOE_BLOCK), blk_e))
    out = jax.ops.segment_sum(y_rows.reshape(n_rows, d), row_tok, num_segments=n + 1)
    return out[:n]


def setup_inputs(seed: int = 0) -> dict:
    key = jax.random.key(seed)
    ks = jax.random.split(key, 22)
    nrm = lambda k, shape, std: jax.random.normal(k, shape, jnp.float32) * std
    D = D_MODEL
    return {
        'x': nrm(ks[0], (BATCH, SEQ, D), 1.0),
        'c': nrm(ks[1], (BATCH, D), 1.0),
        'ctx': nrm(ks[2], (BATCH, CTX_LEN, D), 1.0),
        'c_ctx': nrm(ks[3], (D,), 1.0),
        'ada_w': nrm(ks[4], (DEPTH, D, 6 * D), 0.5 * D ** -0.5),
        'ada_b': nrm(ks[5], (DEPTH, 6 * D), 0.01),
        'ln_g': 1.0 + nrm(ks[6], (DEPTH, 2, D), 0.02),
        'ln_b': nrm(ks[7], (DEPTH, 2, D), 0.01),
        'ab_w_in': nrm(ks[8], (N_EVEN, D, EVEN_IN), D ** -0.5),
        'ab_w_out': nrm(ks[9], (N_EVEN, EVEN_OUT, D), EVEN_OUT ** -0.5 * DN_BETA),
        'diff_lambda': nrm(ks[10], (N_EVEN, 4, A_QK_DIM), 0.1),
        'diff_subln_g': 1.0 + nrm(ks[11], (N_EVEN, A_V_DIM), 0.02),
        'conv_w': nrm(ks[12], (N_EVEN, CONV_W, B_WIDTH), CONV_W ** -0.5),
        'c_w_in': nrm(ks[13], (N_ODD, D, ODD_IN), D ** -0.5),
        'c_w_out': nrm(ks[14], (N_ODD, ODD_OUT, D), ODD_OUT ** -0.5 * DN_BETA),
        'c_sink': nrm(ks[15], (N_ODD, C_HEADS), 1.0),
        'router_w': nrm(ks[16], (DEPTH, D, N_EXPERTS), D ** -0.5),
        'router_b': nrm(ks[17], (DEPTH, N_EXPERTS), 0.01),
        'w_gu': nrm(ks[18], (DEPTH, N_EXPERTS, D, 2 * D_FF), D ** -0.5),
        'b_gu': nrm(ks[19], (DEPTH, N_EXPERTS, 2 * D_FF), 0.01),
        'w_down': nrm(ks[20], (DEPTH, N_EXPERTS, D_FF, D), D_FF ** -0.5 * DN_BETA),
        'b_down': nrm(ks[21], (DEPTH, N_EXPERTS, D), 0.01),
    }


def reference(x, c, ctx, c_ctx, ada_w, ada_b, ln_g, ln_b, ab_w_in, ab_w_out, diff_lambda,
              diff_subln_g, conv_w, c_w_in, c_w_out, c_sink, router_w, router_b, w_gu, b_gu,
              w_down, b_down):
    b, s, d = x.shape
    tabs = rope_tables(s)
    xc = ctx
    for layer in range(DEPTH):
        last = layer == DEPTH - 1
        mod_l = (jax.nn.silu(c) @ ada_w[layer] + ada_b[layer])[:, None, :]
        mod_c = jax.nn.silu(c_ctx) @ ada_w[layer] + ada_b[layer]
        sh1, sc1, g1, sh2, sc2, g2 = jnp.split(mod_l, 6, axis=-1)
        csh1, csc1, cg1, csh2, csc2, cg2 = jnp.split(mod_c, 6, axis=-1)
        h_lat = x * (1.0 + sc1) + sh1
        h_ctx = xc * (1.0 + csc1) + csh1
        i = layer // 2
        if layer % 2 == 0:
            lam_init = 0.8 - 0.6 * math.exp(-0.3 * layer)
            y_lat, y_ctx = even_mixer(h_lat, h_ctx, ab_w_in[i], ab_w_out[i], diff_lambda[i],
                                      diff_subln_g[i], conv_w[i], lam_init, tabs, not last)
        else:
            y_lat, y_ctx = odd_mixer(h_lat, h_ctx, c_w_in[i], c_w_out[i], c_sink[i], tabs, not last)
        x = layer_norm(DN_ALPHA * x + g1 * y_lat, ln_g[layer, 0], ln_b[layer, 0])
        h_lat = x * (1.0 + sc2) + sh2
        moe_p = (router_w[layer], router_b[layer], w_gu[layer], b_gu[layer], w_down[layer], b_down[layer])
        if last:
            f_lat = moe(h_lat.reshape(b * s, d), *moe_p).reshape(b, s, d)
        else:
            xc = layer_norm(DN_ALPHA * xc + cg1 * y_ctx, ln_g[layer, 0], ln_b[layer, 0])
            h_ctx = xc * (1.0 + csc2) + csh2
            n_c = b * xc.shape[1]
            f = moe(jnp.concatenate([h_ctx.reshape(n_c, d), h_lat.reshape(b * s, d)], 0), *moe_p)
            f_lat = f[n_c:].reshape(b, s, d)
            xc = layer_norm(DN_ALPHA * xc + cg2 * f[:n_c].reshape(xc.shape), ln_g[layer, 1], ln_b[layer, 1])
        x = layer_norm(DN_ALPHA * x + g2 * f_lat, ln_g[layer, 1], ln_b[layer, 1])
    return x
```

```python
import functools
import math

import jax
import jax.numpy as jnp
from jax import lax
from jax.experimental import pallas as pl
from jax.experimental.pallas import tpu as pltpu

F32 = jnp.float32
BF16 = jnp.bfloat16
I32 = jnp.int32

D_MODEL = 1024
DEPTH = 2
GRID_W = 64
ROPE_DIM = 64
ROPE_BASE = 10000.0
A_HEADS = 4
A_QK_DIM = 64
A_V_DIM = 128
A_WIDTH = 512
B_WIDTH = 512
C_HEADS = 16
C_KV_HEADS = 2
C_HEAD_DIM = 64
C_WINDOW = 128
N_EXPERTS = 32
TOP_K = 4
D_FF = 1024
SWIGLU_ALPHA = 1.702
SWIGLU_LIMIT = 7.0
LN_EPS = 1e-5
RMS_EPS = 1e-5
NEG_INF = -1e30
DN_ALPHA = (2 * DEPTH) ** 0.25

LANES = 128
SUBLANES = 8
VMEM_LIMIT = 56 * 1024 * 1024

TM_PROJ = 512
TQ_DIFF = 512
TQ_WIN = 128
MOE_ROWS = 512
T_RANK = 512
T_DISPATCH = 256
T_COMBINE = 256


def _cparams(sem):
    return pltpu.CompilerParams(dimension_semantics=sem, vmem_limit_bytes=VMEM_LIMIT)


def _mod_kernel(c_ref, w_ref, b_ref, o_ref):
    c = c_ref[...]
    s = c * jax.nn.sigmoid(c)
    o_ref[...] = jnp.dot(s.astype(BF16), w_ref[...].astype(BF16),
                         preferred_element_type=F32) + b_ref[...]


def _modulation(cvec, ada_w, ada_b):
    d = D_MODEL
    tn = 1536
    return pl.pallas_call(
        _mod_kernel,
        grid=(DEPTH, 6 * d // tn),
        in_specs=[pl.BlockSpec((8, d), lambda l, j: (0, 0)),
                  pl.BlockSpec((None, d, tn), lambda l, j: (l, 0, j)),
                  pl.BlockSpec((None, 1, tn), lambda l, j: (l, 0, j))],
        out_specs=pl.BlockSpec((None, 8, tn), lambda l, j: (l, 0, j)),
        out_shape=jax.ShapeDtypeStruct((DEPTH, 8, 6 * d), F32),
        compiler_params=_cparams(("arbitrary", "arbitrary")),
        name="modulation",
    )(cvec, ada_w, ada_b.reshape(DEPTH, 1, 6 * d))


def _swap16_lanes(t):
    lane = lax.broadcasted_iota(I32, t.shape, 1)
    first = (lane % 32) < 16
    return jnp.where(first, pltpu.roll(t, LANES - 16, 1), pltpu.roll(t, 16, 1))


def _swap16_rows(a):
    pieces = []
    for i in range(0, a.shape[0], 32):
        pieces.append(a[i + 16:i + 32])
        pieces.append(a[i:i + 16])
    return jnp.concatenate(pieces, 0)


def _rope_tables(s):
    rows = jnp.repeat(jnp.arange(s // GRID_W, dtype=F32), GRID_W)
    cols = jnp.tile(jnp.arange(GRID_W, dtype=F32), s // GRID_W)
    axis_dim = ROPE_DIM // 2
    inv = ROPE_BASE ** (-jnp.arange(0, axis_dim, 2, dtype=F32) / axis_dim)
    ar, ac = rows[:, None] * inv, cols[:, None] * inv
    cr, sr, cc, sc = jnp.cos(ar), jnp.sin(ar), jnp.cos(ac), jnp.sin(ac)
    cos64 = jnp.concatenate([cr, cr, cc, cc], -1)
    sin64 = jnp.concatenate([-sr, sr, -sc, sc], -1)
    return cos64, sin64


def _even_inproj_kernel(x_ref, sc_ref, sh_ref, wn_ref, wt_ref, ck_ref, sk_ref, cq_ref, sq_ref,
                        qT_ref, k_ref, vT_ref, bg_ref, u_ref):
    tm = x_ref.shape[0]
    h = (x_ref[...] * (1.0 + sc_ref[...]) + sh_ref[...]).astype(BF16)
    yn = jnp.dot(h, wn_ref[...], preferred_element_type=F32)
    ck = ck_ref[...]
    sk = sk_ref[...]
    pieces = []
    for j in range(A_HEADS):
        t = yn[:, j * LANES:(j + 1) * LANES]
        pieces.append(t * ck + _swap16_lanes(t) * sk)
    k_ref[...] = jnp.concatenate(pieces, 1).astype(BF16)
    bg_ref[...] = yn[:, A_WIDTH:A_WIDTH + B_WIDTH]
    u_ref[...] = yn[:, A_WIDTH + B_WIDTH:A_WIDTH + 2 * B_WIDTH] * yn[:, A_WIDTH + 2 * B_WIDTH:]
    yt = lax.dot_general(wt_ref[...], h, (((1,), (1,)), ((), ())),
                         preferred_element_type=F32)
    qt = yt[:A_WIDTH]
    cq = jnp.concatenate([cq_ref[...]] * A_HEADS, 0)
    sq = jnp.concatenate([sq_ref[...]] * A_HEADS, 0)
    qT_ref[...] = (qt * cq + _swap16_rows(qt) * sq).astype(BF16).reshape(A_HEADS, A_V_DIM, tm)
    vT_ref[...] = yt[A_WIDTH:].astype(BF16).reshape(A_HEADS, A_V_DIM, tm)


def _even_inproj(x, sc, sh, mod_row, wn, wt, ck, sk, cq, sq, tm):
    b, s, d = x.shape
    nb = s // tm
    out_shape = (
        jax.ShapeDtypeStruct((b, A_HEADS, A_V_DIM, s), BF16),
        jax.ShapeDtypeStruct((b, s, A_WIDTH), BF16),
        jax.ShapeDtypeStruct((b, A_HEADS, nb, A_V_DIM, tm), BF16),
        jax.ShapeDtypeStruct((b, s, B_WIDTH), F32),
        jax.ShapeDtypeStruct((b, s, B_WIDTH), F32),
    )
    return pl.pallas_call(
        _even_inproj_kernel,
        grid=(b, nb),
        in_specs=[
            pl.BlockSpec((None, tm, d), lambda bi, i: (bi, i, 0)),
            pl.BlockSpec((None, 1, d), lambda bi, i: (mod_row(bi), 0, 0)),
            pl.BlockSpec((None, 1, d), lambda bi, i: (mod_row(bi), 0, 0)),
            pl.BlockSpec(wn.shape, lambda bi, i: (0, 0)),
            pl.BlockSpec(wt.shape, lambda bi, i: (0, 0)),
            pl.BlockSpec((tm, LANES), lambda bi, i: (i, 0)),
            pl.BlockSpec((tm, LANES), lambda bi, i: (i, 0)),
            pl.BlockSpec((LANES, tm), lambda bi, i: (0, i)),
            pl.BlockSpec((LANES, tm), lambda bi, i: (0, i)),
        ],
        out_specs=(
            pl.BlockSpec((None, A_HEADS, A_V_DIM, tm), lambda bi, i: (bi, 0, 0, i)),
            pl.BlockSpec((None, tm, A_WIDTH), lambda bi, i: (bi, i, 0)),
            pl.BlockSpec((None, A_HEADS, None, A_V_DIM, tm), lambda bi, i: (bi, 0, i, 0, 0)),
            pl.BlockSpec((None, tm, B_WIDTH), lambda bi, i: (bi, i, 0)),
            pl.BlockSpec((None, tm, B_WIDTH), lambda bi, i: (bi, i, 0)),
        ),
        out_shape=out_shape,
        compiler_params=_cparams(("parallel", "arbitrary")),
        name="even_inproj",
    )(x, sc, sh, wn, wt, ck, sk, cq, sq)


def _diff_attn_kernel(*refs, n_chunks, lam_init, with_lat):
    if with_lat:
        qT_ref, k_ref, vT_ref, kc_ref, vTc_ref, lamp_ref, g_ref, o_ref, acc1, acc2 = refs
    else:
        qT_ref, kc_ref, vTc_ref, lamp_ref, g_ref, o_ref, acc1, acc2 = refs
    tq = qT_ref.shape[1]
    qT = qT_ref[...]
    row = lax.broadcasted_iota(I32, qT.shape, 0)
    zero = jnp.zeros_like(qT)
    q1 = jnp.where(row < A_QK_DIM, qT, zero)
    q2 = jnp.where(row >= A_QK_DIM, qT, zero)
    acc1[...] = jnp.zeros_like(acc1)
    acc2[...] = jnp.zeros_like(acc2)

    def one_map(kc, vTc, q, m, l, acc):
        s = jnp.dot(kc, q, preferred_element_type=F32)
        m_new = jnp.maximum(m, jnp.max(s, axis=0, keepdims=True))
        alpha = jnp.exp(m - m_new)
        p = jnp.exp(s - m_new)
        l_new = alpha * l + jnp.sum(p, axis=0, keepdims=True)
        acc[...] = alpha * acc[...] + jnp.dot(vTc, p.astype(BF16), preferred_element_type=F32)
        return m_new, l_new

    def chunk(kc, vTc, carry):
        m1, l1, m2, l2 = carry
        m1, l1 = one_map(kc, vTc, q1, m1, l1, acc1)
        m2, l2 = one_map(kc, vTc, q2, m2, l2, acc2)
        return m1, l1, m2, l2

    neg = jnp.full((1, tq), -jnp.inf, F32)
    zer = jnp.zeros((1, tq), F32)
    carry = (neg, zer, neg, zer)
    if with_lat:
        tk = vT_ref.shape[2]

        def body(c, carry):
            start = pl.multiple_of(c * tk, tk)
            return chunk(k_ref[pl.ds(start, tk), :], vT_ref[c], carry)

        carry = lax.fori_loop(0, n_chunks, body, carry)
    m1, l1, m2, l2 = chunk(kc_ref[...], vTc_ref[...], carry)

    lf = lamp_ref[...]
    e1 = jnp.exp(jnp.sum(lf[0:1] * lf[1:2], axis=1, keepdims=True))
    e2 = jnp.exp(jnp.sum(lf[2:3] * lf[3:4], axis=1, keepdims=True))
    lam = e1 - e2 + lam_init
    o = acc1[...] / l1 - lam * (acc2[...] / l2)
    o = o * lax.rsqrt(jnp.mean(o * o, axis=0, keepdims=True) + RMS_EPS)
    o = o * g_ref[...] * (1.0 - lam_init)
    o_ref[...] = o.T.astype(BF16)


def _diff_attention(qT, k, vT, kc, vTc, lam_p, g_col, lam_init, tq):
    b, _, _, s = qT.shape
    lc = kc.shape[1]
    with_lat = k is not None
    in_specs = [pl.BlockSpec((None, None, A_V_DIM, tq), lambda bi, h, i: (bi, h, 0, i))]
    args = [qT]
    n_chunks = 0
    if with_lat:
        sk = k.shape[1]
        n_chunks, tk = vT.shape[2], vT.shape[4]
        in_specs += [pl.BlockSpec((None, sk, LANES), lambda bi, h, i: (bi, 0, h)),
                     pl.BlockSpec((None, None, n_chunks, A_V_DIM, tk), lambda bi, h, i: (bi, h, 0, 0, 0))]
        args += [k, vT]
    in_specs += [pl.BlockSpec((None, lc, LANES), lambda bi, h, i: (bi, 0, h)),
                 pl.BlockSpec((None, None, None, A_V_DIM, lc), lambda bi, h, i: (bi, h, 0, 0, 0)),
                 pl.BlockSpec(lam_p.shape, lambda bi, h, i: (0, 0)),
                 pl.BlockSpec(g_col.shape, lambda bi, h, i: (0, 0))]
    args += [kc, vTc, lam_p, g_col]
    return pl.pallas_call(
        functools.partial(_diff_attn_kernel, n_chunks=n_chunks, lam_init=lam_init, with_lat=with_lat),
        grid=(b, A_HEADS, s // tq),
        in_specs=in_specs,
        out_specs=pl.BlockSpec((None, tq, A_V_DIM), lambda bi, h, i: (bi, i, h)),
        out_shape=jax.ShapeDtypeStruct((b, s, A_WIDTH), BF16),
        scratch_shapes=[pltpu.VMEM((A_V_DIM, tq), F32), pltpu.VMEM((A_V_DIM, tq), F32)],
        compiler_params=_cparams(("parallel", "parallel", "arbitrary")),
        name="diff_attention_lat" if with_lat else "diff_attention_ctx",
    )(*args)


def _layer_norm(z, g, b):
    mu = jnp.mean(z, axis=-1, keepdims=True)
    zc = z - mu
    var = jnp.mean(zc * zc, axis=-1, keepdims=True)
    return zc * lax.rsqrt(var + LN_EPS) * g + b


def _top4_router(h2, rwT_ref, rb_ref, ids_ref, gates_ref):
    tm = h2.shape[0]
    lt = lax.dot_general(rwT_ref[...], h2.astype(BF16), (((1,), (1,)), ((), ())),
                         preferred_element_type=F32) + rb_ref[...]
    eidx = lax.broadcasted_iota(I32, lt.shape, 0)
    vals, ids = [], []
    for _ in range(TOP_K):
        mx = jnp.max(lt, axis=0, keepdims=True)
        idx = jnp.min(jnp.where(lt == mx, eidx, N_EXPERTS), axis=0, keepdims=True)
        vals.append(mx)
        ids.append(idx)
        lt = jnp.where(eidx == idx, -jnp.inf, lt)
    es = [jnp.exp(v - vals[0]) for v in vals]
    den = es[0] + es[1] + es[2] + es[3]
    ids_ref[...] = jnp.concatenate(ids, 0)
    gates_ref[...] = jnp.concatenate([e / den for e in es], 0)
    del tm


def _outproj_kernel(*refs, has_conv):
    if has_conv:
        (a_ref, bg_ref, u_ref, up_ref, un_ref, cw_ref, wa_ref, wb_ref, x_ref, g1_ref, lng_ref, lnb_ref,
         sc2_ref, sh2_ref, rwT_ref, rb_ref, x1_ref, h2_ref, ids_ref, gates_ref) = refs
    else:
        (a_ref, wa_ref, x_ref, g1_ref, lng_ref, lnb_ref,
         sc2_ref, sh2_ref, rwT_ref, rb_ref, x1_ref, h2_ref, ids_ref, gates_ref) = refs
    y = jnp.dot(a_ref[...], wa_ref[...], preferred_element_type=F32)
    if has_conv:
        tm = u_ref.shape[0]
        i = pl.program_id(1)
        nb = pl.num_programs(1)
        u = u_ref[...]
        prev = jnp.where(i > 0, up_ref[SUBLANES - 1:SUBLANES, :], 0.0)
        nxt = jnp.where(i < nb - 1, un_ref[0:1, :], 0.0)
        r = lax.broadcasted_iota(I32, u.shape, 0)
        um1 = jnp.where(r == 0, prev, pltpu.roll(u, 1, 0))
        up1 = jnp.where(r == tm - 1, nxt, pltpu.roll(u, tm - 1, 0))
        cw = cw_ref[...]
        conv = um1 * cw[0:1] + u * cw[1:2] + up1 * cw[2:3]
        b_mix = (bg_ref[...] * conv).astype(BF16)
        y = y + jnp.dot(b_mix, wb_ref[...], preferred_element_type=F32)
    x1 = _layer_norm(DN_ALPHA * x_ref[...] + g1_ref[...] * y, lng_ref[...], lnb_ref[...])
    x1_ref[...] = x1
    h2 = x1 * (1.0 + sc2_ref[...]) + sh2_ref[...]
    h2_ref[...] = h2
    _top4_router(h2, rwT_ref, rb_ref, ids_ref, gates_ref)


def _outproj(a, conv_in, wa, wb, x, mod_vecs, mod_row, ln_g, ln_b, rwT, rb, tm, tok_off, prev_out):
    b, s, d = x.shape
    nb = s // tm
    g1, sc2, sh2 = mod_vecs
    has_conv = conv_in is not None
    n_tok = prev_out[0].shape[0]
    ob = tok_off // tm

    def row_spec(width):
        return pl.BlockSpec((None, tm, width), lambda bi, i: (bi, i, 0))

    def vec_spec():
        return pl.BlockSpec((None, 1, d), lambda bi, i: (mod_row(bi), 0, 0))

    def full(arr):
        return pl.BlockSpec(arr.shape, lambda bi, i: (0,) * arr.ndim)

    in_specs = [row_spec(a.shape[-1])]
    args = [a]
    if has_conv:
        bg, u, cw = conv_in
        s8 = s // SUBLANES
        t8 = tm // SUBLANES
        in_specs += [row_spec(B_WIDTH), row_spec(B_WIDTH),
                     pl.BlockSpec((None, SUBLANES, B_WIDTH),
                                  lambda bi, i: (bi, jnp.maximum(i * t8 - 1, 0), 0)),
                     pl.BlockSpec((None, SUBLANES, B_WIDTH),
                                  lambda bi, i: (bi, jnp.minimum((i + 1) * t8, s8 - 1), 0)),
                     full(cw), full(wa), full(wb)]
        args += [bg, u, u, u, cw, wa, wb]
    else:
        in_specs += [full(wa)]
        args += [wa]
    in_specs += [row_spec(d), vec_spec(), full(ln_g), full(ln_b), vec_spec(), vec_spec(), full(rwT), full(rb)]
    args += [x, g1, ln_g, ln_b, sc2, sh2, rwT, rb]
    n_in = len(args)
    in_specs += [pl.BlockSpec(memory_space=pl.ANY)] * 3
    args += list(prev_out)
    out_specs = (
        row_spec(d),
        pl.BlockSpec((tm, d), lambda bi, i: (ob + bi * nb + i, 0)),
        pl.BlockSpec((TOP_K, tm), lambda bi, i: (0, ob + bi * nb + i)),
        pl.BlockSpec((TOP_K, tm), lambda bi, i: (0, ob + bi * nb + i)),
    )
    out_shape = (
        jax.ShapeDtypeStruct((b, s, d), F32),
        jax.ShapeDtypeStruct((n_tok, d), F32),
        jax.ShapeDtypeStruct((TOP_K, n_tok), I32),
        jax.ShapeDtypeStruct((TOP_K, n_tok), F32),
    )

    def kern(*refs):
        refs = refs[:n_in] + refs[n_in + 3:]
        _outproj_kernel(*refs, has_conv=has_conv)

    return pl.pallas_call(
        kern,
        grid=(b, nb),
        in_specs=in_specs,
        out_specs=out_specs,
        out_shape=out_shape,
        input_output_aliases={n_in: 1, n_in + 1: 2, n_in + 2: 3},
        compiler_params=_cparams(("parallel", "arbitrary")),
        name="outproj_conv" if has_conv else "outproj",
    )(*args)


def _rank_kernel(ids_ref, rank_ref, cnt_ref, run_ref):
    i = pl.program_id(0)
    tr = ids_ref.shape[1]

    @pl.when(i == 0)
    def _():
        run_ref[...] = jnp.zeros_like(run_ref)

    ids = ids_ref[...]
    eidx = lax.broadcasted_iota(I32, (N_EXPERTS, tr), 0)
    hits = [eidx == ids[k:k + 1] for k in range(TOP_K)]
    member = (hits[0] | hits[1] | hits[2] | hits[3]).astype(F32)
    r = lax.broadcasted_iota(I32, (tr, tr), 0)
    c = lax.broadcasted_iota(I32, (tr, tr), 1)
    upper = (r < c).astype(BF16)
    prefix = jnp.dot(member.astype(BF16), upper, preferred_element_type=F32)
    base = run_ref[:, 0:1] + prefix
    ranks = [jnp.sum(jnp.where(hits[k], base, 0.0), axis=0, keepdims=True) for k in range(TOP_K)]
    rank_ref[...] = jnp.concatenate(ranks, 0).astype(I32)
    run_ref[...] = run_ref[...] + jnp.sum(member, axis=1, keepdims=True)
    cnt_ref[...] = run_ref[...]


def _expert_ranks(ids):
    n = ids.shape[1]
    return pl.pallas_call(
        _rank_kernel,
        grid=(n // T_RANK,),
        in_specs=[pl.BlockSpec((TOP_K, T_RANK), lambda i: (0, i))],
        out_specs=(pl.BlockSpec((TOP_K, T_RANK), lambda i: (0, i)),
                   pl.BlockSpec((N_EXPERTS, LANES), lambda i: (0, 0))),
        out_shape=(jax.ShapeDtypeStruct((TOP_K, n), I32),
                   jax.ShapeDtypeStruct((N_EXPERTS, LANES), F32)),
        scratch_shapes=[pltpu.VMEM((N_EXPERTS, LANES), F32)],
        compiler_params=_cparams(("arbitrary",)),
        name="expert_ranks",
    )(ids)


def _dest_kernel(ps_ref, ids_ref, rank_ref, dest_ref):
    ids = ids_ref[...]
    acc = rank_ref[...]
    for e in range(N_EXPERTS):
        acc = acc + jnp.where(ids == e, ps_ref[e], 0)
    dest_ref[...] = acc


def _destinations(pad_start, ids, rank):
    n = ids.shape[1]
    tn = 2048 if n % 2048 == 0 else 512
    return pl.pallas_call(
        _dest_kernel,
        grid_spec=pltpu.PrefetchScalarGridSpec(
            num_scalar_prefetch=1,
            grid=(n // tn,),
            in_specs=[pl.BlockSpec((TOP_K, tn), lambda i, ps: (0, i)),
                      pl.BlockSpec((TOP_K, tn), lambda i, ps: (0, i))],
            out_specs=pl.BlockSpec((TOP_K, tn), lambda i, ps: (0, i))),
        out_shape=jax.ShapeDtypeStruct((TOP_K, n), I32),
        compiler_params=_cparams(("arbitrary",)),
        name="destinations",
    )(pad_start, ids, rank)


def _dispatch_kernel(dest_ref, h_hbm, xs_in, xs_hbm, sem):
    del xs_in
    i = pl.program_id(0)
    td = dest_ref.shape[1]

    def body(r, carry):
        t = i * td + r
        for k in range(TOP_K):
            d = dest_ref[k, r]
            pltpu.make_async_copy(h_hbm.at[pl.ds(t, 1)], xs_hbm.at[pl.ds(d, 1)], sem).start()
        return carry

    lax.fori_loop(0, td, body, 0)
    pltpu.make_async_copy(h_hbm.at[pl.ds(0, TOP_K * td)], xs_hbm.at[pl.ds(0, TOP_K * td)], sem).wait()


def _dispatch(dest, h2, n_rows):
    n, d = h2.shape
    xs0 = jnp.zeros((n_rows, d), h2.dtype)
    return pl.pallas_call(
        _dispatch_kernel,
        grid=(n // T_DISPATCH,),
        in_specs=[pl.BlockSpec((TOP_K, T_DISPATCH), lambda i: (0, i), memory_space=pltpu.SMEM),
                  pl.BlockSpec(memory_space=pl.ANY),
                  pl.BlockSpec(memory_space=pl.ANY)],
        out_specs=pl.BlockSpec(memory_space=pl.ANY),
        out_shape=jax.ShapeDtypeStruct((n_rows, d), h2.dtype),
        input_output_aliases={2: 0},
        scratch_shapes=[pltpu.SemaphoreType.DMA(())],
        compiler_params=pltpu.CompilerParams(dimension_semantics=("arbitrary",),
                                             has_side_effects=True),
        name="dispatch",
    )(dest, h2, xs0)


def _expert_kernel(be_ref, nu_ref, xs_ref, wgu_ref, bgu_ref, wd_ref, bd_ref, ys_ref):
    i = pl.program_id(0)

    @pl.when(i < nu_ref[0])
    def _():
        x = xs_ref[...].astype(BF16)
        gu = jnp.dot(x, wgu_ref[...], preferred_element_type=F32) + bgu_ref[...]
        gate = jnp.minimum(gu[:, :D_FF], SWIGLU_LIMIT)
        lin = jnp.clip(gu[:, D_FF:], -SWIGLU_LIMIT, SWIGLU_LIMIT)
        act = gate * jax.nn.sigmoid(SWIGLU_ALPHA * gate) * (lin + 1.0)
        ys_ref[...] = jnp.dot(act.astype(BF16), wd_ref[...], preferred_element_type=F32) + bd_ref[...]

    @pl.when(i >= nu_ref[0])
    def _():
        ys_ref[...] = jnp.zeros_like(ys_ref)


def _expert_mlp(blk_e, n_used, xs, w_gu, b_gu, w_down, b_down):
    n_rows, d = xs.shape
    n_blocks = n_rows // MOE_ROWS

    def blk(i, nu):
        return jnp.minimum(i, nu[0] - 1)

    return pl.pallas_call(
        _expert_kernel,
        grid_spec=pltpu.PrefetchScalarGridSpec(
            num_scalar_prefetch=2,
            grid=(n_blocks,),
            in_specs=[
                pl.BlockSpec((MOE_ROWS, d), lambda i, be, nu: (blk(i, nu), 0)),
                pl.BlockSpec((None, d, 2 * D_FF), lambda i, be, nu: (be[blk(i, nu)], 0, 0)),
                pl.BlockSpec((None, 1, 2 * D_FF), lambda i, be, nu: (be[blk(i, nu)], 0, 0)),
                pl.BlockSpec((None, D_FF, d), lambda i, be, nu: (be[blk(i, nu)], 0, 0)),
                pl.BlockSpec((None, 1, d), lambda i, be, nu: (be[blk(i, nu)], 0, 0)),
            ],
            out_specs=pl.BlockSpec((MOE_ROWS, d), lambda i, be, nu: (i, 0))),
        out_shape=jax.ShapeDtypeStruct((n_rows, d), F32),
        compiler_params=_cparams(("arbitrary",)),
        name="expert_mlp",
    )(blk_e, n_used, xs, w_gu, b_gu.reshape(N_EXPERTS, 1, 2 * D_FF), w_down,
      b_down.reshape(N_EXPERTS, 1, d))


def _combine_kernel(dest_ref, gates_ref, ys_hbm, x1_ref, g2_ref, lng_ref, lnb_ref, out_ref, buf, sem):
    tc = dest_ref.shape[1]

    def body(r, carry):
        for k in range(TOP_K):
            d = dest_ref[k, r]
            pltpu.make_async_copy(ys_hbm.at[pl.ds(d, 1)], buf.at[k, pl.ds(r, 1)], sem).start()
        return carry

    lax.fori_loop(0, tc, body, 0)
    for k in range(TOP_K):
        pltpu.make_async_copy(ys_hbm.at[pl.ds(0, tc)], buf.at[k], sem).wait()
    gpad = jnp.concatenate([gates_ref[...], jnp.zeros((LANES - TOP_K, tc), F32)], 0)
    gt = gpad.T
    f = buf[0] * gt[:, 0:1]
    for k in range(1, TOP_K):
        f = f + buf[k] * gt[:, k:k + 1]
    out_ref[...] = _layer_norm(DN_ALPHA * x1_ref[...] + g2_ref[...] * f, lng_ref[...], lnb_ref[...])


def _combine(dest, gates, ys, x1, g2, row_of_block, ln_g, ln_b):
    n, d = x1.shape
    tc = T_COMBINE
    return pl.pallas_call(
        _combine_kernel,
        grid=(n // tc,),
        in_specs=[pl.BlockSpec((TOP_K, tc), lambda i: (0, i), memory_space=pltpu.SMEM),
                  pl.BlockSpec((TOP_K, tc), lambda i: (0, i)),
                  pl.BlockSpec(memory_space=pl.ANY),
                  pl.BlockSpec((tc, d), lambda i: (i, 0)),
                  pl.BlockSpec((None, 1, d), lambda i: (row_of_block(i), 0, 0)),
                  pl.BlockSpec(ln_g.shape, lambda i: (0, 0)),
                  pl.BlockSpec(ln_b.shape, lambda i: (0, 0))],
        out_specs=pl.BlockSpec((tc, d), lambda i: (i, 0)),
        out_shape=jax.ShapeDtypeStruct((n, d), F32),
        scratch_shapes=[pltpu.VMEM((TOP_K, tc, d), F32), pltpu.SemaphoreType.DMA(())],
        compiler_params=_cparams(("arbitrary",)),
        name="combine",
    )(dest, gates, ys, x1, g2, ln_g, ln_b)


def _moe(h2, ids, gates, x1, g2, row_of_block, ln_g, ln_b, w_gu, b_gu, w_down, b_down):
    n = h2.shape[0]
    nk = n * TOP_K
    n_blocks = -(-nk // MOE_ROWS) + N_EXPERTS
    rank, cnt = _expert_ranks(ids)
    counts = cnt[:, 0].astype(I32)
    padded = (counts + MOE_ROWS - 1) // MOE_ROWS * MOE_ROWS
    pad_end = jnp.cumsum(padded)
    pad_start = (pad_end - padded).astype(I32)
    blk_e = jnp.minimum(jnp.searchsorted(pad_end, jnp.arange(n_blocks, dtype=I32) * MOE_ROWS, side='right'),
                        N_EXPERTS - 1).astype(I32)
    n_used = (pad_end[-1:] // MOE_ROWS).astype(I32)
    dest = _destinations(pad_start, ids, rank)
    xs = _dispatch(dest, h2, n_blocks * MOE_ROWS)
    ys = _expert_mlp(blk_e, n_used, xs, w_gu.astype(BF16), b_gu, w_down.astype(BF16), b_down)
    return _combine(dest, gates, ys, x1, g2, row_of_block, ln_g, ln_b)


def _odd_inproj_kernel(x_ref, sc_ref, sh_ref, w_ref, c_ref, s_ref, q_ref, k_ref, v_ref):
    h = (x_ref[...] * (1.0 + sc_ref[...]) + sh_ref[...]).astype(BF16)
    y = jnp.dot(h, w_ref[...], preferred_element_type=F32)
    cs = c_ref[...]
    sn = s_ref[...]
    nq = C_HEADS * C_HEAD_DIM // LANES
    pieces = []
    for j in range(nq + 1):
        t = y[:, j * LANES:(j + 1) * LANES]
        if j < nq:
            pieces.append((t * cs + _swap16_lanes(t) * sn) * (C_HEAD_DIM ** -0.5))
        else:
            pieces.append(t * cs + _swap16_lanes(t) * sn)
    q_ref[...] = jnp.concatenate(pieces[:nq], 1).astype(BF16)
    k_ref[...] = pieces[nq].astype(BF16)
    v_ref[...] = y[:, (nq + 1) * LANES:].astype(BF16)


def _odd_inproj(x, sc, sh, mod_row, w, cs, sn, tm):
    b, s, d = x.shape
    qw = C_HEADS * C_HEAD_DIM
    return pl.pallas_call(
        _odd_inproj_kernel,
        grid=(b, s // tm),
        in_specs=[pl.BlockSpec((None, tm, d), lambda bi, i: (bi, i, 0)),
                  pl.BlockSpec((None, 1, d), lambda bi, i: (mod_row(bi), 0, 0)),
                  pl.BlockSpec((None, 1, d), lambda bi, i: (mod_row(bi), 0, 0)),
                  pl.BlockSpec(w.shape, lambda bi, i: (0, 0)),
                  pl.BlockSpec((tm, LANES), lambda bi, i: (i, 0)),
                  pl.BlockSpec((tm, LANES), lambda bi, i: (i, 0))],
        out_specs=(pl.BlockSpec((None, tm, qw), lambda bi, i: (bi, i, 0)),
                   pl.BlockSpec((None, tm, LANES), lambda bi, i: (bi, i, 0)),
                   pl.BlockSpec((None, tm, LANES), lambda bi, i: (bi, i, 0))),
        out_shape=(jax.ShapeDtypeStruct((b, s, qw), BF16),
                   jax.ShapeDtypeStruct((b, s, LANES), BF16),
                   jax.ShapeDtypeStruct((b, s, LANES), BF16)),
        compiler_params=_cparams(("parallel", "arbitrary")),
        name="odd_inproj",
    )(x, sc, sh, w, cs, sn)


def _win_attn_kernel(q_ref, kp_ref, k0_ref, kn_ref, kc_ref, vp_ref, v0_ref, vn_ref, vc_ref, sink_ref, o_ref):
    tq = q_ref.shape[0]
    i = pl.program_id(1)
    nb = pl.num_programs(1)
    q = q_ref[...]
    lane = lax.broadcasted_iota(I32, (tq, LANES), 1)
    lo = lane < C_HEAD_DIM
    zero = jnp.zeros((tq, LANES), BF16)
    n_pairs = C_HEADS // C_KV_HEADS
    rows = []
    for j in range(n_pairs):
        t = q[:, j * LANES:(j + 1) * LANES]
        rows.append(jnp.where(lo, t, zero))
        rows.append(jnp.where(lo, zero, t))
    qs = jnp.concatenate(rows, 0)
    kb = jnp.concatenate([kp_ref[...], k0_ref[...], kn_ref[...], kc_ref[...]], 0)
    vb = jnp.concatenate([vp_ref[...], v0_ref[...], vn_ref[...], vc_ref[...]], 0)
    nk = kb.shape[0]
    s = lax.dot_general(qs, kb, (((1,), (1,)), ((), ())), preferred_element_type=F32)
    s = s.reshape(C_HEADS, tq, nk)
    qpos = lax.broadcasted_iota(I32, (tq, nk), 0)
    col = lax.broadcasted_iota(I32, (tq, nk), 1)
    ok = (jnp.abs(col - tq - qpos) <= C_WINDOW)
    ok = ok & ((col >= tq) | (i > 0)) & ((col < 2 * tq) | (i < nb - 1))
    ok = ok | (col >= 3 * tq)
    s = jnp.where(ok[None], s, NEG_INF)
    sink = sink_ref[...]
    m = jnp.maximum(jnp.max(s, axis=-1, keepdims=True), sink)
    p = jnp.exp(s - m)
    den = jnp.sum(p, axis=-1, keepdims=True) + jnp.exp(sink - m)
    p = (p / den).astype(BF16).reshape(C_HEADS * tq, nk)
    o = jnp.dot(p, vb, preferred_element_type=F32)
    outs = []
    for j in range(n_pairs):
        a = o[(2 * j) * tq:(2 * j + 1) * tq]
        b = o[(2 * j + 1) * tq:(2 * j + 2) * tq]
        outs.append(jnp.where(lo, a, b))
    o_ref[...] = jnp.concatenate(outs, 1).astype(BF16)


def _win_attention(q, k, v, kc, vc, sink):
    b, s, qw = q.shape
    tq = TQ_WIN
    nb = s // tq
    lc = kc.shape[1]

    def blk(off):
        return pl.BlockSpec((None, tq, LANES), lambda bi, i: (bi, jnp.clip(i + off, 0, nb - 1), 0))

    ctx = pl.BlockSpec((None, lc, LANES), lambda bi, i: (bi, 0, 0))
    return pl.pallas_call(
        _win_attn_kernel,
        grid=(b, nb),
        in_specs=[pl.BlockSpec((None, tq, qw), lambda bi, i: (bi, i, 0)),
                  blk(-1), blk(0), blk(1), ctx, blk(-1), blk(0), blk(1), ctx,
                  pl.BlockSpec(sink.shape, lambda bi, i: (0, 0, 0))],
        out_specs=pl.BlockSpec((None, tq, qw), lambda bi, i: (bi, i, 0)),
        out_shape=jax.ShapeDtypeStruct((b, s, qw), BF16),
        compiler_params=_cparams(("parallel", "arbitrary")),
        name="window_attention",
    )(q, k, k, k, kc, v, v, v, vc, sink)


def kernel(x, c, ctx, c_ctx, ada_w, ada_b, ln_g, ln_b, ab_w_in, ab_w_out, diff_lambda, diff_subln_g,
           conv_w, c_w_in, c_w_out, c_sink, router_w, router_b, w_gu, b_gu, w_down, b_down):
    b, s, d = x.shape
    lc = ctx.shape[1]
    n_c = b * lc
    ctx_row = b

    cvec = jnp.zeros((8, d), F32).at[:b].set(c).at[b].set(c_ctx)
    mod = _modulation(cvec, ada_w, ada_b)

    def mod_vec(layer, j):
        return mod[layer, :, j * d:(j + 1) * d].reshape(8, 1, d)

    lat_row = lambda bi: bi
    ctx_rowf = lambda bi: ctx_row

    cos64, sin64 = _rope_tables(s)
    cos_l = jnp.concatenate([cos64, cos64], -1)
    sin_l = jnp.concatenate([sin64, sin64], -1)
    one_c = jnp.ones((lc, LANES), F32)
    zero_c = jnp.zeros((lc, LANES), F32)
    qscale = A_QK_DIM ** -0.5

    l = 0
    w_in = ab_w_in[0]
    wn = w_in[:, A_WIDTH:].astype(BF16)
    wn = jnp.concatenate([wn[:, :A_WIDTH], wn[:, 2 * A_WIDTH:]], 1)
    wt = jnp.concatenate([w_in[:, :A_WIDTH], w_in[:, 2 * A_WIDTH:3 * A_WIDTH]], 1).T.astype(BF16)
    sc1, sh1, g1 = mod_vec(l, 1), mod_vec(l, 0), mod_vec(l, 2)
    sh2, sc2, g2 = mod_vec(l, 3), mod_vec(l, 4), mod_vec(l, 5)
    lam_init = 0.8 - 0.6 * math.exp(-0.3 * l)

    qT, k, vT, bg, u = _even_inproj(x, sc1, sh1, lat_row, wn, wt, cos_l, sin_l,
                                    (cos_l * qscale).T, (sin_l * qscale).T, TM_PROJ)
    qTc, kc, vTc, bgc, uc = _even_inproj(ctx, sc1, sh1, ctx_rowf, wn, wt, one_c, zero_c,
                                         (one_c * qscale).T, zero_c.T, lc)
    g_col = diff_subln_g[0].reshape(A_V_DIM, 1)
    a_lat = _diff_attention(qT, k, vT, kc, vTc, diff_lambda[0], g_col, lam_init, TQ_DIFF)
    a_ctx = _diff_attention(qTc, None, None, kc, vTc, diff_lambda[0], g_col, lam_init, lc)

    w_out = ab_w_out[0].astype(BF16)
    wa, wb = w_out[:A_WIDTH], w_out[A_WIDTH:]
    lng0, lnb0 = ln_g[l, 0].reshape(1, d), ln_b[l, 0].reshape(1, d)
    lng1, lnb1 = ln_g[l, 1].reshape(1, d), ln_b[l, 1].reshape(1, d)
    rwT = router_w[l].T.astype(BF16)
    rb = router_b[l].reshape(N_EXPERTS, 1)
    n0 = n_c + b * s
    empty = (jnp.zeros((n0, d), F32), jnp.zeros((TOP_K, n0), I32), jnp.zeros((TOP_K, n0), F32))
    xc1, h2a, idsa, gatesa = _outproj(a_ctx, (bgc, uc, conv_w[0]), wa, wb, ctx, (g1, sc2, sh2), ctx_rowf,
                                      lng0, lnb0, rwT, rb, lc, 0, empty)
    x1, h2a, idsa, gatesa = _outproj(a_lat, (bg, u, conv_w[0]), wa, wb, x, (g1, sc2, sh2), lat_row,
                                     lng0, lnb0, rwT, rb, TM_PROJ, n_c, (h2a, idsa, gatesa))
    x1_all = jnp.concatenate([xc1.reshape(n_c, d), x1.reshape(b * s, d)], 0)
    ncb = n_c // T_COMBINE
    spb = s // T_COMBINE
    row_of_block0 = lambda i: jnp.where(i < ncb, ctx_row, jnp.maximum(i - ncb, 0) // spb)
    y_all = _moe(h2a, idsa, gatesa, x1_all, g2, row_of_block0, lng1, lnb1,
                 w_gu[l], b_gu[l], w_down[l], b_down[l])
    xc = y_all[:n_c].reshape(b, lc, d)
    x = y_all[n_c:].reshape(b, s, d)

    l = 1
    sc1, sh1, g1 = mod_vec(l, 1), mod_vec(l, 0), mod_vec(l, 2)
    sh2, sc2, g2 = mod_vec(l, 3), mod_vec(l, 4), mod_vec(l, 5)
    g = C_HEADS // C_KV_HEADS
    perm = jnp.array([(kv * g + j) * C_HEAD_DIM + dd for j in range(g) for kv in range(C_KV_HEADS)
                      for dd in range(C_HEAD_DIM)], I32)
    w_in = c_w_in[0]
    qw = C_HEADS * C_HEAD_DIM
    w_odd = jnp.concatenate([w_in[:, :qw][:, perm], w_in[:, qw:]], 1).astype(BF16)
    q, k, v = _odd_inproj(x, sc1, sh1, lat_row, w_odd, cos_l, sin_l, TM_PROJ)
    _, kc, vc = _odd_inproj(xc, sc1, sh1, ctx_rowf, w_odd, one_c, zero_c, lc)
    sink = c_sink[0][perm[::C_HEAD_DIM] // C_HEAD_DIM].reshape(C_HEADS, 1, 1)
    o = _win_attention(q, k, v, kc, vc, sink)
    wa = c_w_out[0][perm].astype(BF16)
    lng0, lnb0 = ln_g[l, 0].reshape(1, d), ln_b[l, 0].reshape(1, d)
    lng1, lnb1 = ln_g[l, 1].reshape(1, d), ln_b[l, 1].reshape(1, d)
    rwT = router_w[l].T.astype(BF16)
    rb = router_b[l].reshape(N_EXPERTS, 1)
    n1 = b * s
    empty = (jnp.zeros((n1, d), F32), jnp.zeros((TOP_K, n1), I32), jnp.zeros((TOP_K, n1), F32))
    x1, h2a, idsa, gatesa = _outproj(o, None, wa, None, x, (g1, sc2, sh2), lat_row,
                                     lng0, lnb0, rwT, rb, TM_PROJ, 0, empty)
    row_of_block1 = lambda i: i // spb
    y = _moe(h2a, idsa, gatesa, x1.reshape(n1, d), g2, row_of_block1, lng1, lnb1,
             w_gu[l], b_gu[l], w_down[l], b_down[l])
    return y.reshape(b, s, d)
```

```python
import functools
import math

import jax
import jax.numpy as jnp
from jax import lax
from jax.experimental import pallas as pl
from jax.experimental.pallas import tpu as pltpu

F32 = jnp.float32
BF16 = jnp.bfloat16
I32 = jnp.int32

D_MODEL = 1024
DEPTH = 2
GRID_W = 64
ROPE_DIM = 64
ROPE_BASE = 10000.0
A_HEADS = 4
A_QK_DIM = 64
A_V_DIM = 128
A_WIDTH = 512
B_WIDTH = 512
C_HEADS = 16
C_KV_HEADS = 2
C_HEAD_DIM = 64
C_WINDOW = 128
N_EXPERTS = 32
TOP_K = 4
D_FF = 1024
SWIGLU_ALPHA = 1.702
SWIGLU_LIMIT = 7.0
LN_EPS = 1e-5
RMS_EPS = 1e-5
NEG_INF = -1e30
DN_ALPHA = (2 * DEPTH) ** 0.25

LANES = 128
SUBLANES = 8
VMEM_LIMIT = 56 * 1024 * 1024

TM_PROJ = 512
TQ_DIFF = 512
TQ_WIN = 128
MOE_ROWS = 512
T_RANK = 512
T_DISPATCH = 256
T_COMBINE = 256


def _cparams(sem):
    return pltpu.CompilerParams(dimension_semantics=sem, vmem_limit_bytes=VMEM_LIMIT)


def _mod_kernel(c_ref, w_ref, b_ref, o_ref):
    c = c_ref[...]
    s = c * jax.nn.sigmoid(c)
    o_ref[...] = jnp.dot(s.astype(BF16), w_ref[...].astype(BF16),
                         preferred_element_type=F32) + b_ref[...]


def _modulation(cvec, ada_w, ada_b):
    d = D_MODEL
    tn = 1536
    return pl.pallas_call(
        _mod_kernel,
        grid=(DEPTH, 6 * d // tn),
        in_specs=[pl.BlockSpec((8, d), lambda l, j: (0, 0)),
                  pl.BlockSpec((None, d, tn), lambda l, j: (l, 0, j)),
                  pl.BlockSpec((None, 1, tn), lambda l, j: (l, 0, j))],
        out_specs=pl.BlockSpec((None, 8, tn), lambda l, j: (l, 0, j)),
        out_shape=jax.ShapeDtypeStruct((DEPTH, 8, 6 * d), F32),
        compiler_params=_cparams(("arbitrary", "arbitrary")),
        name="modulation",
    )(cvec, ada_w, ada_b.reshape(DEPTH, 1, 6 * d))


def _swap16_lanes(t):
    lane = lax.broadcasted_iota(I32, t.shape, 1)
    first = (lane % 32) < 16
    return jnp.where(first, pltpu.roll(t, LANES - 16, 1), pltpu.roll(t, 16, 1))


def _swap16_rows(a):
    pieces = []
    for i in range(0, a.shape[0], 32):
        pieces.append(a[i + 16:i + 32])
        pieces.append(a[i:i + 16])
    return jnp.concatenate(pieces, 0)


def _rope_tables(s):
    rows = jnp.repeat(jnp.arange(s // GRID_W, dtype=F32), GRID_W)
    cols = jnp.tile(jnp.arange(GRID_W, dtype=F32), s // GRID_W)
    axis_dim = ROPE_DIM // 2
    inv = ROPE_BASE ** (-jnp.arange(0, axis_dim, 2, dtype=F32) / axis_dim)
    ar, ac = rows[:, None] * inv, cols[:, None] * inv
    cr, sr, cc, sc = jnp.cos(ar), jnp.sin(ar), jnp.cos(ac), jnp.sin(ac)
    cos64 = jnp.concatenate([cr, cr, cc, cc], -1)
    sin64 = jnp.concatenate([-sr, sr, -sc, sc], -1)
    return cos64, sin64


def _even_inproj_kernel(x_ref, sc_ref, sh_ref, wn_ref, wt_ref, ck_ref, sk_ref, cq_ref, sq_ref,
                        qT_ref, k_ref, vT_ref, bg_ref, u_ref):
    tm = x_ref.shape[0]
    h = (x_ref[...] * (1.0 + sc_ref[...]) + sh_ref[...]).astype(BF16)
    yn = jnp.dot(h, wn_ref[...], preferred_element_type=F32)
    ck = ck_ref[...]
    sk = sk_ref[...]
    pieces = []
    for j in range(A_HEADS):
        t = yn[:, j * LANES:(j + 1) * LANES]
        pieces.append(t * ck + _swap16_lanes(t) * sk)
    k_ref[...] = jnp.concatenate(pieces, 1).astype(BF16)
    bg_ref[...] = yn[:, A_WIDTH:A_WIDTH + B_WIDTH]
    u_ref[...] = yn[:, A_WIDTH + B_WIDTH:A_WIDTH + 2 * B_WIDTH] * yn[:, A_WIDTH + 2 * B_WIDTH:]
    yt = lax.dot_general(wt_ref[...], h, (((1,), (1,)), ((), ())),
                         preferred_element_type=F32)
    qt = yt[:A_WIDTH]
    cq = jnp.concatenate([cq_ref[...]] * A_HEADS, 0)
    sq = jnp.concatenate([sq_ref[...]] * A_HEADS, 0)
    qT_ref[...] = (qt * cq + _swap16_rows(qt) * sq).astype(BF16).reshape(A_HEADS, A_V_DIM, tm)
    vT_ref[...] = yt[A_WIDTH:].astype(BF16).reshape(A_HEADS, A_V_DIM, tm)


def _even_inproj(x, sc, sh, mod_row, wn, wt, ck, sk, cq, sq, tm):
    b, s, d = x.shape
    nb = s // tm
    out_shape = (
        jax.ShapeDtypeStruct((b, A_HEADS, A_V_DIM, s), BF16),
        jax.ShapeDtypeStruct((b, s, A_WIDTH), BF16),
        jax.ShapeDtypeStruct((b, A_HEADS, nb, A_V_DIM, tm), BF16),
        jax.ShapeDtypeStruct((b, s, B_WIDTH), F32),
        jax.ShapeDtypeStruct((b, s, B_WIDTH), F32),
    )
    return pl.pallas_call(
        _even_inproj_kernel,
        grid=(b, nb),
        in_specs=[
            pl.BlockSpec((None, tm, d), lambda bi, i: (bi, i, 0)),
            pl.BlockSpec((None, 1, d), lambda bi, i: (mod_row(bi), 0, 0)),
            pl.BlockSpec((None, 1, d), lambda bi, i: (mod_row(bi), 0, 0)),
            pl.BlockSpec(wn.shape, lambda bi, i: (0, 0)),
            pl.BlockSpec(wt.shape, lambda bi, i: (0, 0)),
            pl.BlockSpec((tm, LANES), lambda bi, i: (i, 0)),
            pl.BlockSpec((tm, LANES), lambda bi, i: (i, 0)),
            pl.BlockSpec((LANES, tm), lambda bi, i: (0, i)),
            pl.BlockSpec((LANES, tm), lambda bi, i: (0, i)),
        ],
        out_specs=(
            pl.BlockSpec((None, A_HEADS, A_V_DIM, tm), lambda bi, i: (bi, 0, 0, i)),
            pl.BlockSpec((None, tm, A_WIDTH), lambda bi, i: (bi, i, 0)),
            pl.BlockSpec((None, A_HEADS, None, A_V_DIM, tm), lambda bi, i: (bi, 0, i, 0, 0)),
            pl.BlockSpec((None, tm, B_WIDTH), lambda bi, i: (bi, i, 0)),
            pl.BlockSpec((None, tm, B_WIDTH), lambda bi, i: (bi, i, 0)),
        ),
        out_shape=out_shape,
        compiler_params=_cparams(("parallel", "arbitrary")),
        name="even_inproj",
    )(x, sc, sh, wn, wt, ck, sk, cq, sq)


def _diff_attn_kernel(*refs, n_chunks, lam_init, with_lat):
    if with_lat:
        qT_ref, k_ref, vT_ref, kc_ref, vTc_ref, lamp_ref, g_ref, o_ref, acc1, acc2 = refs
    else:
        qT_ref, kc_ref, vTc_ref, lamp_ref, g_ref, o_ref, acc1, acc2 = refs
    tq = qT_ref.shape[1]
    qT = qT_ref[...]
    row = lax.broadcasted_iota(I32, qT.shape, 0)
    zero = jnp.zeros_like(qT)
    q1 = jnp.where(row < A_QK_DIM, qT, zero)
    q2 = jnp.where(row >= A_QK_DIM, qT, zero)
    acc1[...] = jnp.zeros_like(acc1)
    acc2[...] = jnp.zeros_like(acc2)

    def one_map(kc, vTc, q, m, l, acc):
        s = jnp.dot(kc, q, preferred_element_type=F32)
        m_new = jnp.maximum(m, jnp.max(s, axis=0, keepdims=True))
        alpha = jnp.exp(m - m_new)
        p = jnp.exp(s - m_new)
        l_new = alpha * l + jnp.sum(p, axis=0, keepdims=True)
        acc[...] = alpha * acc[...] + jnp.dot(vTc, p.astype(BF16), preferred_element_type=F32)
        return m_new, l_new

    def chunk(kc, vTc, carry):
        m1, l1, m2, l2 = carry
        m1, l1 = one_map(kc, vTc, q1, m1, l1, acc1)
        m2, l2 = one_map(kc, vTc, q2, m2, l2, acc2)
        return m1, l1, m2, l2

    neg = jnp.full((1, tq), -jnp.inf, F32)
    zer = jnp.zeros((1, tq), F32)
    carry = (neg, zer, neg, zer)
    if with_lat:
        tk = vT_ref.shape[2]

        def body(c, carry):
            start = pl.multiple_of(c * tk, tk)
            return chunk(k_ref[pl.ds(start, tk), :], vT_ref[c], carry)

        carry = lax.fori_loop(0, n_chunks, body, carry)
    m1, l1, m2, l2 = chunk(kc_ref[...], vTc_ref[...], carry)

    lf = lamp_ref[...]
    e1 = jnp.exp(jnp.sum(lf[0:1] * lf[1:2], axis=1, keepdims=True))
    e2 = jnp.exp(jnp.sum(lf[2:3] * lf[3:4], axis=1, keepdims=True))
    lam = e1 - e2 + lam_init
    o = acc1[...] / l1 - lam * (acc2[...] / l2)
    o = o * lax.rsqrt(jnp.mean(o * o, axis=0, keepdims=True) + RMS_EPS)
    o = o * g_ref[...] * (1.0 - lam_init)
    o_ref[...] = o.T.astype(BF16)


def _diff_attention(qT, k, vT, kc, vTc, lam_p, g_col, lam_init, tq):
    b, _, _, s = qT.shape
    lc = kc.shape[1]
    with_lat = k is not None
    in_specs = [pl.BlockSpec((None, None, A_V_DIM, tq), lambda bi, h, i: (bi, h, 0, i))]
    args = [qT]
    n_chunks = 0
    if with_lat:
        sk = k.shape[1]
        n_chunks, tk = vT.shape[2], vT.shape[4]
        in_specs += [pl.BlockSpec((None, sk, LANES), lambda bi, h, i: (bi, 0, h)),
                     pl.BlockSpec((None, None, n_chunks, A_V_DIM, tk), lambda bi, h, i: (bi, h, 0, 0, 0))]
        args += [k, vT]
    in_specs += [pl.BlockSpec((None, lc, LANES), lambda bi, h, i: (bi, 0, h)),
                 pl.BlockSpec((None, None, None, A_V_DIM, lc), lambda bi, h, i: (bi, h, 0, 0, 0)),
                 pl.BlockSpec(lam_p.shape, lambda bi, h, i: (0, 0)),
                 pl.BlockSpec(g_col.shape, lambda bi, h, i: (0, 0))]
    args += [kc, vTc, lam_p, g_col]
    return pl.pallas_call(
        functools.partial(_diff_attn_kernel, n_chunks=n_chunks, lam_init=lam_init, with_lat=with_lat),
        grid=(b, A_HEADS, s // tq),
        in_specs=in_specs,
        out_specs=pl.BlockSpec((None, tq, A_V_DIM), lambda bi, h, i: (bi, i, h)),
        out_shape=jax.ShapeDtypeStruct((b, s, A_WIDTH), BF16),
        scratch_shapes=[pltpu.VMEM((A_V_DIM, tq), F32), pltpu.VMEM((A_V_DIM, tq), F32)],
        compiler_params=_cparams(("parallel", "parallel", "arbitrary")),
        name="diff_attention_lat" if with_lat else "diff_attention_ctx",
    )(*args)


def _layer_norm(z, g, b):
    mu = jnp.mean(z, axis=-1, keepdims=True)
    zc = z - mu
    var = jnp.mean(zc * zc, axis=-1, keepdims=True)
    return zc * lax.rsqrt(var + LN_EPS) * g + b


def _top4_router(h2, rwT_ref, rb_ref, ids_ref, gates_ref):
    tm = h2.shape[0]
    lt = lax.dot_general(rwT_ref[...], h2.astype(BF16), (((1,), (1,)), ((), ())),
                         preferred_element_type=F32) + rb_ref[...]
    eidx = lax.broadcasted_iota(I32, lt.shape, 0)
    vals, ids = [], []
    for _ in range(TOP_K):
        mx = jnp.max(lt, axis=0, keepdims=True)
        idx = jnp.min(jnp.where(lt == mx, eidx, N_EXPERTS), axis=0, keepdims=True)
        vals.append(mx)
        ids.append(idx)
        lt = jnp.where(eidx == idx, -jnp.inf, lt)
    es = [jnp.exp(v - vals[0]) for v in vals]
    den = es[0] + es[1] + es[2] + es[3]
    ids_ref[...] = jnp.concatenate(ids, 0)
    gates_ref[...] = jnp.concatenate([e / den for e in es], 0)
    del tm


def _outproj_kernel(*refs, has_conv):
    if has_conv:
        (a_ref, bg_ref, u_ref, up_ref, un_ref, cw_ref, wa_ref, wb_ref, x_ref, g1_ref, lng_ref, lnb_ref,
         sc2_ref, sh2_ref, rwT_ref, rb_ref, x1_ref, h2_ref, ids_ref, gates_ref) = refs
    else:
        (a_ref, wa_ref, x_ref, g1_ref, lng_ref, lnb_ref,
         sc2_ref, sh2_ref, rwT_ref, rb_ref, x1_ref, h2_ref, ids_ref, gates_ref) = refs
    y = jnp.dot(a_ref[...], wa_ref[...], preferred_element_type=F32)
    if has_conv:
        tm = u_ref.shape[0]
        i = pl.program_id(1)
        nb = pl.num_programs(1)
        u = u_ref[...]
        prev = jnp.where(i > 0, up_ref[SUBLANES - 1:SUBLANES, :], 0.0)
        nxt = jnp.where(i < nb - 1, un_ref[0:1, :], 0.0)
        r = lax.broadcasted_iota(I32, u.shape, 0)
        um1 = jnp.where(r == 0, prev, pltpu.roll(u, 1, 0))
        up1 = jnp.where(r == tm - 1, nxt, pltpu.roll(u, tm - 1, 0))
        cw = cw_ref[...]
        conv = um1 * cw[0:1] + u * cw[1:2] + up1 * cw[2:3]
        b_mix = (bg_ref[...] * conv).astype(BF16)
        y = y + jnp.dot(b_mix, wb_ref[...], preferred_element_type=F32)
    x1 = _layer_norm(DN_ALPHA * x_ref[...] + g1_ref[...] * y, lng_ref[...], lnb_ref[...])
    x1_ref[...] = x1
    h2 = x1 * (1.0 + sc2_ref[...]) + sh2_ref[...]
    h2_ref[...] = h2
    _top4_router(h2, rwT_ref, rb_ref, ids_ref, gates_ref)


def _outproj(a, conv_in, wa, wb, x, mod_vecs, mod_row, ln_g, ln_b, rwT, rb, tm, tok_off, prev_out):
    b, s, d = x.shape
    nb = s // tm
    g1, sc2, sh2 = mod_vecs
    has_conv = conv_in is not None
    n_tok = prev_out[0].shape[0]
    ob = tok_off // tm

    def row_spec(width):
        return pl.BlockSpec((None, tm, width), lambda bi, i: (bi, i, 0))

    def vec_spec():
        return pl.BlockSpec((None, 1, d), lambda bi, i: (mod_row(bi), 0, 0))

    def full(arr):
        return pl.BlockSpec(arr.shape, lambda bi, i: (0,) * arr.ndim)

    in_specs = [row_spec(a.shape[-1])]
    args = [a]
    if has_conv:
        bg, u, cw = conv_in
        s8 = s // SUBLANES
        t8 = tm // SUBLANES
        in_specs += [row_spec(B_WIDTH), row_spec(B_WIDTH),
                     pl.BlockSpec((None, SUBLANES, B_WIDTH),
                                  lambda bi, i: (bi, jnp.maximum(i * t8 - 1, 0), 0)),
                     pl.BlockSpec((None, SUBLANES, B_WIDTH),
                                  lambda bi, i: (bi, jnp.minimum((i + 1) * t8, s8 - 1), 0)),
                     full(cw), full(wa), full(wb)]
        args += [bg, u, u, u, cw, wa, wb]
    else:
        in_specs += [full(wa)]
        args += [wa]
    in_specs += [row_spec(d), vec_spec(), full(ln_g), full(ln_b), vec_spec(), vec_spec(), full(rwT), full(rb)]
    args += [x, g1, ln_g, ln_b, sc2, sh2, rwT, rb]
    n_in = len(args)
    in_specs += [pl.BlockSpec(memory_space=pl.ANY)] * 3
    args += list(prev_out)
    out_specs = (
        row_spec(d),
        pl.BlockSpec((tm, d), lambda bi, i: (ob + bi * nb + i, 0)),
        pl.BlockSpec((TOP_K, tm), lambda bi, i: (0, ob + bi * nb + i)),
        pl.BlockSpec((TOP_K, tm), lambda bi, i: (0, ob + bi * nb + i)),
    )
    out_shape = (
        jax.ShapeDtypeStruct((b, s, d), F32),
        jax.ShapeDtypeStruct((n_tok, d), F32),
        jax.ShapeDtypeStruct((TOP_K, n_tok), I32),
        jax.ShapeDtypeStruct((TOP_K, n_tok), F32),
    )

    def kern(*refs):
        refs = refs[:n_in] + refs[n_in + 3:]
        _outproj_kernel(*refs, has_conv=has_conv)

    return pl.pallas_call(
        kern,
        grid=(b, nb),
        in_specs=in_specs,
        out_specs=out_specs,
        out_shape=out_shape,
        input_output_aliases={n_in: 1, n_in + 1: 2, n_in + 2: 3},
        compiler_params=_cparams(("parallel", "arbitrary")),
        name="outproj_conv" if has_conv else "outproj",
    )(*args)


def _rank_kernel(ids_ref, rank_ref, cnt_ref, run_ref):
    i = pl.program_id(0)
    tr = ids_ref.shape[1]

    @pl.when(i == 0)
    def _():
        run_ref[...] = jnp.zeros_like(run_ref)

    ids = ids_ref[...]
    eidx = lax.broadcasted_iota(I32, (N_EXPERTS, tr), 0)
    hits = [eidx == ids[k:k + 1] for k in range(TOP_K)]
    member = (hits[0] | hits[1] | hits[2] | hits[3]).astype(F32)
    r = lax.broadcasted_iota(I32, (tr, tr), 0)
    c = lax.broadcasted_iota(I32, (tr, tr), 1)
    upper = (r < c).astype(BF16)
    prefix = jnp.dot(member.astype(BF16), upper, preferred_element_type=F32)
    base = run_ref[:, 0:1] + prefix
    ranks = [jnp.sum(jnp.where(hits[k], base, 0.0), axis=0, keepdims=True) for k in range(TOP_K)]
    rank_ref[...] = jnp.concatenate(ranks, 0).astype(I32)
    run_ref[...] = run_ref[...] + jnp.sum(member, axis=1, keepdims=True)
    cnt_ref[...] = run_ref[...]


def _expert_ranks(ids):
    n = ids.shape[1]
    return pl.pallas_call(
        _rank_kernel,
        grid=(n // T_RANK,),
        in_specs=[pl.BlockSpec((TOP_K, T_RANK), lambda i: (0, i))],
        out_specs=(pl.BlockSpec((TOP_K, T_RANK), lambda i: (0, i)),
                   pl.BlockSpec((N_EXPERTS, LANES), lambda i: (0, 0))),
        out_shape=(jax.ShapeDtypeStruct((TOP_K, n), I32),
                   jax.ShapeDtypeStruct((N_EXPERTS, LANES), F32)),
        scratch_shapes=[pltpu.VMEM((N_EXPERTS, LANES), F32)],
        compiler_params=_cparams(("arbitrary",)),
        name="expert_ranks",
    )(ids)


def _dest_kernel(ps_ref, ids_ref, rank_ref, dest_ref):
    ids = ids_ref[...]
    acc = rank_ref[...]
    for e in range(N_EXPERTS):
        acc = acc + jnp.where(ids == e, ps_ref[e], 0)
    dest_ref[...] = acc


def _destinations(pad_start, ids, rank):
    n = ids.shape[1]
    tn = 2048 if n % 2048 == 0 else 512
    return pl.pallas_call(
        _dest_kernel,
        grid_spec=pltpu.PrefetchScalarGridSpec(
            num_scalar_prefetch=1,
            grid=(n // tn,),
            in_specs=[pl.BlockSpec((TOP_K, tn), lambda i, ps: (0, i)),
                      pl.BlockSpec((TOP_K, tn), lambda i, ps: (0, i))],
            out_specs=pl.BlockSpec((TOP_K, tn), lambda i, ps: (0, i))),
        out_shape=jax.ShapeDtypeStruct((TOP_K, n), I32),
        compiler_params=_cparams(("arbitrary",)),
        name="destinations",
    )(pad_start, ids, rank)


def _dispatch_kernel(dest_ref, h_ref, xs_in, xs_hbm, sem):
    del xs_in
    td = dest_ref.shape[1]

    def body(r, carry):
        for k in range(TOP_K):
            d = dest_ref[k, r]
            pltpu.make_async_copy(h_ref.at[pl.ds(r, 1)], xs_hbm.at[pl.ds(d, 1)], sem).start()
        return carry

    lax.fori_loop(0, td, body, 0, unroll=8)
    for k in range(TOP_K):
        pltpu.make_async_copy(h_ref, xs_hbm.at[pl.ds(0, td)], sem).wait()


def _dispatch(dest, h2, n_rows):
    n, d = h2.shape
    xs0 = jnp.zeros((n_rows, d), h2.dtype)
    return pl.pallas_call(
        _dispatch_kernel,
        grid=(n // T_DISPATCH,),
        in_specs=[pl.BlockSpec((TOP_K, T_DISPATCH), lambda i: (0, i), memory_space=pltpu.SMEM),
                  pl.BlockSpec((T_DISPATCH, d), lambda i: (i, 0)),
                  pl.BlockSpec(memory_space=pl.ANY)],
        out_specs=pl.BlockSpec(memory_space=pl.ANY),
        out_shape=jax.ShapeDtypeStruct((n_rows, d), h2.dtype),
        input_output_aliases={2: 0},
        scratch_shapes=[pltpu.SemaphoreType.DMA(())],
        compiler_params=pltpu.CompilerParams(dimension_semantics=("arbitrary",),
                                             has_side_effects=True, vmem_limit_bytes=VMEM_LIMIT),
        name="dispatch",
    )(dest, h2, xs0)


def _expert_kernel(be_ref, nu_ref, xs_ref, wgu_ref, bgu_ref, wd_ref, bd_ref, ys_ref):
    i = pl.program_id(0)

    @pl.when(i < nu_ref[0])
    def _():
        x = xs_ref[...].astype(BF16)
        gu = jnp.dot(x, wgu_ref[...], preferred_element_type=F32) + bgu_ref[...]
        gate = jnp.minimum(gu[:, :D_FF], SWIGLU_LIMIT)
        lin = jnp.clip(gu[:, D_FF:], -SWIGLU_LIMIT, SWIGLU_LIMIT)
        act = gate * jax.nn.sigmoid(SWIGLU_ALPHA * gate) * (lin + 1.0)
        ys_ref[...] = jnp.dot(act.astype(BF16), wd_ref[...], preferred_element_type=F32) + bd_ref[...]

    @pl.when(i >= nu_ref[0])
    def _():
        ys_ref[...] = jnp.zeros_like(ys_ref)


def _expert_mlp(blk_e, n_used, xs, w_gu, b_gu, w_down, b_down):
    n_rows, d = xs.shape
    n_blocks = n_rows // MOE_ROWS

    def blk(i, nu):
        return jnp.minimum(i, nu[0] - 1)

    return pl.pallas_call(
        _expert_kernel,
        grid_spec=pltpu.PrefetchScalarGridSpec(
            num_scalar_prefetch=2,
            grid=(n_blocks,),
            in_specs=[
                pl.BlockSpec((MOE_ROWS, d), lambda i, be, nu: (blk(i, nu), 0)),
                pl.BlockSpec((None, d, 2 * D_FF), lambda i, be, nu: (be[blk(i, nu)], 0, 0)),
                pl.BlockSpec((None, 1, 2 * D_FF), lambda i, be, nu: (be[blk(i, nu)], 0, 0)),
                pl.BlockSpec((None, D_FF, d), lambda i, be, nu: (be[blk(i, nu)], 0, 0)),
                pl.BlockSpec((None, 1, d), lambda i, be, nu: (be[blk(i, nu)], 0, 0)),
            ],
            out_specs=pl.BlockSpec((MOE_ROWS, d), lambda i, be, nu: (i, 0))),
        out_shape=jax.ShapeDtypeStruct((n_rows, d), F32),
        compiler_params=_cparams(("arbitrary",)),
        name="expert_mlp",
    )(blk_e, n_used, xs, w_gu, b_gu.reshape(N_EXPERTS, 1, 2 * D_FF), w_down,
      b_down.reshape(N_EXPERTS, 1, d))


def _combine_kernel(dest_ref, gates_ref, ys_hbm, x1_ref, g2_ref, lng_ref, lnb_ref, out_ref, buf, sem):
    tc = dest_ref.shape[1]

    def body(r, carry):
        for k in range(TOP_K):
            d = dest_ref[k, r]
            pltpu.make_async_copy(ys_hbm.at[pl.ds(d, 1)], buf.at[k, pl.ds(r, 1)], sem).start()
        return carry

    lax.fori_loop(0, tc, body, 0)
    for k in range(TOP_K):
        pltpu.make_async_copy(ys_hbm.at[pl.ds(0, tc)], buf.at[k], sem).wait()
    gpad = jnp.concatenate([gates_ref[...], jnp.zeros((LANES - TOP_K, tc), F32)], 0)
    gt = gpad.T
    f = buf[0] * gt[:, 0:1]
    for k in range(1, TOP_K):
        f = f + buf[k] * gt[:, k:k + 1]
    out_ref[...] = _layer_norm(DN_ALPHA * x1_ref[...] + g2_ref[...] * f, lng_ref[...], lnb_ref[...])


def _combine(dest, gates, ys, x1, g2, row_of_block, ln_g, ln_b):
    n, d = x1.shape
    tc = T_COMBINE
    return pl.pallas_call(
        _combine_kernel,
        grid=(n // tc,),
        in_specs=[pl.BlockSpec((TOP_K, tc), lambda i: (0, i), memory_space=pltpu.SMEM),
                  pl.BlockSpec((TOP_K, tc), lambda i: (0, i)),
                  pl.BlockSpec(memory_space=pl.ANY),
                  pl.BlockSpec((tc, d), lambda i: (i, 0)),
                  pl.BlockSpec((None, 1, d), lambda i: (row_of_block(i), 0, 0)),
                  pl.BlockSpec(ln_g.shape, lambda i: (0, 0)),
                  pl.BlockSpec(ln_b.shape, lambda i: (0, 0))],
        out_specs=pl.BlockSpec((tc, d), lambda i: (i, 0)),
        out_shape=jax.ShapeDtypeStruct((n, d), F32),
        scratch_shapes=[pltpu.VMEM((TOP_K, tc, d), F32), pltpu.SemaphoreType.DMA(())],
        compiler_params=_cparams(("arbitrary",)),
        name="combine",
    )(dest, gates, ys, x1, g2, ln_g, ln_b)


def _moe(h2, ids, gates, x1, g2, row_of_block, ln_g, ln_b, w_gu, b_gu, w_down, b_down):
    n = h2.shape[0]
    nk = n * TOP_K
    n_blocks = -(-nk // MOE_ROWS) + N_EXPERTS
    rank, cnt = _expert_ranks(ids)
    counts = cnt[:, 0].astype(I32)
    padded = (counts + MOE_ROWS - 1) // MOE_ROWS * MOE_ROWS
    pad_end = jnp.cumsum(padded)
    pad_start = (pad_end - padded).astype(I32)
    blk_start = jnp.arange(n_blocks, dtype=I32) * MOE_ROWS
    blk_e = jnp.minimum(jnp.sum((pad_end[None, :] <= blk_start[:, None]).astype(I32), axis=1),
                        N_EXPERTS - 1).astype(I32)
    n_used = (pad_end[-1:] // MOE_ROWS).astype(I32)
    dest = _destinations(pad_start, ids, rank)
    xs = _dispatch(dest, h2, n_blocks * MOE_ROWS)
    ys = _expert_mlp(blk_e, n_used, xs, w_gu.astype(BF16), b_gu, w_down.astype(BF16), b_down)
    return _combine(dest, gates, ys, x1, g2, row_of_block, ln_g, ln_b)


def _odd_inproj_kernel(x_ref, sc_ref, sh_ref, w_ref, c_ref, s_ref, q_ref, k_ref, v_ref):
    h = (x_ref[...] * (1.0 + sc_ref[...]) + sh_ref[...]).astype(BF16)
    y = jnp.dot(h, w_ref[...], preferred_element_type=F32)
    cs = c_ref[...]
    sn = s_ref[...]
    nq = C_HEADS * C_HEAD_DIM // LANES
    pieces = []
    for j in range(nq + 1):
        t = y[:, j * LANES:(j + 1) * LANES]
        if j < nq:
            pieces.append((t * cs + _swap16_lanes(t) * sn) * (C_HEAD_DIM ** -0.5))
        else:
            pieces.append(t * cs + _swap16_lanes(t) * sn)
    q_ref[...] = jnp.concatenate(pieces[:nq], 1).astype(BF16)
    k_ref[...] = pieces[nq].astype(BF16)
    v_ref[...] = y[:, (nq + 1) * LANES:].astype(BF16)


def _odd_inproj(x, sc, sh, mod_row, w, cs, sn, tm):
    b, s, d = x.shape
    qw = C_HEADS * C_HEAD_DIM
    return pl.pallas_call(
        _odd_inproj_kernel,
        grid=(b, s // tm),
        in_specs=[pl.BlockSpec((None, tm, d), lambda bi, i: (bi, i, 0)),
                  pl.BlockSpec((None, 1, d), lambda bi, i: (mod_row(bi), 0, 0)),
                  pl.BlockSpec((None, 1, d), lambda bi, i: (mod_row(bi), 0, 0)),
                  pl.BlockSpec(w.shape, lambda bi, i: (0, 0)),
                  pl.BlockSpec((tm, LANES), lambda bi, i: (i, 0)),
                  pl.BlockSpec((tm, LANES), lambda bi, i: (i, 0))],
        out_specs=(pl.BlockSpec((None, tm, qw), lambda bi, i: (bi, i, 0)),
                   pl.BlockSpec((None, tm, LANES), lambda bi, i: (bi, i, 0)),
                   pl.BlockSpec((None, tm, LANES), lambda bi, i: (bi, i, 0))),
        out_shape=(jax.ShapeDtypeStruct((b, s, qw), BF16),
                   jax.ShapeDtypeStruct((b, s, LANES), BF16),
                   jax.ShapeDtypeStruct((b, s, LANES), BF16)),
        compiler_params=_cparams(("parallel", "arbitrary")),
        name="odd_inproj",
    )(x, sc, sh, w, cs, sn)


def _win_attn_kernel(q_ref, kp_ref, k0_ref, kn_ref, kc_ref, vp_ref, v0_ref, vn_ref, vc_ref, sink_ref, o_ref):
    tq = q_ref.shape[0]
    i = pl.program_id(1)
    nb = pl.num_programs(1)
    q = q_ref[...]
    lane = lax.broadcasted_iota(I32, (tq, LANES), 1)
    lo = lane < C_HEAD_DIM
    zero = jnp.zeros((tq, LANES), BF16)
    n_pairs = C_HEADS // C_KV_HEADS
    rows = []
    for j in range(n_pairs):
        t = q[:, j * LANES:(j + 1) * LANES]
        rows.append(jnp.where(lo, t, zero))
        rows.append(jnp.where(lo, zero, t))
    qs = jnp.concatenate(rows, 0)
    kb = jnp.concatenate([kp_ref[...], k0_ref[...], kn_ref[...], kc_ref[...]], 0)
    vb = jnp.concatenate([vp_ref[...], v0_ref[...], vn_ref[...], vc_ref[...]], 0)
    nk = kb.shape[0]
    s = lax.dot_general(qs, kb, (((1,), (1,)), ((), ())), preferred_element_type=F32)
    s = s.reshape(C_HEADS, tq, nk)
    qpos = lax.broadcasted_iota(I32, (tq, nk), 0)
    col = lax.broadcasted_iota(I32, (tq, nk), 1)
    ok = (jnp.abs(col - tq - qpos) <= C_WINDOW)
    ok = ok & ((col >= tq) | (i > 0)) & ((col < 2 * tq) | (i < nb - 1))
    ok = ok | (col >= 3 * tq)
    s = jnp.where(ok[None], s, NEG_INF)
    sink = sink_ref[...]
    m = jnp.maximum(jnp.max(s, axis=-1, keepdims=True), sink)
    p = jnp.exp(s - m)
    den = jnp.sum(p, axis=-1, keepdims=True) + jnp.exp(sink - m)
    p = (p / den).astype(BF16).reshape(C_HEADS * tq, nk)
    o = jnp.dot(p, vb, preferred_element_type=F32)
    outs = []
    for j in range(n_pairs):
        a = o[(2 * j) * tq:(2 * j + 1) * tq]
        b = o[(2 * j + 1) * tq:(2 * j + 2) * tq]
        outs.append(jnp.where(lo, a, b))
    o_ref[...] = jnp.concatenate(outs, 1).astype(BF16)


def _win_attention(q, k, v, kc, vc, sink):
    b, s, qw = q.shape
    tq = TQ_WIN
    nb = s // tq
    lc = kc.shape[1]

    def blk(off):
        return pl.BlockSpec((None, tq, LANES), lambda bi, i: (bi, jnp.clip(i + off, 0, nb - 1), 0))

    ctx = pl.BlockSpec((None, lc, LANES), lambda bi, i: (bi, 0, 0))
    return pl.pallas_call(
        _win_attn_kernel,
        grid=(b, nb),
        in_specs=[pl.BlockSpec((None, tq, qw), lambda bi, i: (bi, i, 0)),
                  blk(-1), blk(0), blk(1), ctx, blk(-1), blk(0), blk(1), ctx,
                  pl.BlockSpec(sink.shape, lambda bi, i: (0, 0, 0))],
        out_specs=pl.BlockSpec((None, tq, qw), lambda bi, i: (bi, i, 0)),
        out_shape=jax.ShapeDtypeStruct((b, s, qw), BF16),
        compiler_params=_cparams(("parallel", "arbitrary")),
        name="window_attention",
    )(q, k, k, k, kc, v, v, v, vc, sink)


def kernel(x, c, ctx, c_ctx, ada_w, ada_b, ln_g, ln_b, ab_w_in, ab_w_out, diff_lambda, diff_subln_g,
           conv_w, c_w_in, c_w_out, c_sink, router_w, router_b, w_gu, b_gu, w_down, b_down):
    b, s, d = x.shape
    lc = ctx.shape[1]
    n_c = b * lc
    ctx_row = b

    cvec = jnp.zeros((8, d), F32).at[:b].set(c).at[b].set(c_ctx)
    mod = _modulation(cvec, ada_w, ada_b)

    def mod_vec(layer, j):
        return mod[layer, :, j * d:(j + 1) * d].reshape(8, 1, d)

    lat_row = lambda bi: bi
    ctx_rowf = lambda bi: ctx_row

    cos64, sin64 = _rope_tables(s)
    cos_l = jnp.concatenate([cos64, cos64], -1)
    sin_l = jnp.concatenate([sin64, sin64], -1)
    one_c = jnp.ones((lc, LANES), F32)
    zero_c = jnp.zeros((lc, LANES), F32)
    qscale = A_QK_DIM ** -0.5

    l = 0
    w_in = ab_w_in[0]
    wn = w_in[:, A_WIDTH:].astype(BF16)
    wn = jnp.concatenate([wn[:, :A_WIDTH], wn[:, 2 * A_WIDTH:]], 1)
    wt = jnp.concatenate([w_in[:, :A_WIDTH], w_in[:, 2 * A_WIDTH:3 * A_WIDTH]], 1).T.astype(BF16)
    sc1, sh1, g1 = mod_vec(l, 1), mod_vec(l, 0), mod_vec(l, 2)
    sh2, sc2, g2 = mod_vec(l, 3), mod_vec(l, 4), mod_vec(l, 5)
    lam_init = 0.8 - 0.6 * math.exp(-0.3 * l)

    qT, k, vT, bg, u = _even_inproj(x, sc1, sh1, lat_row, wn, wt, cos_l, sin_l,
                                    (cos_l * qscale).T, (sin_l * qscale).T, TM_PROJ)
    qTc, kc, vTc, bgc, uc = _even_inproj(ctx, sc1, sh1, ctx_rowf, wn, wt, one_c, zero_c,
                                         (one_c * qscale).T, zero_c.T, lc)
    g_col = diff_subln_g[0].reshape(A_V_DIM, 1)
    a_lat = _diff_attention(qT, k, vT, kc, vTc, diff_lambda[0], g_col, lam_init, TQ_DIFF)
    a_ctx = _diff_attention(qTc, None, None, kc, vTc, diff_lambda[0], g_col, lam_init, lc)

    w_out = ab_w_out[0].astype(BF16)
    wa, wb = w_out[:A_WIDTH], w_out[A_WIDTH:]
    lng0, lnb0 = ln_g[l, 0].reshape(1, d), ln_b[l, 0].reshape(1, d)
    lng1, lnb1 = ln_g[l, 1].reshape(1, d), ln_b[l, 1].reshape(1, d)
    rwT = router_w[l].T.astype(BF16)
    rb = router_b[l].reshape(N_EXPERTS, 1)
    n0 = n_c + b * s
    empty = (jnp.zeros((n0, d), F32), jnp.zeros((TOP_K, n0), I32), jnp.zeros((TOP_K, n0), F32))
    xc1, h2a, idsa, gatesa = _outproj(a_ctx, (bgc, uc, conv_w[0]), wa, wb, ctx, (g1, sc2, sh2), ctx_rowf,
                                      lng0, lnb0, rwT, rb, lc, 0, empty)
    x1, h2a, idsa, gatesa = _outproj(a_lat, (bg, u, conv_w[0]), wa, wb, x, (g1, sc2, sh2), lat_row,
                                     lng0, lnb0, rwT, rb, TM_PROJ, n_c, (h2a, idsa, gatesa))
    x1_all = jnp.concatenate([xc1.reshape(n_c, d), x1.reshape(b * s, d)], 0)
    ncb = n_c // T_COMBINE
    spb = s // T_COMBINE
    row_of_block0 = lambda i: jnp.where(i < ncb, ctx_row, jnp.maximum(i - ncb, 0) // spb)
    y_all = _moe(h2a, idsa, gatesa, x1_all, g2, row_of_block0, lng1, lnb1,
                 w_gu[l], b_gu[l], w_down[l], b_down[l])
    xc = y_all[:n_c].reshape(b, lc, d)
    x = y_all[n_c:].reshape(b, s, d)

    l = 1
    sc1, sh1, g1 = mod_vec(l, 1), mod_vec(l, 0), mod_vec(l, 2)
    sh2, sc2, g2 = mod_vec(l, 3), mod_vec(l, 4), mod_vec(l, 5)
    g = C_HEADS // C_KV_HEADS
    perm = jnp.array([(kv * g + j) * C_HEAD_DIM + dd for j in range(g) for kv in range(C_KV_HEADS)
                      for dd in range(C_HEAD_DIM)], I32)
    w_in = c_w_in[0]
    qw = C_HEADS * C_HEAD_DIM
    w_odd = jnp.concatenate([w_in[:, :qw][:, perm], w_in[:, qw:]], 1).astype(BF16)
    q, k, v = _odd_inproj(x, sc1, sh1, lat_row, w_odd, cos_l, sin_l, TM_PROJ)
    _, kc, vc = _odd_inproj(xc, sc1, sh1, ctx_rowf, w_odd, one_c, zero_c, lc)
    sink = c_sink[0][perm[::C_HEAD_DIM] // C_HEAD_DIM].reshape(C_HEADS, 1, 1)
    o = _win_attention(q, k, v, kc, vc, sink)
    wa = c_w_out[0][perm].astype(BF16)
    lng0, lnb0 = ln_g[l, 0].reshape(1, d), ln_b[l, 0].reshape(1, d)
    lng1, lnb1 = ln_g[l, 1].reshape(1, d), ln_b[l, 1].reshape(1, d)
    rwT = router_w[l].T.astype(BF16)
    rb = router_b[l].reshape(N_EXPERTS, 1)
    n1 = b * s
    empty = (jnp.zeros((n1, d), F32), jnp.zeros((TOP_K, n1), I32), jnp.zeros((TOP_K, n1), F32))
    x1, h2a, idsa, gatesa = _outproj(o, None, wa, None, x, (g1, sc2, sh2), lat_row,
                                     lng0, lnb0, rwT, rb, TM_PROJ, 0, empty)
    row_of_block1 = lambda i: i // spb
    y = _moe(h2a, idsa, gatesa, x1.reshape(n1, d), g2, row_of_block1, lng1, lnb1,
             w_gu[l], b_gu[l], w_down[l], b_down[l])
    return y.reshape(b, s, d)
```

```python
import functools
import math

import jax
import jax.numpy as jnp
from jax import lax
from jax.experimental import pallas as pl
from jax.experimental.pallas import tpu as pltpu

F32 = jnp.float32
BF16 = jnp.bfloat16
I32 = jnp.int32

D_MODEL = 1024
DEPTH = 2
GRID_W = 64
ROPE_DIM = 64
ROPE_BASE = 10000.0
A_HEADS = 4
A_QK_DIM = 64
A_V_DIM = 128
A_WIDTH = 512
B_WIDTH = 512
C_HEADS = 16
C_KV_HEADS = 2
C_HEAD_DIM = 64
C_WINDOW = 128
N_EXPERTS = 32
TOP_K = 4
D_FF = 1024
SWIGLU_ALPHA = 1.702
SWIGLU_LIMIT = 7.0
LN_EPS = 1e-5
RMS_EPS = 1e-5
NEG_INF = -1e30
DN_ALPHA = (2 * DEPTH) ** 0.25

LANES = 128
SUBLANES = 8
VMEM_LIMIT = 56 * 1024 * 1024

TM_PROJ = 512
TQ_DIFF = 512
TQ_WIN = 128
MOE_ROWS = 512
T_RANK = 512
T_DISPATCH = 512
T_COMBINE = 256


def _cparams(sem):
    return pltpu.CompilerParams(dimension_semantics=sem, vmem_limit_bytes=VMEM_LIMIT)


def _mod_kernel(c_ref, w_ref, b_ref, o_ref):
    c = c_ref[...]
    s = c * jax.nn.sigmoid(c)
    o_ref[...] = jnp.dot(s.astype(BF16), w_ref[...].astype(BF16),
                         preferred_element_type=F32) + b_ref[...]


def _modulation(cvec, ada_w, ada_b):
    d = D_MODEL
    tn = 1536
    return pl.pallas_call(
        _mod_kernel,
        grid=(DEPTH, 6 * d // tn),
        in_specs=[pl.BlockSpec((8, d), lambda l, j: (0, 0)),
                  pl.BlockSpec((None, d, tn), lambda l, j: (l, 0, j)),
                  pl.BlockSpec((None, 1, tn), lambda l, j: (l, 0, j))],
        out_specs=pl.BlockSpec((None, 8, tn), lambda l, j: (l, 0, j)),
        out_shape=jax.ShapeDtypeStruct((DEPTH, 8, 6 * d), F32),
        compiler_params=_cparams(("arbitrary", "arbitrary")),
        name="modulation",
    )(cvec, ada_w, ada_b.reshape(DEPTH, 1, 6 * d))


def _swap16_lanes(t):
    lane = lax.broadcasted_iota(I32, t.shape, 1)
    first = (lane % 32) < 16
    return jnp.where(first, pltpu.roll(t, LANES - 16, 1), pltpu.roll(t, 16, 1))


def _swap16_rows(a):
    pieces = []
    for i in range(0, a.shape[0], 32):
        pieces.append(a[i + 16:i + 32])
        pieces.append(a[i:i + 16])
    return jnp.concatenate(pieces, 0)


def _rope_tables(s):
    rows = jnp.repeat(jnp.arange(s // GRID_W, dtype=F32), GRID_W)
    cols = jnp.tile(jnp.arange(GRID_W, dtype=F32), s // GRID_W)
    axis_dim = ROPE_DIM // 2
    inv = ROPE_BASE ** (-jnp.arange(0, axis_dim, 2, dtype=F32) / axis_dim)
    ar, ac = rows[:, None] * inv, cols[:, None] * inv
    cr, sr, cc, sc = jnp.cos(ar), jnp.sin(ar), jnp.cos(ac), jnp.sin(ac)
    cos64 = jnp.concatenate([cr, cr, cc, cc], -1)
    sin64 = jnp.concatenate([-sr, sr, -sc, sc], -1)
    return cos64, sin64


def _even_inproj_kernel(x_ref, sc_ref, sh_ref, wn_ref, wt_ref, ck_ref, sk_ref, cq_ref, sq_ref,
                        qT_ref, k_ref, vT_ref, bg_ref, u_ref):
    tm = x_ref.shape[0]
    h = (x_ref[...] * (1.0 + sc_ref[...]) + sh_ref[...]).astype(BF16)
    yn = jnp.dot(h, wn_ref[...], preferred_element_type=F32)
    ck = ck_ref[...]
    sk = sk_ref[...]
    pieces = []
    for j in range(A_HEADS):
        t = yn[:, j * LANES:(j + 1) * LANES]
        pieces.append(t * ck + _swap16_lanes(t) * sk)
    k_ref[...] = jnp.concatenate(pieces, 1).astype(BF16)
    bg_ref[...] = yn[:, A_WIDTH:A_WIDTH + B_WIDTH]
    u_ref[...] = yn[:, A_WIDTH + B_WIDTH:A_WIDTH + 2 * B_WIDTH] * yn[:, A_WIDTH + 2 * B_WIDTH:]
    yt = lax.dot_general(wt_ref[...], h, (((1,), (1,)), ((), ())),
                         preferred_element_type=F32)
    qt = yt[:A_WIDTH]
    cq = jnp.concatenate([cq_ref[...]] * A_HEADS, 0)
    sq = jnp.concatenate([sq_ref[...]] * A_HEADS, 0)
    qT_ref[...] = (qt * cq + _swap16_rows(qt) * sq).astype(BF16).reshape(A_HEADS, A_V_DIM, tm)
    vT_ref[...] = yt[A_WIDTH:].astype(BF16).reshape(A_HEADS, A_V_DIM, tm)


def _even_inproj(x, x_off, b, s, sc, sh, mod_row, wn, wt, ck, sk, cq, sq, tm):
    d = x.shape[1]
    nb = s // tm
    xo = x_off // tm
    out_shape = (
        jax.ShapeDtypeStruct((b, A_HEADS, A_V_DIM, s), BF16),
        jax.ShapeDtypeStruct((b, s, A_WIDTH), BF16),
        jax.ShapeDtypeStruct((b, A_HEADS, nb, A_V_DIM, tm), BF16),
        jax.ShapeDtypeStruct((b, s, B_WIDTH), F32),
        jax.ShapeDtypeStruct((b, s, B_WIDTH), F32),
    )
    return pl.pallas_call(
        _even_inproj_kernel,
        grid=(b, nb),
        in_specs=[
            pl.BlockSpec((tm, d), lambda bi, i: (xo + bi * nb + i, 0)),
            pl.BlockSpec((None, 1, d), lambda bi, i: (mod_row(bi), 0, 0)),
            pl.BlockSpec((None, 1, d), lambda bi, i: (mod_row(bi), 0, 0)),
            pl.BlockSpec(wn.shape, lambda bi, i: (0, 0)),
            pl.BlockSpec(wt.shape, lambda bi, i: (0, 0)),
            pl.BlockSpec((tm, LANES), lambda bi, i: (i, 0)),
            pl.BlockSpec((tm, LANES), lambda bi, i: (i, 0)),
            pl.BlockSpec((LANES, tm), lambda bi, i: (0, i)),
            pl.BlockSpec((LANES, tm), lambda bi, i: (0, i)),
        ],
        out_specs=(
            pl.BlockSpec((None, A_HEADS, A_V_DIM, tm), lambda bi, i: (bi, 0, 0, i)),
            pl.BlockSpec((None, tm, A_WIDTH), lambda bi, i: (bi, i, 0)),
            pl.BlockSpec((None, A_HEADS, None, A_V_DIM, tm), lambda bi, i: (bi, 0, i, 0, 0)),
            pl.BlockSpec((None, tm, B_WIDTH), lambda bi, i: (bi, i, 0)),
            pl.BlockSpec((None, tm, B_WIDTH), lambda bi, i: (bi, i, 0)),
        ),
        out_shape=out_shape,
        compiler_params=_cparams(("parallel", "arbitrary")),
        name="even_inproj",
    )(x, sc, sh, wn, wt, ck, sk, cq, sq)


def _diff_attn_kernel(*refs, n_chunks, lam_init, with_lat):
    if with_lat:
        (qT_ref, k_ref, vT_ref, kc_ref, vTc_ref, lamp_ref, g_ref, o_ref,
         acc1, acc2, sa1, sa2, sb1, sb2) = refs
    else:
        qT_ref, kc_ref, vTc_ref, lamp_ref, g_ref, o_ref, acc1, acc2 = refs
    tq = qT_ref.shape[1]
    qT = qT_ref[...]
    row = lax.broadcasted_iota(I32, qT.shape, 0)
    zero = jnp.zeros_like(qT)
    q1 = jnp.where(row < A_QK_DIM, qT, zero)
    q2 = jnp.where(row >= A_QK_DIM, qT, zero)
    acc1[...] = jnp.zeros_like(acc1)
    acc2[...] = jnp.zeros_like(acc2)

    def one_map(s, vTc, m, l, acc):
        m_new = jnp.maximum(m, jnp.max(s, axis=0, keepdims=True))
        alpha = jnp.exp2(m - m_new)
        p = jnp.exp2(s - m_new)
        l_new = alpha * l + jnp.sum(p, axis=0, keepdims=True)
        acc[...] = alpha * acc[...] + jnp.dot(vTc, p.astype(BF16), preferred_element_type=F32)
        return m_new, l_new

    def softmax_pv(s1, s2, vTc, carry):
        m1, l1, m2, l2 = carry
        m1, l1 = one_map(s1, vTc, m1, l1, acc1)
        m2, l2 = one_map(s2, vTc, m2, l2, acc2)
        return m1, l1, m2, l2

    neg = jnp.full((1, tq), -jnp.inf, F32)
    zer = jnp.zeros((1, tq), F32)
    kc = kc_ref[...]
    carry = softmax_pv(jnp.dot(kc, q1, preferred_element_type=F32),
                       jnp.dot(kc, q2, preferred_element_type=F32), vTc_ref[...], (neg, zer, neg, zer))
    if with_lat:
        tk = vT_ref.shape[2]

        def scores(c, s1_ref, s2_ref):
            kk = k_ref[pl.ds(pl.multiple_of(c * tk, tk), tk), :]
            s1_ref[...] = jnp.dot(kk, q1, preferred_element_type=F32)
            s2_ref[...] = jnp.dot(kk, q2, preferred_element_type=F32)

        scores(0, sa1, sa2)

        def body(j, carry):
            c = 2 * j
            scores(c + 1, sb1, sb2)
            carry = softmax_pv(sa1[...], sa2[...], vT_ref[c], carry)
            scores(c + 2, sa1, sa2)
            return softmax_pv(sb1[...], sb2[...], vT_ref[c + 1], carry)

        carry = lax.fori_loop(0, n_chunks // 2 - 1, body, carry)
        scores(n_chunks - 1, sb1, sb2)
        carry = softmax_pv(sa1[...], sa2[...], vT_ref[n_chunks - 2], carry)
        carry = softmax_pv(sb1[...], sb2[...], vT_ref[n_chunks - 1], carry)
    m1, l1, m2, l2 = carry

    lf = lamp_ref[...]
    e1 = jnp.exp(jnp.sum(lf[0:1] * lf[1:2], axis=1, keepdims=True))
    e2 = jnp.exp(jnp.sum(lf[2:3] * lf[3:4], axis=1, keepdims=True))
    lam = e1 - e2 + lam_init
    o = acc1[...] / l1 - lam * (acc2[...] / l2)
    o = o * lax.rsqrt(jnp.mean(o * o, axis=0, keepdims=True) + RMS_EPS)
    o = o * g_ref[...] * (1.0 - lam_init)
    o_ref[...] = o.T.astype(BF16)


def _diff_attention(qT, k, vT, kc, vTc, lam_p, g_col, lam_init, tq):
    b, _, _, s = qT.shape
    lc = kc.shape[1]
    with_lat = k is not None
    in_specs = [pl.BlockSpec((None, None, A_V_DIM, tq), lambda bi, h, i: (bi, h, 0, i))]
    args = [qT]
    n_chunks = 0
    if with_lat:
        sk = k.shape[1]
        n_chunks, tk = vT.shape[2], vT.shape[4]
        in_specs += [pl.BlockSpec((None, sk, LANES), lambda bi, h, i: (bi, 0, h)),
                     pl.BlockSpec((None, None, n_chunks, A_V_DIM, tk), lambda bi, h, i: (bi, h, 0, 0, 0))]
        args += [k, vT]
    in_specs += [pl.BlockSpec((None, lc, LANES), lambda bi, h, i: (bi, 0, h)),
                 pl.BlockSpec((None, None, None, A_V_DIM, lc), lambda bi, h, i: (bi, h, 0, 0, 0)),
                 pl.BlockSpec(lam_p.shape, lambda bi, h, i: (0, 0)),
                 pl.BlockSpec(g_col.shape, lambda bi, h, i: (0, 0))]
    args += [kc, vTc, lam_p, g_col]
    scratch = [pltpu.VMEM((A_V_DIM, tq), F32), pltpu.VMEM((A_V_DIM, tq), F32)]
    if with_lat:
        assert n_chunks % 2 == 0 and n_chunks >= 2
        scratch += [pltpu.VMEM((tk, tq), F32)] * 4
    return pl.pallas_call(
        functools.partial(_diff_attn_kernel, n_chunks=n_chunks, lam_init=lam_init, with_lat=with_lat),
        grid=(b, A_HEADS, s // tq),
        in_specs=in_specs,
        out_specs=pl.BlockSpec((None, tq, A_V_DIM), lambda bi, h, i: (bi, i, h)),
        out_shape=jax.ShapeDtypeStruct((b, s, A_WIDTH), BF16),
        scratch_shapes=scratch,
        compiler_params=_cparams(("parallel", "parallel", "arbitrary")),
        name="diff_attention_lat" if with_lat else "diff_attention_ctx",
    )(*args)


def _layer_norm(z, g, b):
    mu = jnp.mean(z, axis=-1, keepdims=True)
    zc = z - mu
    var = jnp.mean(zc * zc, axis=-1, keepdims=True)
    return zc * lax.rsqrt(var + LN_EPS) * g + b


def _top4_router(h2, rwT_ref, rb_ref, ids_ref, gates_ref):
    tm = h2.shape[0]
    lt = lax.dot_general(rwT_ref[...], h2.astype(BF16), (((1,), (1,)), ((), ())),
                         preferred_element_type=F32) + rb_ref[...]
    eidx = lax.broadcasted_iota(I32, lt.shape, 0)
    vals, ids = [], []
    for _ in range(TOP_K):
        mx = jnp.max(lt, axis=0, keepdims=True)
        idx = jnp.min(jnp.where(lt == mx, eidx, N_EXPERTS), axis=0, keepdims=True)
        vals.append(mx)
        ids.append(idx)
        lt = jnp.where(eidx == idx, -jnp.inf, lt)
    es = [jnp.exp(v - vals[0]) for v in vals]
    den = es[0] + es[1] + es[2] + es[3]
    ids_ref[...] = jnp.concatenate(ids, 0)
    gates_ref[...] = jnp.concatenate([e / den for e in es], 0)
    del tm


def _outproj_kernel(*refs, has_conv):
    if has_conv:
        (a_ref, bg_ref, u_ref, up_ref, un_ref, cw_ref, wa_ref, wb_ref, x_ref, g1_ref, lng_ref, lnb_ref,
         sc2_ref, sh2_ref, rwT_ref, rb_ref, x1_ref, h2_ref, ids_ref, gates_ref) = refs
    else:
        (a_ref, wa_ref, x_ref, g1_ref, lng_ref, lnb_ref,
         sc2_ref, sh2_ref, rwT_ref, rb_ref, x1_ref, h2_ref, ids_ref, gates_ref) = refs
    y = jnp.dot(a_ref[...], wa_ref[...], preferred_element_type=F32)
    if has_conv:
        tm = u_ref.shape[0]
        i = pl.program_id(1)
        nb = pl.num_programs(1)
        u = u_ref[...]
        prev = jnp.where(i > 0, up_ref[SUBLANES - 1:SUBLANES, :], 0.0)
        nxt = jnp.where(i < nb - 1, un_ref[0:1, :], 0.0)
        r = lax.broadcasted_iota(I32, u.shape, 0)
        um1 = jnp.where(r == 0, prev, pltpu.roll(u, 1, 0))
        up1 = jnp.where(r == tm - 1, nxt, pltpu.roll(u, tm - 1, 0))
        cw = cw_ref[...]
        conv = um1 * cw[0:1] + u * cw[1:2] + up1 * cw[2:3]
        b_mix = (bg_ref[...] * conv).astype(BF16)
        y = y + jnp.dot(b_mix, wb_ref[...], preferred_element_type=F32)
    x1 = _layer_norm(DN_ALPHA * x_ref[...] + g1_ref[...] * y, lng_ref[...], lnb_ref[...])
    x1_ref[...] = x1
    h2 = x1 * (1.0 + sc2_ref[...]) + sh2_ref[...]
    h2_ref[...] = h2
    _top4_router(h2, rwT_ref, rb_ref, ids_ref, gates_ref)


def _outproj(a, conv_in, wa, wb, x, x_off, mod_vecs, mod_row, ln_g, ln_b, rwT, rb, tm, tok_off, prev_out):
    b, s = a.shape[0], a.shape[1]
    d = x.shape[1]
    nb = s // tm
    g1, sc2, sh2 = mod_vecs
    has_conv = conv_in is not None
    n_tok = prev_out[0].shape[0] if prev_out else b * s
    ob = tok_off // tm
    xo = x_off // tm

    def row_spec(width):
        return pl.BlockSpec((None, tm, width), lambda bi, i: (bi, i, 0))

    def vec_spec():
        return pl.BlockSpec((None, 1, d), lambda bi, i: (mod_row(bi), 0, 0))

    def full(arr):
        return pl.BlockSpec(arr.shape, lambda bi, i: (0,) * arr.ndim)

    in_specs = [row_spec(a.shape[-1])]
    args = [a]
    if has_conv:
        bg, u, cw = conv_in
        s8 = s // SUBLANES
        t8 = tm // SUBLANES
        in_specs += [row_spec(B_WIDTH), row_spec(B_WIDTH),
                     pl.BlockSpec((None, SUBLANES, B_WIDTH),
                                  lambda bi, i: (bi, jnp.maximum(i * t8 - 1, 0), 0)),
                     pl.BlockSpec((None, SUBLANES, B_WIDTH),
                                  lambda bi, i: (bi, jnp.minimum((i + 1) * t8, s8 - 1), 0)),
                     full(cw), full(wa), full(wb)]
        args += [bg, u, u, u, cw, wa, wb]
    else:
        in_specs += [full(wa)]
        args += [wa]
    in_specs += [pl.BlockSpec((tm, d), lambda bi, i: (xo + bi * nb + i, 0)),
                 vec_spec(), full(ln_g), full(ln_b), vec_spec(), vec_spec(), full(rwT), full(rb)]
    args += [x, g1, ln_g, ln_b, sc2, sh2, rwT, rb]
    n_in = len(args)
    n_alias = len(prev_out)
    in_specs += [pl.BlockSpec(memory_space=pl.ANY)] * n_alias
    args += list(prev_out)
    out_specs = (
        pl.BlockSpec((tm, d), lambda bi, i: (ob + bi * nb + i, 0)),
        pl.BlockSpec((tm, d), lambda bi, i: (ob + bi * nb + i, 0)),
        pl.BlockSpec((TOP_K, tm), lambda bi, i: (0, ob + bi * nb + i)),
        pl.BlockSpec((TOP_K, tm), lambda bi, i: (0, ob + bi * nb + i)),
    )
    out_shape = (
        jax.ShapeDtypeStruct((n_tok, d), F32),
        jax.ShapeDtypeStruct((n_tok, d), F32),
        jax.ShapeDtypeStruct((TOP_K, n_tok), I32),
        jax.ShapeDtypeStruct((TOP_K, n_tok), F32),
    )

    def kern(*refs):
        refs = refs[:n_in] + refs[n_in + n_alias:]
        _outproj_kernel(*refs, has_conv=has_conv)

    return pl.pallas_call(
        kern,
        grid=(b, nb),
        in_specs=in_specs,
        out_specs=out_specs,
        out_shape=out_shape,
        input_output_aliases={n_in + j: j for j in range(n_alias)},
        compiler_params=_cparams(("parallel", "arbitrary")),
        name="outproj_conv" if has_conv else "outproj",
    )(*args)


def _rank_kernel(ids_ref, rank_ref, cnt_ref, run_ref):
    i = pl.program_id(0)
    tr = ids_ref.shape[1]

    @pl.when(i == 0)
    def _():
        run_ref[...] = jnp.zeros_like(run_ref)

    ids = ids_ref[...]
    eidx = lax.broadcasted_iota(I32, (N_EXPERTS, tr), 0)
    hits = [eidx == ids[k:k + 1] for k in range(TOP_K)]
    member = (hits[0] | hits[1] | hits[2] | hits[3]).astype(F32)
    r = lax.broadcasted_iota(I32, (tr, tr), 0)
    c = lax.broadcasted_iota(I32, (tr, tr), 1)
    upper = (r < c).astype(BF16)
    prefix = jnp.dot(member.astype(BF16), upper, preferred_element_type=F32)
    base = run_ref[:, 0:1] + prefix
    ranks = [jnp.sum(jnp.where(hits[k], base, 0.0), axis=0, keepdims=True) for k in range(TOP_K)]
    rank_ref[...] = jnp.concatenate(ranks, 0).astype(I32)
    run_ref[...] = run_ref[...] + jnp.sum(member, axis=1, keepdims=True)
    cnt_ref[...] = run_ref[...]


def _expert_ranks(ids):
    n = ids.shape[1]
    return pl.pallas_call(
        _rank_kernel,
        grid=(n // T_RANK,),
        in_specs=[pl.BlockSpec((TOP_K, T_RANK), lambda i: (0, i))],
        out_specs=(pl.BlockSpec((TOP_K, T_RANK), lambda i: (0, i)),
                   pl.BlockSpec((N_EXPERTS, LANES), lambda i: (0, 0))),
        out_shape=(jax.ShapeDtypeStruct((TOP_K, n), I32),
                   jax.ShapeDtypeStruct((N_EXPERTS, LANES), F32)),
        scratch_shapes=[pltpu.VMEM((N_EXPERTS, LANES), F32)],
        compiler_params=_cparams(("arbitrary",)),
        name="expert_ranks",
    )(ids)


def _dest_kernel(ps_ref, ids_ref, rank_ref, dest_ref):
    ids = ids_ref[...]
    acc = rank_ref[...]
    for e in range(N_EXPERTS):
        acc = acc + jnp.where(ids == e, ps_ref[e], 0)
    dest_ref[...] = acc


def _destinations(pad_start, ids, rank):
    n = ids.shape[1]
    tn = 2048 if n % 2048 == 0 else 512
    return pl.pallas_call(
        _dest_kernel,
        grid_spec=pltpu.PrefetchScalarGridSpec(
            num_scalar_prefetch=1,
            grid=(n // tn,),
            in_specs=[pl.BlockSpec((TOP_K, tn), lambda i, ps: (0, i)),
                      pl.BlockSpec((TOP_K, tn), lambda i, ps: (0, i))],
            out_specs=pl.BlockSpec((TOP_K, tn), lambda i, ps: (0, i))),
        out_shape=jax.ShapeDtypeStruct((TOP_K, n), I32),
        compiler_params=_cparams(("arbitrary",)),
        name="destinations",
    )(pad_start, ids, rank)


_PAD_BITS = tuple(range(SUBLANES - 1)) + tuple(
    1 << j for j in range(SUBLANES.bit_length() - 1, MOE_ROWS.bit_length() - 1))


def _dispatch_kernel(pp_ref, np_ref, nu_ref, dest_ref, h_ref, xs_hbm, zeros_ref, sem, zsem):
    td = dest_ref.shape[1]
    n_blocks = xs_hbm.shape[0] // MOE_ROWS

    def pad_copy(e, bit):
        pos, npad = pp_ref[e], np_ref[e]
        head = (-pos) & (SUBLANES - 1)
        if bit < SUBLANES:
            return bit < head, pltpu.make_async_copy(zeros_ref.at[pl.ds(0, 1)],
                                                     xs_hbm.at[pl.ds(pos + bit, 1)], zsem)
        body = npad - head
        off = pl.multiple_of(pos + head + (body & (-2 * bit)), SUBLANES)
        return (body & bit) != 0, pltpu.make_async_copy(zeros_ref.at[pl.ds(0, bit)],
                                                        xs_hbm.at[pl.ds(off, bit)], zsem)

    def blk_copy(blk):
        off = pl.multiple_of(blk * MOE_ROWS, MOE_ROWS)
        return pltpu.make_async_copy(zeros_ref, xs_hbm.at[pl.ds(off, MOE_ROWS)], zsem)

    def pad_all(wait):
        def per_expert(e, carry):
            for bit in _PAD_BITS:
                cond, cp = pad_copy(e, bit)

                @pl.when(cond)
                def _():
                    cp.wait() if wait else cp.start()
            return carry

        def per_block(blk, carry):
            cp = blk_copy(blk)
            cp.wait() if wait else cp.start()
            return carry

        lax.fori_loop(0, N_EXPERTS, per_expert, 0)
        lax.fori_loop(nu_ref[0], n_blocks, per_block, 0)

    @pl.when(pl.program_id(0) == 0)
    def _():
        zeros_ref[...] = jnp.zeros_like(zeros_ref)
        pad_all(False)
        pad_all(True)

    def body(r, carry):
        for k in range(TOP_K):
            d = dest_ref[k, r]
            pltpu.make_async_copy(h_ref.at[pl.ds(r, 1)], xs_hbm.at[pl.ds(d, 1)], sem).start()
        return carry

    lax.fori_loop(0, td, body, 0, unroll=8)
    for k in range(TOP_K):
        pltpu.make_async_copy(h_ref, xs_hbm.at[pl.ds(0, td)], sem).wait()


def _dispatch(pad_pos, n_pad, n_used, dest, h2, n_rows):
    n, d = h2.shape
    return pl.pallas_call(
        _dispatch_kernel,
        grid_spec=pltpu.PrefetchScalarGridSpec(
            num_scalar_prefetch=3,
            grid=(n // T_DISPATCH,),
            in_specs=[pl.BlockSpec((TOP_K, T_DISPATCH), lambda i, *_: (0, i), memory_space=pltpu.SMEM),
                      pl.BlockSpec((T_DISPATCH, d), lambda i, *_: (i, 0))],
            out_specs=pl.BlockSpec(memory_space=pl.ANY),
            scratch_shapes=[pltpu.VMEM((MOE_ROWS, d), h2.dtype),
                            pltpu.SemaphoreType.DMA(()), pltpu.SemaphoreType.DMA(())]),
        out_shape=jax.ShapeDtypeStruct((n_rows, d), h2.dtype),
        compiler_params=pltpu.CompilerParams(dimension_semantics=("arbitrary",),
                                             has_side_effects=True, vmem_limit_bytes=VMEM_LIMIT),
        name="dispatch",
    )(pad_pos, n_pad, n_used, dest, h2)


def _expert_kernel(be_ref, nu_ref, xs_ref, wgu_ref, bgu_ref, wd_ref, bd_ref, ys_ref, wgu_bf, wd_bf):
    i = pl.program_id(0)
    nu = nu_ref[0]
    last = jnp.minimum(i, nu - 1)
    new_expert = (i == 0) | (be_ref[last] != be_ref[jnp.maximum(last - 1, 0)])

    @pl.when((i < nu) & new_expert)
    def _():
        wgu_bf[...] = wgu_ref[...].astype(BF16)
        wd_bf[...] = wd_ref[...].astype(BF16)

    @pl.when(i < nu)
    def _():
        x = xs_ref[...].astype(BF16)
        gu = jnp.dot(x, wgu_bf[...], preferred_element_type=F32) + bgu_ref[...]
        gate = jnp.minimum(gu[:, :D_FF], SWIGLU_LIMIT)
        lin = jnp.clip(gu[:, D_FF:], -SWIGLU_LIMIT, SWIGLU_LIMIT)
        act = gate * jax.nn.sigmoid(SWIGLU_ALPHA * gate) * (lin + 1.0)
        ys_ref[...] = jnp.dot(act.astype(BF16), wd_bf[...], preferred_element_type=F32) + bd_ref[...]

    @pl.when(i >= nu_ref[0])
    def _():
        ys_ref[...] = jnp.zeros_like(ys_ref)


def _expert_mlp(blk_e, n_used, xs, w_gu, b_gu, w_down, b_down):
    n_rows, d = xs.shape
    n_blocks = n_rows // MOE_ROWS

    def blk(i, nu):
        return jnp.minimum(i, nu[0] - 1)

    return pl.pallas_call(
        _expert_kernel,
        grid_spec=pltpu.PrefetchScalarGridSpec(
            num_scalar_prefetch=2,
            grid=(n_blocks,),
            in_specs=[
                pl.BlockSpec((MOE_ROWS, d), lambda i, be, nu: (blk(i, nu), 0)),
                pl.BlockSpec((None, d, 2 * D_FF), lambda i, be, nu: (be[blk(i, nu)], 0, 0)),
                pl.BlockSpec((None, 1, 2 * D_FF), lambda i, be, nu: (be[blk(i, nu)], 0, 0)),
                pl.BlockSpec((None, D_FF, d), lambda i, be, nu: (be[blk(i, nu)], 0, 0)),
                pl.BlockSpec((None, 1, d), lambda i, be, nu: (be[blk(i, nu)], 0, 0)),
            ],
            out_specs=pl.BlockSpec((MOE_ROWS, d), lambda i, be, nu: (i, 0)),
            scratch_shapes=[pltpu.VMEM((d, 2 * D_FF), BF16), pltpu.VMEM((D_FF, d), BF16)]),
        out_shape=jax.ShapeDtypeStruct((n_rows, d), F32),
        compiler_params=_cparams(("arbitrary",)),
        name="expert_mlp",
    )(blk_e, n_used, xs, w_gu, b_gu.reshape(N_EXPERTS, 1, 2 * D_FF), w_down,
      b_down.reshape(N_EXPERTS, 1, d))


def _combine_kernel(dest_ref, dnext_ref, gates_ref, ys_hbm, x1_ref, g2_ref, lng_ref, lnb_ref, out_ref,
                    buf, sem):
    tc = dest_ref.shape[1]
    i = pl.program_id(0)
    n = pl.num_programs(0)
    slot = i % 2

    def gather(d_ref, slot):
        def body(r, carry):
            for k in range(TOP_K):
                pltpu.make_async_copy(ys_hbm.at[pl.ds(d_ref[k, r], 1)], buf.at[slot, k, pl.ds(r, 1)],
                                      sem.at[slot]).start()
            return carry

        lax.fori_loop(0, tc, body, 0, unroll=8)

    @pl.when(i == 0)
    def _():
        gather(dest_ref, 0)

    @pl.when(i + 1 < n)
    def _():
        gather(dnext_ref, 1 - slot)

    for k in range(TOP_K):
        pltpu.make_async_copy(ys_hbm.at[pl.ds(0, tc)], buf.at[slot, k], sem.at[slot]).wait()
    gpad = jnp.concatenate([gates_ref[...], jnp.zeros((LANES - TOP_K, tc), F32)], 0)
    gt = gpad.T
    f = buf[slot, 0] * gt[:, 0:1]
    for k in range(1, TOP_K):
        f = f + buf[slot, k] * gt[:, k:k + 1]
    out_ref[...] = _layer_norm(DN_ALPHA * x1_ref[...] + g2_ref[...] * f, lng_ref[...], lnb_ref[...])


def _combine(dest, gates, ys, x1, g2, row_of_block, ln_g, ln_b):
    n, d = x1.shape
    tc = T_COMBINE
    nsteps = n // tc
    return pl.pallas_call(
        _combine_kernel,
        grid=(nsteps,),
        in_specs=[pl.BlockSpec((TOP_K, tc), lambda i: (0, i), memory_space=pltpu.SMEM),
                  pl.BlockSpec((TOP_K, tc), lambda i: (0, jnp.minimum(i + 1, nsteps - 1)),
                               memory_space=pltpu.SMEM),
                  pl.BlockSpec((TOP_K, tc), lambda i: (0, i)),
                  pl.BlockSpec(memory_space=pl.ANY),
                  pl.BlockSpec((tc, d), lambda i: (i, 0)),
                  pl.BlockSpec((None, 1, d), lambda i: (row_of_block(i), 0, 0)),
                  pl.BlockSpec(ln_g.shape, lambda i: (0, 0)),
                  pl.BlockSpec(ln_b.shape, lambda i: (0, 0))],
        out_specs=pl.BlockSpec((tc, d), lambda i: (i, 0)),
        out_shape=jax.ShapeDtypeStruct((n, d), F32),
        scratch_shapes=[pltpu.VMEM((2, TOP_K, tc, d), F32), pltpu.SemaphoreType.DMA((2,))],
        compiler_params=_cparams(("arbitrary",)),
        name="combine",
    )(dest, dest, gates, ys, x1, g2, ln_g, ln_b)


def _moe(h2, ids, gates, x1, g2, row_of_block, ln_g, ln_b, w_gu, b_gu, w_down, b_down):
    n = h2.shape[0]
    nk = n * TOP_K
    n_blocks = -(-nk // MOE_ROWS) + N_EXPERTS
    rank, cnt = _expert_ranks(ids)
    counts = cnt[:, 0].astype(I32)
    padded = (counts + MOE_ROWS - 1) // MOE_ROWS * MOE_ROWS
    pad_end = jnp.cumsum(padded)
    pad_start = (pad_end - padded).astype(I32)
    blk_start = jnp.arange(n_blocks, dtype=I32) * MOE_ROWS
    blk_e = jnp.minimum(jnp.sum((pad_end[None, :] <= blk_start[:, None]).astype(I32), axis=1),
                        N_EXPERTS - 1).astype(I32)
    n_used = (pad_end[-1:] // MOE_ROWS).astype(I32)
    dest = _destinations(pad_start, ids, rank)
    xs = _dispatch(pad_start + counts, padded - counts, n_used, dest, h2, n_blocks * MOE_ROWS)
    ys = _expert_mlp(blk_e, n_used, xs, w_gu, b_gu, w_down, b_down)
    return _combine(dest, gates, ys, x1, g2, row_of_block, ln_g, ln_b)


def _odd_inproj_kernel(x_ref, sc_ref, sh_ref, w_ref, c_ref, s_ref, q_ref, k_ref, v_ref):
    h = (x_ref[...] * (1.0 + sc_ref[...]) + sh_ref[...]).astype(BF16)
    y = jnp.dot(h, w_ref[...], preferred_element_type=F32)
    cs = c_ref[...]
    sn = s_ref[...]
    nq = C_HEADS * C_HEAD_DIM // LANES
    pieces = []
    for j in range(nq + 1):
        t = y[:, j * LANES:(j + 1) * LANES]
        if j < nq:
            pieces.append((t * cs + _swap16_lanes(t) * sn) * (C_HEAD_DIM ** -0.5))
        else:
            pieces.append(t * cs + _swap16_lanes(t) * sn)
    q_ref[...] = jnp.concatenate(pieces[:nq], 1).astype(BF16)
    k_ref[...] = pieces[nq].astype(BF16)
    v_ref[...] = y[:, (nq + 1) * LANES:].astype(BF16)


def _odd_inproj(x, x_off, b, s, sc, sh, mod_row, w, cs, sn, tm):
    d = x.shape[1]
    qw = C_HEADS * C_HEAD_DIM
    nb = s // tm
    xo = x_off // tm
    return pl.pallas_call(
        _odd_inproj_kernel,
        grid=(b, nb),
        in_specs=[pl.BlockSpec((tm, d), lambda bi, i: (xo + bi * nb + i, 0)),
                  pl.BlockSpec((None, 1, d), lambda bi, i: (mod_row(bi), 0, 0)),
                  pl.BlockSpec((None, 1, d), lambda bi, i: (mod_row(bi), 0, 0)),
                  pl.BlockSpec(w.shape, lambda bi, i: (0, 0)),
                  pl.BlockSpec((tm, LANES), lambda bi, i: (i, 0)),
                  pl.BlockSpec((tm, LANES), lambda bi, i: (i, 0))],
        out_specs=(pl.BlockSpec((None, tm, qw), lambda bi, i: (bi, i, 0)),
                   pl.BlockSpec((None, tm, LANES), lambda bi, i: (bi, i, 0)),
                   pl.BlockSpec((None, tm, LANES), lambda bi, i: (bi, i, 0))),
        out_shape=(jax.ShapeDtypeStruct((b, s, qw), BF16),
                   jax.ShapeDtypeStruct((b, s, LANES), BF16),
                   jax.ShapeDtypeStruct((b, s, LANES), BF16)),
        compiler_params=_cparams(("parallel", "arbitrary")),
        name="odd_inproj",
    )(x, sc, sh, w, cs, sn)


def _win_attn_kernel(q_ref, kp_ref, k0_ref, kn_ref, kc_ref, vp_ref, v0_ref, vn_ref, vc_ref, sink_ref, o_ref):
    tq = q_ref.shape[0]
    i = pl.program_id(1)
    nb = pl.num_programs(1)
    q = q_ref[...]
    lane = lax.broadcasted_iota(I32, (tq, LANES), 1)
    lo = lane < C_HEAD_DIM
    zero = jnp.zeros((tq, LANES), BF16)
    n_pairs = C_HEADS // C_KV_HEADS
    rows = []
    for j in range(n_pairs):
        t = q[:, j * LANES:(j + 1) * LANES]
        rows.append(jnp.where(lo, t, zero))
        rows.append(jnp.where(lo, zero, t))
    qs = jnp.concatenate(rows, 0)
    kb = jnp.concatenate([kp_ref[...], k0_ref[...], kn_ref[...], kc_ref[...]], 0)
    vb = jnp.concatenate([vp_ref[...], v0_ref[...], vn_ref[...], vc_ref[...]], 0)
    nk = kb.shape[0]
    s = lax.dot_general(qs, kb, (((1,), (1,)), ((), ())), preferred_element_type=F32)
    s = s.reshape(C_HEADS, tq, nk)
    qpos = lax.broadcasted_iota(I32, (tq, nk), 0)
    col = lax.broadcasted_iota(I32, (tq, nk), 1)
    ok = (jnp.abs(col - tq - qpos) <= C_WINDOW)
    ok = ok & ((col >= tq) | (i > 0)) & ((col < 2 * tq) | (i < nb - 1))
    ok = ok | (col >= 3 * tq)
    s = jnp.where(ok[None], s, NEG_INF)
    sink = sink_ref[...]
    m = jnp.maximum(jnp.max(s, axis=-1, keepdims=True), sink)
    p = jnp.exp(s - m)
    den = jnp.sum(p, axis=-1, keepdims=True) + jnp.exp(sink - m)
    o = jnp.dot(p.astype(BF16).reshape(C_HEADS * tq, nk), vb, preferred_element_type=F32)
    o = o / den.reshape(C_HEADS * tq, 1)
    outs = []
    for j in range(n_pairs):
        a = o[(2 * j) * tq:(2 * j + 1) * tq]
        b = o[(2 * j + 1) * tq:(2 * j + 2) * tq]
        outs.append(jnp.where(lo, a, b))
    o_ref[...] = jnp.concatenate(outs, 1).astype(BF16)


def _win_attention(q, k, v, kc, vc, sink):
    b, s, qw = q.shape
    tq = TQ_WIN
    nb = s // tq
    lc = kc.shape[1]

    def blk(off):
        return pl.BlockSpec((None, tq, LANES), lambda bi, i: (bi, jnp.clip(i + off, 0, nb - 1), 0))

    ctx = pl.BlockSpec((None, lc, LANES), lambda bi, i: (bi, 0, 0))
    return pl.pallas_call(
        _win_attn_kernel,
        grid=(b, nb),
        in_specs=[pl.BlockSpec((None, tq, qw), lambda bi, i: (bi, i, 0)),
                  blk(-1), blk(0), blk(1), ctx, blk(-1), blk(0), blk(1), ctx,
                  pl.BlockSpec(sink.shape, lambda bi, i: (0, 0, 0))],
        out_specs=pl.BlockSpec((None, tq, qw), lambda bi, i: (bi, i, 0)),
        out_shape=jax.ShapeDtypeStruct((b, s, qw), BF16),
        compiler_params=_cparams(("parallel", "arbitrary")),
        name="window_attention",
    )(q, k, k, k, kc, v, v, v, vc, sink)


def kernel(x, c, ctx, c_ctx, ada_w, ada_b, ln_g, ln_b, ab_w_in, ab_w_out, diff_lambda, diff_subln_g,
           conv_w, c_w_in, c_w_out, c_sink, router_w, router_b, w_gu, b_gu, w_down, b_down):
    b, s, d = x.shape
    lc = ctx.shape[1]
    n_c = b * lc
    ctx_row = b

    cvec = jnp.zeros((8, d), F32).at[:b].set(c).at[b].set(c_ctx)
    mod = _modulation(cvec, ada_w, ada_b)

    def mod_vec(layer, j):
        return mod[layer, :, j * d:(j + 1) * d].reshape(8, 1, d)

    lat_row = lambda bi: bi
    ctx_rowf = lambda bi: ctx_row

    cos64, sin64 = _rope_tables(s)
    cos_l = jnp.concatenate([cos64, cos64], -1)
    sin_l = jnp.concatenate([sin64, sin64], -1)
    one_c = jnp.ones((lc, LANES), F32)
    zero_c = jnp.zeros((lc, LANES), F32)
    qscale = A_QK_DIM ** -0.5 * math.log2(math.e)

    l = 0
    w_in = ab_w_in[0]
    wn = w_in[:, A_WIDTH:].astype(BF16)
    wn = jnp.concatenate([wn[:, :A_WIDTH], wn[:, 2 * A_WIDTH:]], 1)
    wt = jnp.concatenate([w_in[:, :A_WIDTH], w_in[:, 2 * A_WIDTH:3 * A_WIDTH]], 1).T.astype(BF16)
    sc1, sh1, g1 = mod_vec(l, 1), mod_vec(l, 0), mod_vec(l, 2)
    sh2, sc2, g2 = mod_vec(l, 3), mod_vec(l, 4), mod_vec(l, 5)
    lam_init = 0.8 - 0.6 * math.exp(-0.3 * l)

    x_tok = x.reshape(b * s, d)
    ctx_tok = ctx.reshape(n_c, d)
    qT, k, vT, bg, u = _even_inproj(x_tok, 0, b, s, sc1, sh1, lat_row, wn, wt, cos_l, sin_l,
                                    (cos_l * qscale).T, (sin_l * qscale).T, TM_PROJ)
    qTc, kc, vTc, bgc, uc = _even_inproj(ctx_tok, 0, b, lc, sc1, sh1, ctx_rowf, wn, wt, one_c, zero_c,
                                         (one_c * qscale).T, zero_c.T, lc)
    g_col = diff_subln_g[0].reshape(A_V_DIM, 1)
    a_lat = _diff_attention(qT, k, vT, kc, vTc, diff_lambda[0], g_col, lam_init, TQ_DIFF)
    a_ctx = _diff_attention(qTc, None, None, kc, vTc, diff_lambda[0], g_col, lam_init, lc)

    w_out = ab_w_out[0].astype(BF16)
    wa, wb = w_out[:A_WIDTH], w_out[A_WIDTH:]
    lng0, lnb0 = ln_g[l, 0].reshape(1, d), ln_b[l, 0].reshape(1, d)
    lng1, lnb1 = ln_g[l, 1].reshape(1, d), ln_b[l, 1].reshape(1, d)
    rwT = router_w[l].T.astype(BF16)
    rb = router_b[l].reshape(N_EXPERTS, 1)
    n0 = n_c + b * s
    empty = (jnp.zeros((n0, d), F32), jnp.zeros((n0, d), F32),
             jnp.zeros((TOP_K, n0), I32), jnp.zeros((TOP_K, n0), F32))
    part = _outproj(a_ctx, (bgc, uc, conv_w[0]), wa, wb, ctx_tok, 0, (g1, sc2, sh2), ctx_rowf,
                    lng0, lnb0, rwT, rb, lc, 0, empty)
    x1a, h2a, idsa, gatesa = _outproj(a_lat, (bg, u, conv_w[0]), wa, wb, x_tok, 0, (g1, sc2, sh2), lat_row,
                                      lng0, lnb0, rwT, rb, TM_PROJ, n_c, part)
    ncb = n_c // T_COMBINE
    spb = s // T_COMBINE
    row_of_block0 = lambda i: jnp.where(i < ncb, ctx_row, jnp.maximum(i - ncb, 0) // spb)
    y_all = _moe(h2a, idsa, gatesa, x1a, g2, row_of_block0, lng1, lnb1,
                 w_gu[l], b_gu[l], w_down[l], b_down[l])

    l = 1
    sc1, sh1, g1 = mod_vec(l, 1), mod_vec(l, 0), mod_vec(l, 2)
    sh2, sc2, g2 = mod_vec(l, 3), mod_vec(l, 4), mod_vec(l, 5)
    g = C_HEADS // C_KV_HEADS
    perm = jnp.array([(kv * g + j) * C_HEAD_DIM + dd for j in range(g) for kv in range(C_KV_HEADS)
                      for dd in range(C_HEAD_DIM)], I32)
    w_in = c_w_in[0]
    qw = C_HEADS * C_HEAD_DIM
    w_odd = jnp.concatenate([w_in[:, :qw][:, perm], w_in[:, qw:]], 1).astype(BF16)
    q, k, v = _odd_inproj(y_all, n_c, b, s, sc1, sh1, lat_row, w_odd, cos_l, sin_l, TM_PROJ)
    _, kc, vc = _odd_inproj(y_all, 0, b, lc, sc1, sh1, ctx_rowf, w_odd, one_c, zero_c, lc)
    sink = c_sink[0][perm[::C_HEAD_DIM] // C_HEAD_DIM].reshape(C_HEADS, 1, 1)
    o = _win_attention(q, k, v, kc, vc, sink)
    wa = c_w_out[0][perm].astype(BF16)
    lng0, lnb0 = ln_g[l, 0].reshape(1, d), ln_b[l, 0].reshape(1, d)
    lng1, lnb1 = ln_g[l, 1].reshape(1, d), ln_b[l, 1].reshape(1, d)
    rwT = router_w[l].T.astype(BF16)
    rb = router_b[l].reshape(N_EXPERTS, 1)
    x1a, h2a, idsa, gatesa = _outproj(o, None, wa, None, y_all, n_c, (g1, sc2, sh2), lat_row,
                                      lng0, lnb0, rwT, rb, TM_PROJ, 0, ())
    row_of_block1 = lambda i: i // spb
    y = _moe(h2a, idsa, gatesa, x1a, g2, row_of_block1, lng1, lnb1,
             w_gu[l], b_gu[l], w_down[l], b_down[l])
    return y.reshape(b, s, d)
```

```python
import functools
import math

import jax
import jax.numpy as jnp
from jax import lax
from jax.experimental import pallas as pl
from jax.experimental.pallas import tpu as pltpu

F32 = jnp.float32
BF16 = jnp.bfloat16
I32 = jnp.int32

D_MODEL = 1024
DEPTH = 2
GRID_W = 64
ROPE_DIM = 64
ROPE_BASE = 10000.0
A_HEADS = 4
A_QK_DIM = 64
A_V_DIM = 128
A_WIDTH = 512
B_WIDTH = 512
C_HEADS = 16
C_KV_HEADS = 2
C_HEAD_DIM = 64
C_WINDOW = 128
N_EXPERTS = 32
TOP_K = 4
D_FF = 1024
SWIGLU_ALPHA = 1.702
SWIGLU_LIMIT = 7.0
LN_EPS = 1e-5
RMS_EPS = 1e-5
NEG_INF = -1e30
DN_ALPHA = (2 * DEPTH) ** 0.25

LANES = 128
SUBLANES = 8
VMEM_LIMIT = 56 * 1024 * 1024

TM_PROJ = 512
TQ_DIFF = 512
TQ_WIN = 128
MOE_ROWS = 512
T_RANK = 512
T_DISPATCH = 512
T_COMBINE = 256


def _cparams(sem):
    return pltpu.CompilerParams(dimension_semantics=sem, vmem_limit_bytes=VMEM_LIMIT)


def _mod_kernel(c_ref, w_ref, b_ref, o_ref):
    c = c_ref[...]
    s = c * jax.nn.sigmoid(c)
    o_ref[...] = jnp.dot(s.astype(BF16), w_ref[...].astype(BF16),
                         preferred_element_type=F32) + b_ref[...]


def _modulation(cvec, ada_w, ada_b):
    d = D_MODEL
    tn = 1536
    return pl.pallas_call(
        _mod_kernel,
        grid=(DEPTH, 6 * d // tn),
        in_specs=[pl.BlockSpec((8, d), lambda l, j: (0, 0)),
                  pl.BlockSpec((None, d, tn), lambda l, j: (l, 0, j)),
                  pl.BlockSpec((None, 1, tn), lambda l, j: (l, 0, j))],
        out_specs=pl.BlockSpec((None, 8, tn), lambda l, j: (l, 0, j)),
        out_shape=jax.ShapeDtypeStruct((DEPTH, 8, 6 * d), F32),
        compiler_params=_cparams(("arbitrary", "arbitrary")),
        name="modulation",
    )(cvec, ada_w, ada_b.reshape(DEPTH, 1, 6 * d))


def _swap16_lanes(t):
    lane = lax.broadcasted_iota(I32, t.shape, 1)
    first = (lane % 32) < 16
    return jnp.where(first, pltpu.roll(t, LANES - 16, 1), pltpu.roll(t, 16, 1))


def _swap16_rows(a):
    pieces = []
    for i in range(0, a.shape[0], 32):
        pieces.append(a[i + 16:i + 32])
        pieces.append(a[i:i + 16])
    return jnp.concatenate(pieces, 0)


def _rope_tables(s):
    rows = jnp.repeat(jnp.arange(s // GRID_W, dtype=F32), GRID_W)
    cols = jnp.tile(jnp.arange(GRID_W, dtype=F32), s // GRID_W)
    axis_dim = ROPE_DIM // 2
    inv = ROPE_BASE ** (-jnp.arange(0, axis_dim, 2, dtype=F32) / axis_dim)
    ar, ac = rows[:, None] * inv, cols[:, None] * inv
    cr, sr, cc, sc = jnp.cos(ar), jnp.sin(ar), jnp.cos(ac), jnp.sin(ac)
    cos64 = jnp.concatenate([cr, cr, cc, cc], -1)
    sin64 = jnp.concatenate([-sr, sr, -sc, sc], -1)
    return cos64, sin64


def _even_inproj_kernel(x_ref, sc_ref, sh_ref, wn_ref, wt_ref, ck_ref, sk_ref, cq_ref, sq_ref,
                        qT_ref, k_ref, vT_ref, bg_ref, u_ref):
    tm = x_ref.shape[0]
    h = (x_ref[...] * (1.0 + sc_ref[...]) + sh_ref[...]).astype(BF16)
    yn = jnp.dot(h, wn_ref[...], preferred_element_type=F32)
    ck = ck_ref[...]
    sk = sk_ref[...]
    pieces = []
    for j in range(A_HEADS):
        t = yn[:, j * LANES:(j + 1) * LANES]
        pieces.append(t * ck + _swap16_lanes(t) * sk)
    k_ref[...] = jnp.concatenate(pieces, 1).astype(BF16)
    bg_ref[...] = yn[:, A_WIDTH:A_WIDTH + B_WIDTH]
    u_ref[...] = yn[:, A_WIDTH + B_WIDTH:A_WIDTH + 2 * B_WIDTH] * yn[:, A_WIDTH + 2 * B_WIDTH:]
    yt = lax.dot_general(wt_ref[...], h, (((1,), (1,)), ((), ())),
                         preferred_element_type=F32)
    qt = yt[:A_WIDTH]
    cq = jnp.concatenate([cq_ref[...]] * A_HEADS, 0)
    sq = jnp.concatenate([sq_ref[...]] * A_HEADS, 0)
    qT_ref[...] = (qt * cq + _swap16_rows(qt) * sq).astype(BF16).reshape(A_HEADS, A_V_DIM, tm)
    vT_ref[...] = yt[A_WIDTH:].astype(BF16).reshape(A_HEADS, A_V_DIM, tm)


def _even_inproj(x, x_off, b, s, sc, sh, mod_row, wn, wt, ck, sk, cq, sq, tm):
    d = x.shape[1]
    nb = s // tm
    xo = x_off // tm
    out_shape = (
        jax.ShapeDtypeStruct((b, A_HEADS, A_V_DIM, s), BF16),
        jax.ShapeDtypeStruct((b, s, A_WIDTH), BF16),
        jax.ShapeDtypeStruct((b, A_HEADS, nb, A_V_DIM, tm), BF16),
        jax.ShapeDtypeStruct((b, s, B_WIDTH), F32),
        jax.ShapeDtypeStruct((b, s, B_WIDTH), F32),
    )
    return pl.pallas_call(
        _even_inproj_kernel,
        grid=(b, nb),
        in_specs=[
            pl.BlockSpec((tm, d), lambda bi, i: (xo + bi * nb + i, 0)),
            pl.BlockSpec((None, 1, d), lambda bi, i: (mod_row(bi), 0, 0)),
            pl.BlockSpec((None, 1, d), lambda bi, i: (mod_row(bi), 0, 0)),
            pl.BlockSpec(wn.shape, lambda bi, i: (0, 0)),
            pl.BlockSpec(wt.shape, lambda bi, i: (0, 0)),
            pl.BlockSpec((tm, LANES), lambda bi, i: (i, 0)),
            pl.BlockSpec((tm, LANES), lambda bi, i: (i, 0)),
            pl.BlockSpec((LANES, tm), lambda bi, i: (0, i)),
            pl.BlockSpec((LANES, tm), lambda bi, i: (0, i)),
        ],
        out_specs=(
            pl.BlockSpec((None, A_HEADS, A_V_DIM, tm), lambda bi, i: (bi, 0, 0, i)),
            pl.BlockSpec((None, tm, A_WIDTH), lambda bi, i: (bi, i, 0)),
            pl.BlockSpec((None, A_HEADS, None, A_V_DIM, tm), lambda bi, i: (bi, 0, i, 0, 0)),
            pl.BlockSpec((None, tm, B_WIDTH), lambda bi, i: (bi, i, 0)),
            pl.BlockSpec((None, tm, B_WIDTH), lambda bi, i: (bi, i, 0)),
        ),
        out_shape=out_shape,
        compiler_params=_cparams(("parallel", "arbitrary")),
        name="even_inproj",
    )(x, sc, sh, wn, wt, ck, sk, cq, sq)


def _diff_attn_kernel(*refs, n_chunks, lam_init, with_lat):
    if with_lat:
        (qT_ref, k_ref, vT_ref, kc_ref, vTc_ref, lamp_ref, g_ref, o_ref,
         acc1, acc2, sa1, sa2, sb1, sb2) = refs
    else:
        qT_ref, kc_ref, vTc_ref, lamp_ref, g_ref, o_ref, acc1, acc2 = refs
    tq = qT_ref.shape[1]
    qT = qT_ref[...]
    row = lax.broadcasted_iota(I32, qT.shape, 0)
    zero = jnp.zeros_like(qT)
    q1 = jnp.where(row < A_QK_DIM, qT, zero)
    q2 = jnp.where(row >= A_QK_DIM, qT, zero)
    acc1[...] = jnp.zeros_like(acc1)
    acc2[...] = jnp.zeros_like(acc2)

    def one_map(s, vTc, m, l, acc):
        m_new = jnp.maximum(m, jnp.max(s, axis=0, keepdims=True))
        alpha = jnp.exp2(m - m_new)
        p = jnp.exp2(s - m_new)
        l_new = alpha * l + jnp.sum(p, axis=0, keepdims=True)
        acc[...] = alpha * acc[...] + jnp.dot(vTc, p.astype(BF16), preferred_element_type=F32)
        return m_new, l_new

    def softmax_pv(s1, s2, vTc, carry):
        m1, l1, m2, l2 = carry
        m1, l1 = one_map(s1, vTc, m1, l1, acc1)
        m2, l2 = one_map(s2, vTc, m2, l2, acc2)
        return m1, l1, m2, l2

    neg = jnp.full((1, tq), -jnp.inf, F32)
    zer = jnp.zeros((1, tq), F32)
    kc = kc_ref[...]
    carry = softmax_pv(jnp.dot(kc, q1, preferred_element_type=F32),
                       jnp.dot(kc, q2, preferred_element_type=F32), vTc_ref[...], (neg, zer, neg, zer))
    if with_lat:
        tk = vT_ref.shape[2]

        def scores(c, s1_ref, s2_ref):
            kk = k_ref[pl.ds(pl.multiple_of(c * tk, tk), tk), :]
            s1_ref[...] = jnp.dot(kk, q1, preferred_element_type=F32)
            s2_ref[...] = jnp.dot(kk, q2, preferred_element_type=F32)

        scores(0, sa1, sa2)

        def body(j, carry):
            c = 2 * j
            scores(c + 1, sb1, sb2)
            carry = softmax_pv(sa1[...], sa2[...], vT_ref[c], carry)
            scores(c + 2, sa1, sa2)
            return softmax_pv(sb1[...], sb2[...], vT_ref[c + 1], carry)

        carry = lax.fori_loop(0, n_chunks // 2 - 1, body, carry)
        scores(n_chunks - 1, sb1, sb2)
        carry = softmax_pv(sa1[...], sa2[...], vT_ref[n_chunks - 2], carry)
        carry = softmax_pv(sb1[...], sb2[...], vT_ref[n_chunks - 1], carry)
    m1, l1, m2, l2 = carry

    lf = lamp_ref[...]
    e1 = jnp.exp(jnp.sum(lf[0:1] * lf[1:2], axis=1, keepdims=True))
    e2 = jnp.exp(jnp.sum(lf[2:3] * lf[3:4], axis=1, keepdims=True))
    lam = e1 - e2 + lam_init
    o = acc1[...] / l1 - lam * (acc2[...] / l2)
    o = o * lax.rsqrt(jnp.mean(o * o, axis=0, keepdims=True) + RMS_EPS)
    o = o * g_ref[...] * (1.0 - lam_init)
    o_ref[...] = o.T.astype(BF16)


def _diff_attention(qT, k, vT, kc, vTc, lam_p, g_col, lam_init, tq):
    b, _, _, s = qT.shape
    lc = kc.shape[1]
    with_lat = k is not None
    in_specs = [pl.BlockSpec((None, None, A_V_DIM, tq), lambda bi, h, i: (bi, h, 0, i))]
    args = [qT]
    n_chunks = 0
    if with_lat:
        sk = k.shape[1]
        n_chunks, tk = vT.shape[2], vT.shape[4]
        in_specs += [pl.BlockSpec((None, sk, LANES), lambda bi, h, i: (bi, 0, h)),
                     pl.BlockSpec((None, None, n_chunks, A_V_DIM, tk), lambda bi, h, i: (bi, h, 0, 0, 0))]
        args += [k, vT]
    in_specs += [pl.BlockSpec((None, lc, LANES), lambda bi, h, i: (bi, 0, h)),
                 pl.BlockSpec((None, None, None, A_V_DIM, lc), lambda bi, h, i: (bi, h, 0, 0, 0)),
                 pl.BlockSpec(lam_p.shape, lambda bi, h, i: (0, 0)),
                 pl.BlockSpec(g_col.shape, lambda bi, h, i: (0, 0))]
    args += [kc, vTc, lam_p, g_col]
    scratch = [pltpu.VMEM((A_V_DIM, tq), F32), pltpu.VMEM((A_V_DIM, tq), F32)]
    if with_lat:
        assert n_chunks % 2 == 0 and n_chunks >= 2
        scratch += [pltpu.VMEM((tk, tq), F32)] * 4
    return pl.pallas_call(
        functools.partial(_diff_attn_kernel, n_chunks=n_chunks, lam_init=lam_init, with_lat=with_lat),
        grid=(b, A_HEADS, s // tq),
        in_specs=in_specs,
        out_specs=pl.BlockSpec((None, tq, A_V_DIM), lambda bi, h, i: (bi, i, h)),
        out_shape=jax.ShapeDtypeStruct((b, s, A_WIDTH), BF16),
        scratch_shapes=scratch,
        compiler_params=_cparams(("parallel", "parallel", "arbitrary")),
        name="diff_attention_lat" if with_lat else "diff_attention_ctx",
    )(*args)


def _layer_norm(z, g, b):
    mu = jnp.mean(z, axis=-1, keepdims=True)
    zc = z - mu
    var = jnp.mean(zc * zc, axis=-1, keepdims=True)
    return zc * lax.rsqrt(var + LN_EPS) * g + b


def _top4_router(h2, rwT_ref, rb_ref, ids_ref, gates_ref):
    tm = h2.shape[0]
    lt = lax.dot_general(rwT_ref[...], h2.astype(BF16), (((1,), (1,)), ((), ())),
                         preferred_element_type=F32) + rb_ref[...]
    eidx = lax.broadcasted_iota(I32, lt.shape, 0)
    vals, ids = [], []
    for _ in range(TOP_K):
        mx = jnp.max(lt, axis=0, keepdims=True)
        idx = jnp.min(jnp.where(lt == mx, eidx, N_EXPERTS), axis=0, keepdims=True)
        vals.append(mx)
        ids.append(idx)
        lt = jnp.where(eidx == idx, -jnp.inf, lt)
    es = [jnp.exp(v - vals[0]) for v in vals]
    den = es[0] + es[1] + es[2] + es[3]
    ids_ref[...] = jnp.concatenate(ids, 0)
    gates_ref[...] = jnp.concatenate([e / den for e in es], 0)
    del tm


def _outproj_kernel(*refs, has_conv):
    if has_conv:
        (a_ref, bg_ref, u_ref, up_ref, un_ref, cw_ref, wa_ref, wb_ref, x_ref, g1_ref, lng_ref, lnb_ref,
         sc2_ref, sh2_ref, rwT_ref, rb_ref, x1_ref, h2_ref, ids_ref, gates_ref) = refs
    else:
        (a_ref, wa_ref, x_ref, g1_ref, lng_ref, lnb_ref,
         sc2_ref, sh2_ref, rwT_ref, rb_ref, x1_ref, h2_ref, ids_ref, gates_ref) = refs
    y = jnp.dot(a_ref[...], wa_ref[...], preferred_element_type=F32)
    if has_conv:
        tm = u_ref.shape[0]
        i = pl.program_id(1)
        nb = pl.num_programs(1)
        u = u_ref[...]
        prev = jnp.where(i > 0, up_ref[SUBLANES - 1:SUBLANES, :], 0.0)
        nxt = jnp.where(i < nb - 1, un_ref[0:1, :], 0.0)
        r = lax.broadcasted_iota(I32, u.shape, 0)
        um1 = jnp.where(r == 0, prev, pltpu.roll(u, 1, 0))
        up1 = jnp.where(r == tm - 1, nxt, pltpu.roll(u, tm - 1, 0))
        cw = cw_ref[...]
        conv = um1 * cw[0:1] + u * cw[1:2] + up1 * cw[2:3]
        b_mix = (bg_ref[...] * conv).astype(BF16)
        y = y + jnp.dot(b_mix, wb_ref[...], preferred_element_type=F32)
    x1 = _layer_norm(DN_ALPHA * x_ref[...] + g1_ref[...] * y, lng_ref[...], lnb_ref[...])
    x1_ref[...] = x1
    h2 = x1 * (1.0 + sc2_ref[...]) + sh2_ref[...]
    h2_ref[...] = h2
    _top4_router(h2, rwT_ref, rb_ref, ids_ref, gates_ref)


def _outproj(a, conv_in, wa, wb, x, x_off, mod_vecs, mod_row, ln_g, ln_b, rwT, rb, tm, tok_off, prev_out):
    b, s = a.shape[0], a.shape[1]
    d = x.shape[1]
    nb = s // tm
    g1, sc2, sh2 = mod_vecs
    has_conv = conv_in is not None
    n_tok = prev_out[0].shape[0] if prev_out else b * s
    ob = tok_off // tm
    xo = x_off // tm

    def row_spec(width):
        return pl.BlockSpec((None, tm, width), lambda bi, i: (bi, i, 0))

    def vec_spec():
        return pl.BlockSpec((None, 1, d), lambda bi, i: (mod_row(bi), 0, 0))

    def full(arr):
        return pl.BlockSpec(arr.shape, lambda bi, i: (0,) * arr.ndim)

    in_specs = [row_spec(a.shape[-1])]
    args = [a]
    if has_conv:
        bg, u, cw = conv_in
        s8 = s // SUBLANES
        t8 = tm // SUBLANES
        in_specs += [row_spec(B_WIDTH), row_spec(B_WIDTH),
                     pl.BlockSpec((None, SUBLANES, B_WIDTH),
                                  lambda bi, i: (bi, jnp.maximum(i * t8 - 1, 0), 0)),
                     pl.BlockSpec((None, SUBLANES, B_WIDTH),
                                  lambda bi, i: (bi, jnp.minimum((i + 1) * t8, s8 - 1), 0)),
                     full(cw), full(wa), full(wb)]
        args += [bg, u, u, u, cw, wa, wb]
    else:
        in_specs += [full(wa)]
        args += [wa]
    in_specs += [pl.BlockSpec((tm, d), lambda bi, i: (xo + bi * nb + i, 0)),
                 vec_spec(), full(ln_g), full(ln_b), vec_spec(), vec_spec(), full(rwT), full(rb)]
    args += [x, g1, ln_g, ln_b, sc2, sh2, rwT, rb]
    n_in = len(args)
    n_alias = len(prev_out)
    in_specs += [pl.BlockSpec(memory_space=pl.ANY)] * n_alias
    args += list(prev_out)
    out_specs = (
        pl.BlockSpec((tm, d), lambda bi, i: (ob + bi * nb + i, 0)),
        pl.BlockSpec((tm, d), lambda bi, i: (ob + bi * nb + i, 0)),
        pl.BlockSpec((TOP_K, tm), lambda bi, i: (0, ob + bi * nb + i)),
        pl.BlockSpec((TOP_K, tm), lambda bi, i: (0, ob + bi * nb + i)),
    )
    out_shape = (
        jax.ShapeDtypeStruct((n_tok, d), F32),
        jax.ShapeDtypeStruct((n_tok, d), F32),
        jax.ShapeDtypeStruct((TOP_K, n_tok), I32),
        jax.ShapeDtypeStruct((TOP_K, n_tok), F32),
    )

    def kern(*refs):
        refs = refs[:n_in] + refs[n_in + n_alias:]
        _outproj_kernel(*refs, has_conv=has_conv)

    return pl.pallas_call(
        kern,
        grid=(b, nb),
        in_specs=in_specs,
        out_specs=out_specs,
        out_shape=out_shape,
        input_output_aliases={n_in + j: j for j in range(n_alias)},
        compiler_params=_cparams(("parallel", "arbitrary")),
        name="outproj_conv" if has_conv else "outproj",
    )(*args)


def _rank_kernel(ids_ref, rank_ref, cnt_ref, run_ref):
    i = pl.program_id(0)
    tr = ids_ref.shape[1]

    @pl.when(i == 0)
    def _():
        run_ref[...] = jnp.zeros_like(run_ref)

    ids = ids_ref[...]
    eidx = lax.broadcasted_iota(I32, (N_EXPERTS, tr), 0)
    hits = [eidx == ids[k:k + 1] for k in range(TOP_K)]
    member = (hits[0] | hits[1] | hits[2] | hits[3]).astype(F32)
    r = lax.broadcasted_iota(I32, (tr, tr), 0)
    c = lax.broadcasted_iota(I32, (tr, tr), 1)
    upper = (r < c).astype(BF16)
    prefix = jnp.dot(member.astype(BF16), upper, preferred_element_type=F32)
    base = run_ref[:, 0:1] + prefix
    ranks = [jnp.sum(jnp.where(hits[k], base, 0.0), axis=0, keepdims=True) for k in range(TOP_K)]
    rank_ref[...] = jnp.concatenate(ranks, 0).astype(I32)
    run_ref[...] = run_ref[...] + jnp.sum(member, axis=1, keepdims=True)
    cnt_ref[...] = run_ref[...]


def _expert_ranks(ids):
    n = ids.shape[1]
    return pl.pallas_call(
        _rank_kernel,
        grid=(n // T_RANK,),
        in_specs=[pl.BlockSpec((TOP_K, T_RANK), lambda i: (0, i))],
        out_specs=(pl.BlockSpec((TOP_K, T_RANK), lambda i: (0, i)),
                   pl.BlockSpec((N_EXPERTS, LANES), lambda i: (0, 0))),
        out_shape=(jax.ShapeDtypeStruct((TOP_K, n), I32),
                   jax.ShapeDtypeStruct((N_EXPERTS, LANES), F32)),
        scratch_shapes=[pltpu.VMEM((N_EXPERTS, LANES), F32)],
        compiler_params=_cparams(("arbitrary",)),
        name="expert_ranks",
    )(ids)


def _dest_kernel(ps_ref, ids_ref, rank_ref, dest_ref):
    ids = ids_ref[...]
    acc = rank_ref[...]
    for e in range(N_EXPERTS):
        acc = acc + jnp.where(ids == e, ps_ref[e], 0)
    dest_ref[...] = acc


def _destinations(pad_start, ids, rank):
    n = ids.shape[1]
    tn = 2048 if n % 2048 == 0 else 512
    return pl.pallas_call(
        _dest_kernel,
        grid_spec=pltpu.PrefetchScalarGridSpec(
            num_scalar_prefetch=1,
            grid=(n // tn,),
            in_specs=[pl.BlockSpec((TOP_K, tn), lambda i, ps: (0, i)),
                      pl.BlockSpec((TOP_K, tn), lambda i, ps: (0, i))],
            out_specs=pl.BlockSpec((TOP_K, tn), lambda i, ps: (0, i))),
        out_shape=jax.ShapeDtypeStruct((TOP_K, n), I32),
        compiler_params=_cparams(("arbitrary",)),
        name="destinations",
    )(pad_start, ids, rank)


_PAD_BITS = tuple(range(SUBLANES - 1)) + tuple(
    1 << j for j in range(SUBLANES.bit_length() - 1, MOE_ROWS.bit_length() - 1))


def _dispatch_kernel(pp_ref, np_ref, nu_ref, dest_ref, h_ref, xs_hbm, zeros_ref, sem, zsem):
    td = dest_ref.shape[1]
    n_blocks = xs_hbm.shape[0] // MOE_ROWS

    def pad_copy(e, bit):
        pos, npad = pp_ref[e], np_ref[e]
        head = (-pos) & (SUBLANES - 1)
        if bit < SUBLANES:
            return bit < head, pltpu.make_async_copy(zeros_ref.at[pl.ds(0, 1)],
                                                     xs_hbm.at[pl.ds(pos + bit, 1)], zsem)
        body = npad - head
        off = pl.multiple_of(pos + head + (body & (-2 * bit)), SUBLANES)
        return (body & bit) != 0, pltpu.make_async_copy(zeros_ref.at[pl.ds(0, bit)],
                                                        xs_hbm.at[pl.ds(off, bit)], zsem)

    def blk_copy(blk):
        off = pl.multiple_of(blk * MOE_ROWS, MOE_ROWS)
        return pltpu.make_async_copy(zeros_ref, xs_hbm.at[pl.ds(off, MOE_ROWS)], zsem)

    def pad_all(wait):
        def per_expert(e, carry):
            for bit in _PAD_BITS:
                cond, cp = pad_copy(e, bit)

                @pl.when(cond)
                def _():
                    cp.wait() if wait else cp.start()
            return carry

        def per_block(blk, carry):
            cp = blk_copy(blk)
            cp.wait() if wait else cp.start()
            return carry

        lax.fori_loop(0, N_EXPERTS, per_expert, 0)
        lax.fori_loop(nu_ref[0], n_blocks, per_block, 0)

    @pl.when(pl.program_id(0) == 0)
    def _():
        zeros_ref[...] = jnp.zeros_like(zeros_ref)
        pad_all(False)
        pad_all(True)

    def body(r, carry):
        for k in range(TOP_K):
            d = dest_ref[k, r]
            pltpu.make_async_copy(h_ref.at[pl.ds(r, 1)], xs_hbm.at[pl.ds(d, 1)],
                                  sem).start(priority=k % 2)
        return carry

    lax.fori_loop(0, td, body, 0, unroll=8)
    for k in range(TOP_K):
        pltpu.make_async_copy(h_ref, xs_hbm.at[pl.ds(0, td)], sem).wait()


def _dispatch(pad_pos, n_pad, n_used, dest, h2, n_rows):
    n, d = h2.shape
    return pl.pallas_call(
        _dispatch_kernel,
        grid_spec=pltpu.PrefetchScalarGridSpec(
            num_scalar_prefetch=3,
            grid=(n // T_DISPATCH,),
            in_specs=[pl.BlockSpec((TOP_K, T_DISPATCH), lambda i, *_: (0, i), memory_space=pltpu.SMEM),
                      pl.BlockSpec((T_DISPATCH, d), lambda i, *_: (i, 0))],
            out_specs=pl.BlockSpec(memory_space=pl.ANY),
            scratch_shapes=[pltpu.VMEM((MOE_ROWS, d), h2.dtype),
                            pltpu.SemaphoreType.DMA(()), pltpu.SemaphoreType.DMA(())]),
        out_shape=jax.ShapeDtypeStruct((n_rows, d), h2.dtype),
        compiler_params=pltpu.CompilerParams(dimension_semantics=("arbitrary",),
                                             has_side_effects=True, vmem_limit_bytes=VMEM_LIMIT),
        name="dispatch",
    )(pad_pos, n_pad, n_used, dest, h2)


def _expert_kernel(be_ref, nu_ref, xs_ref, wgu_ref, bgu_ref, wd_ref, bd_ref, ys_ref, wgu_bf, wd_bf):
    i = pl.program_id(0)
    nu = nu_ref[0]
    last = jnp.minimum(i, nu - 1)
    new_expert = (i == 0) | (be_ref[last] != be_ref[jnp.maximum(last - 1, 0)])

    @pl.when((i < nu) & new_expert)
    def _():
        wgu_bf[...] = wgu_ref[...].astype(BF16)
        wd_bf[...] = wd_ref[...].astype(BF16)

    @pl.when(i < nu)
    def _():
        x = xs_ref[...].astype(BF16)
        gu = jnp.dot(x, wgu_bf[...], preferred_element_type=F32) + bgu_ref[...]
        gate = jnp.minimum(gu[:, :D_FF], SWIGLU_LIMIT)
        lin = jnp.clip(gu[:, D_FF:], -SWIGLU_LIMIT, SWIGLU_LIMIT)
        act = gate * jax.nn.sigmoid(SWIGLU_ALPHA * gate) * (lin + 1.0)
        ys_ref[...] = jnp.dot(act.astype(BF16), wd_bf[...], preferred_element_type=F32) + bd_ref[...]

    @pl.when(i >= nu_ref[0])
    def _():
        ys_ref[...] = jnp.zeros_like(ys_ref)


def _expert_mlp(blk_e, n_used, xs, w_gu, b_gu, w_down, b_down, layer):
    n_rows, d = xs.shape
    n_blocks = n_rows // MOE_ROWS

    def blk(i, nu):
        return jnp.minimum(i, nu[0] - 1)

    return pl.pallas_call(
        _expert_kernel,
        grid_spec=pltpu.PrefetchScalarGridSpec(
            num_scalar_prefetch=2,
            grid=(n_blocks,),
            in_specs=[
                pl.BlockSpec((MOE_ROWS, d), lambda i, be, nu: (blk(i, nu), 0)),
                pl.BlockSpec((None, None, d, 2 * D_FF), lambda i, be, nu: (layer, be[blk(i, nu)], 0, 0)),
                pl.BlockSpec((None, None, 1, 2 * D_FF), lambda i, be, nu: (layer, be[blk(i, nu)], 0, 0)),
                pl.BlockSpec((None, None, D_FF, d), lambda i, be, nu: (layer, be[blk(i, nu)], 0, 0)),
                pl.BlockSpec((None, None, 1, d), lambda i, be, nu: (layer, be[blk(i, nu)], 0, 0)),
            ],
            out_specs=pl.BlockSpec((MOE_ROWS, d), lambda i, be, nu: (i, 0)),
            scratch_shapes=[pltpu.VMEM((d, 2 * D_FF), BF16), pltpu.VMEM((D_FF, d), BF16)]),
        out_shape=jax.ShapeDtypeStruct((n_rows, d), F32),
        compiler_params=_cparams(("arbitrary",)),
        name="expert_mlp",
    )(blk_e, n_used, xs, w_gu, b_gu.reshape(DEPTH, N_EXPERTS, 1, 2 * D_FF), w_down,
      b_down.reshape(DEPTH, N_EXPERTS, 1, d))


def _combine_kernel(dest_ref, dnext_ref, gates_ref, ys_hbm, x1_ref, g2_ref, lng_ref, lnb_ref, out_ref,
                    buf, sem):
    tc = dest_ref.shape[1]
    i = pl.program_id(0)
    n = pl.num_programs(0)
    slot = i % 2

    def gather(d_ref, slot):
        def body(r, carry):
            for k in range(TOP_K):
                pltpu.make_async_copy(ys_hbm.at[pl.ds(d_ref[k, r], 1)], buf.at[slot, k, pl.ds(r, 1)],
                                      sem.at[slot]).start(priority=k % 2)
            return carry

        lax.fori_loop(0, tc, body, 0, unroll=8)

    @pl.when(i == 0)
    def _():
        gather(dest_ref, 0)

    @pl.when(i + 1 < n)
    def _():
        gather(dnext_ref, 1 - slot)

    for k in range(TOP_K):
        pltpu.make_async_copy(ys_hbm.at[pl.ds(0, tc)], buf.at[slot, k], sem.at[slot]).wait()
    gpad = jnp.concatenate([gates_ref[...], jnp.zeros((LANES - TOP_K, tc), F32)], 0)
    gt = gpad.T
    f = buf[slot, 0] * gt[:, 0:1]
    for k in range(1, TOP_K):
        f = f + buf[slot, k] * gt[:, k:k + 1]
    out_ref[...] = _layer_norm(DN_ALPHA * x1_ref[...] + g2_ref[...] * f, lng_ref[...], lnb_ref[...])


def _combine(dest, gates, ys, x1, g2, row_of_block, ln_g, ln_b):
    n, d = x1.shape
    tc = T_COMBINE
    nsteps = n // tc
    return pl.pallas_call(
        _combine_kernel,
        grid=(nsteps,),
        in_specs=[pl.BlockSpec((TOP_K, tc), lambda i: (0, i), memory_space=pltpu.SMEM),
                  pl.BlockSpec((TOP_K, tc), lambda i: (0, jnp.minimum(i + 1, nsteps - 1)),
                               memory_space=pltpu.SMEM),
                  pl.BlockSpec((TOP_K, tc), lambda i: (0, i)),
                  pl.BlockSpec(memory_space=pl.ANY),
                  pl.BlockSpec((tc, d), lambda i: (i, 0)),
                  pl.BlockSpec((None, 1, d), lambda i: (row_of_block(i), 0, 0)),
                  pl.BlockSpec(ln_g.shape, lambda i: (0, 0)),
                  pl.BlockSpec(ln_b.shape, lambda i: (0, 0))],
        out_specs=pl.BlockSpec((tc, d), lambda i: (i, 0)),
        out_shape=jax.ShapeDtypeStruct((n, d), F32),
        scratch_shapes=[pltpu.VMEM((2, TOP_K, tc, d), F32), pltpu.SemaphoreType.DMA((2,))],
        compiler_params=_cparams(("arbitrary",)),
        name="combine",
    )(dest, dest, gates, ys, x1, g2, ln_g, ln_b)


def _moe(h2, ids, gates, x1, g2, row_of_block, ln_g, ln_b, w_gu, b_gu, w_down, b_down, layer):
    n = h2.shape[0]
    nk = n * TOP_K
    n_blocks = -(-nk // MOE_ROWS) + N_EXPERTS
    rank, cnt = _expert_ranks(ids)
    counts = cnt[:, 0].astype(I32)
    padded = (counts + MOE_ROWS - 1) // MOE_ROWS * MOE_ROWS
    pad_end = jnp.cumsum(padded)
    pad_start = (pad_end - padded).astype(I32)
    blk_start = jnp.arange(n_blocks, dtype=I32) * MOE_ROWS
    blk_e = jnp.minimum(jnp.sum((pad_end[None, :] <= blk_start[:, None]).astype(I32), axis=1),
                        N_EXPERTS - 1).astype(I32)
    n_used = (pad_end[-1:] // MOE_ROWS).astype(I32)
    dest = _destinations(pad_start, ids, rank)
    xs = _dispatch(pad_start + counts, padded - counts, n_used, dest, h2, n_blocks * MOE_ROWS)
    ys = _expert_mlp(blk_e, n_used, xs, w_gu, b_gu, w_down, b_down, layer)
    return _combine(dest, gates, ys, x1, g2, row_of_block, ln_g, ln_b)


def _odd_inproj_kernel(x_ref, sc_ref, sh_ref, wk_ref, wt_ref, ck_ref, sk_ref, cq_ref, sq_ref,
                       qT_ref, k_ref, vT_ref):
    h = (x_ref[...] * (1.0 + sc_ref[...]) + sh_ref[...]).astype(BF16)
    kk = jnp.dot(h, wk_ref[...], preferred_element_type=F32)
    k_ref[...] = (kk * ck_ref[...] + _swap16_lanes(kk) * sk_ref[...]).astype(BF16)
    yt = lax.dot_general(wt_ref[...], h, (((1,), (1,)), ((), ())),
                         preferred_element_type=F32)
    qw = C_HEADS * C_HEAD_DIM
    qt = yt[:qw]
    n_pairs = C_HEADS // C_KV_HEADS
    cq = jnp.concatenate([cq_ref[...]] * n_pairs, 0)
    sq = jnp.concatenate([sq_ref[...]] * n_pairs, 0)
    qT_ref[...] = (qt * cq + _swap16_rows(qt) * sq).astype(BF16)
    vT_ref[...] = yt[qw:].astype(BF16)


def _odd_inproj(x, x_off, b, s, sc, sh, mod_row, wk, wt, ck, sk, cq, sq, tm):
    d = x.shape[1]
    qw = C_HEADS * C_HEAD_DIM
    nb = s // tm
    xo = x_off // tm
    return pl.pallas_call(
        _odd_inproj_kernel,
        grid=(b, nb),
        in_specs=[pl.BlockSpec((tm, d), lambda bi, i: (xo + bi * nb + i, 0)),
                  pl.BlockSpec((None, 1, d), lambda bi, i: (mod_row(bi), 0, 0)),
                  pl.BlockSpec((None, 1, d), lambda bi, i: (mod_row(bi), 0, 0)),
                  pl.BlockSpec(wk.shape, lambda bi, i: (0, 0)),
                  pl.BlockSpec(wt.shape, lambda bi, i: (0, 0)),
                  pl.BlockSpec((tm, LANES), lambda bi, i: (i, 0)),
                  pl.BlockSpec((tm, LANES), lambda bi, i: (i, 0)),
                  pl.BlockSpec((LANES, tm), lambda bi, i: (0, i)),
                  pl.BlockSpec((LANES, tm), lambda bi, i: (0, i))],
        out_specs=(pl.BlockSpec((None, qw, tm), lambda bi, i: (bi, 0, i)),
                   pl.BlockSpec((None, tm, LANES), lambda bi, i: (bi, i, 0)),
                   pl.BlockSpec((None, LANES, tm), lambda bi, i: (bi, 0, i))),
        out_shape=(jax.ShapeDtypeStruct((b, qw, s), BF16),
                   jax.ShapeDtypeStruct((b, s, LANES), BF16),
                   jax.ShapeDtypeStruct((b, LANES, s), BF16)),
        compiler_params=_cparams(("parallel", "arbitrary")),
        name="odd_inproj",
    )(x, sc, sh, wk, wt, ck, sk, cq, sq)


def _win_attn_kernel(qT_ref, kp_ref, k0_ref, kn_ref, kc_ref, vp_ref, v0_ref, vn_ref, vc_ref, sink_ref,
                     o_ref, p_ref):
    tq = qT_ref.shape[1]
    i = pl.program_id(1)
    nb = pl.num_programs(1)
    n_pairs = C_HEADS // C_KV_HEADS
    qT = qT_ref[...]
    row = lax.broadcasted_iota(I32, (LANES, tq), 0)
    lo = row < C_HEAD_DIM
    zero = jnp.zeros((LANES, tq), BF16)
    cols = []
    for j in range(n_pairs):
        t = qT[j * LANES:(j + 1) * LANES]
        cols.append(jnp.where(lo, t, zero))
        cols.append(jnp.where(lo, zero, t))
    qs = jnp.concatenate(cols, 1)
    kb = jnp.concatenate([kp_ref[...], k0_ref[...], kn_ref[...], kc_ref[...]], 0)
    vbT = jnp.concatenate([vp_ref[...], v0_ref[...], vn_ref[...], vc_ref[...]], 1)
    nk = kb.shape[0]
    s = jnp.dot(kb, qs, preferred_element_type=F32)
    key = lax.broadcasted_iota(I32, (nk, tq), 0)
    qpos = lax.broadcasted_iota(I32, (nk, tq), 1)
    ok = (jnp.abs(key - tq - qpos) <= C_WINDOW)
    ok = ok & ((key >= tq) | (i > 0)) & ((key < 2 * tq) | (i < nb - 1))
    ok = ok | (key >= 3 * tq)
    sink = sink_ref[...]
    dens = []
    for hb in range(C_HEADS):
        sl = slice(hb * tq, (hb + 1) * tq)
        sb = jnp.where(ok, s[:, sl], NEG_INF)
        m = jnp.maximum(jnp.max(sb, axis=0, keepdims=True), sink[:, sl])
        p = jnp.exp2(sb - m)
        dens.append(jnp.sum(p, axis=0, keepdims=True) + jnp.exp2(sink[:, sl] - m))
        p_ref[:, sl] = p.astype(BF16)
    oT = jnp.dot(vbT, p_ref[...], preferred_element_type=F32)
    outs = []
    for j in range(n_pairs):
        a = oT[:, (2 * j) * tq:(2 * j + 1) * tq] / dens[2 * j]
        b = oT[:, (2 * j + 1) * tq:(2 * j + 2) * tq] / dens[2 * j + 1]
        outs.append(jnp.where(lo, a, b).T)
    o_ref[...] = jnp.concatenate(outs, 1).astype(BF16)


def _win_attention(qT, k, vT, kc, vcT, sink):
    b, qw, s = qT.shape
    tq = TQ_WIN
    nb = s // tq
    lc = kc.shape[1]
    nk = 3 * tq + lc

    def kblk(off):
        return pl.BlockSpec((None, tq, LANES), lambda bi, i: (bi, jnp.clip(i + off, 0, nb - 1), 0))

    def vblk(off):
        return pl.BlockSpec((None, LANES, tq), lambda bi, i: (bi, 0, jnp.clip(i + off, 0, nb - 1)))

    return pl.pallas_call(
        _win_attn_kernel,
        grid=(b, nb),
        in_specs=[pl.BlockSpec((None, qw, tq), lambda bi, i: (bi, 0, i)),
                  kblk(-1), kblk(0), kblk(1), pl.BlockSpec((None, lc, LANES), lambda bi, i: (bi, 0, 0)),
                  vblk(-1), vblk(0), vblk(1), pl.BlockSpec((None, LANES, lc), lambda bi, i: (bi, 0, 0)),
                  pl.BlockSpec(sink.shape, lambda bi, i: (0, 0))],
        out_specs=pl.BlockSpec((None, tq, qw), lambda bi, i: (bi, i, 0)),
        out_shape=jax.ShapeDtypeStruct((b, s, qw), BF16),
        scratch_shapes=[pltpu.VMEM((nk, C_HEADS * tq), BF16)],
        compiler_params=_cparams(("parallel", "arbitrary")),
        name="window_attention",
    )(qT, k, k, k, kc, vT, vT, vT, vcT, sink)


def kernel(x, c, ctx, c_ctx, ada_w, ada_b, ln_g, ln_b, ab_w_in, ab_w_out, diff_lambda, diff_subln_g,
           conv_w, c_w_in, c_w_out, c_sink, router_w, router_b, w_gu, b_gu, w_down, b_down):
    b, s, d = x.shape
    lc = ctx.shape[1]
    n_c = b * lc
    ctx_row = b

    cvec = jnp.zeros((8, d), F32).at[:b].set(c).at[b].set(c_ctx)
    mod = _modulation(cvec, ada_w, ada_b)

    def mod_vec(layer, j):
        return mod[layer, :, j * d:(j + 1) * d].reshape(8, 1, d)

    lat_row = lambda bi: bi
    ctx_rowf = lambda bi: ctx_row

    cos64, sin64 = _rope_tables(s)
    cos_l = jnp.concatenate([cos64, cos64], -1)
    sin_l = jnp.concatenate([sin64, sin64], -1)
    one_c = jnp.ones((lc, LANES), F32)
    zero_c = jnp.zeros((lc, LANES), F32)
    qscale = A_QK_DIM ** -0.5 * math.log2(math.e)

    l = 0
    w_in = ab_w_in[0]
    wn = w_in[:, A_WIDTH:].astype(BF16)
    wn = jnp.concatenate([wn[:, :A_WIDTH], wn[:, 2 * A_WIDTH:]], 1)
    wt = jnp.concatenate([w_in[:, :A_WIDTH], w_in[:, 2 * A_WIDTH:3 * A_WIDTH]], 1).T.astype(BF16)
    sc1, sh1, g1 = mod_vec(l, 1), mod_vec(l, 0), mod_vec(l, 2)
    sh2, sc2, g2 = mod_vec(l, 3), mod_vec(l, 4), mod_vec(l, 5)
    lam_init = 0.8 - 0.6 * math.exp(-0.3 * l)

    x_tok = x.reshape(b * s, d)
    ctx_tok = ctx.reshape(n_c, d)
    qT, k, vT, bg, u = _even_inproj(x_tok, 0, b, s, sc1, sh1, lat_row, wn, wt, cos_l, sin_l,
                                    (cos_l * qscale).T, (sin_l * qscale).T, TM_PROJ)
    qTc, kc, vTc, bgc, uc = _even_inproj(ctx_tok, 0, b, lc, sc1, sh1, ctx_rowf, wn, wt, one_c, zero_c,
                                         (one_c * qscale).T, zero_c.T, lc)
    g_col = diff_subln_g[0].reshape(A_V_DIM, 1)
    a_lat = _diff_attention(qT, k, vT, kc, vTc, diff_lambda[0], g_col, lam_init, TQ_DIFF)
    a_ctx = _diff_attention(qTc, None, None, kc, vTc, diff_lambda[0], g_col, lam_init, lc)

    w_out = ab_w_out[0].astype(BF16)
    wa, wb = w_out[:A_WIDTH], w_out[A_WIDTH:]
    lng0, lnb0 = ln_g[l, 0].reshape(1, d), ln_b[l, 0].reshape(1, d)
    lng1, lnb1 = ln_g[l, 1].reshape(1, d), ln_b[l, 1].reshape(1, d)
    rwT = router_w[l].T.astype(BF16)
    rb = router_b[l].reshape(N_EXPERTS, 1)
    n0 = n_c + b * s
    empty = (jnp.zeros((n0, d), F32), jnp.zeros((n0, d), F32),
             jnp.zeros((TOP_K, n0), I32), jnp.zeros((TOP_K, n0), F32))
    part = _outproj(a_ctx, (bgc, uc, conv_w[0]), wa, wb, ctx_tok, 0, (g1, sc2, sh2), ctx_rowf,
                    lng0, lnb0, rwT, rb, lc, 0, empty)
    x1a, h2a, idsa, gatesa = _outproj(a_lat, (bg, u, conv_w[0]), wa, wb, x_tok, 0, (g1, sc2, sh2), lat_row,
                                      lng0, lnb0, rwT, rb, TM_PROJ, n_c, part)
    ncb = n_c // T_COMBINE
    spb = s // T_COMBINE
    row_of_block0 = lambda i: jnp.where(i < ncb, ctx_row, jnp.maximum(i - ncb, 0) // spb)
    y_all = _moe(h2a, idsa, gatesa, x1a, g2, row_of_block0, lng1, lnb1, w_gu, b_gu, w_down, b_down, l)

    l = 1
    sc1, sh1, g1 = mod_vec(l, 1), mod_vec(l, 0), mod_vec(l, 2)
    sh2, sc2, g2 = mod_vec(l, 3), mod_vec(l, 4), mod_vec(l, 5)
    g = C_HEADS // C_KV_HEADS
    perm = jnp.array([(kv * g + j) * C_HEAD_DIM + dd for j in range(g) for kv in range(C_KV_HEADS)
                      for dd in range(C_HEAD_DIM)], I32)
    w_in = c_w_in[0]
    qw = C_HEADS * C_HEAD_DIM
    kvw = C_KV_HEADS * C_HEAD_DIM
    wk_odd = w_in[:, qw:qw + kvw].astype(BF16)
    wt_odd = jnp.concatenate([w_in[:, :qw][:, perm], w_in[:, qw + kvw:]], 1).T.astype(BF16)
    wscale = C_HEAD_DIM ** -0.5 * math.log2(math.e)
    qT, k, vT = _odd_inproj(y_all, n_c, b, s, sc1, sh1, lat_row, wk_odd, wt_odd, cos_l, sin_l,
                            (cos_l * wscale).T, (sin_l * wscale).T, TM_PROJ)
    _, kc, vcT = _odd_inproj(y_all, 0, b, lc, sc1, sh1, ctx_rowf, wk_odd, wt_odd, one_c, zero_c,
                             (one_c * wscale).T, zero_c.T, lc)
    sink = c_sink[0][perm[::C_HEAD_DIM] // C_HEAD_DIM] * math.log2(math.e)
    sink = jnp.repeat(sink, TQ_WIN).reshape(1, C_HEADS * TQ_WIN)
    o = _win_attention(qT, k, vT, kc, vcT, sink)
    wa = c_w_out[0][perm].astype(BF16)
    lng0, lnb0 = ln_g[l, 0].reshape(1, d), ln_b[l, 0].reshape(1, d)
    lng1, lnb1 = ln_g[l, 1].reshape(1, d), ln_b[l, 1].reshape(1, d)
    rwT = router_w[l].T.astype(BF16)
    rb = router_b[l].reshape(N_EXPERTS, 1)
    x1a, h2a, idsa, gatesa = _outproj(o, None, wa, None, y_all, n_c, (g1, sc2, sh2), lat_row,
                                      lng0, lnb0, rwT, rb, TM_PROJ, 0, ())
    row_of_block1 = lambda i: i // spb
    y = _moe(h2a, idsa, gatesa, x1a, g2, row_of_block1, lng1, lnb1, w_gu, b_gu, w_down, b_down, l)
    return y.reshape(b, s, d)
```

```python
import functools
import math

import jax
import jax.numpy as jnp
from jax import lax
from jax.experimental import pallas as pl
from jax.experimental.pallas import tpu as pltpu

F32 = jnp.float32
BF16 = jnp.bfloat16
I32 = jnp.int32

D_MODEL = 1024
DEPTH = 2
GRID_W = 64
ROPE_DIM = 64
ROPE_BASE = 10000.0
A_HEADS = 4
A_QK_DIM = 64
A_V_DIM = 128
A_WIDTH = 512
B_WIDTH = 512
C_HEADS = 16
C_KV_HEADS = 2
C_HEAD_DIM = 64
C_WINDOW = 128
N_EXPERTS = 32
TOP_K = 4
D_FF = 1024
SWIGLU_ALPHA = 1.702
SWIGLU_LIMIT = 7.0
LN_EPS = 1e-5
RMS_EPS = 1e-5
NEG_INF = -1e30
DN_ALPHA = (2 * DEPTH) ** 0.25

LANES = 128
SUBLANES = 8
VMEM_LIMIT = 56 * 1024 * 1024

TM_PROJ = 512
TQ_DIFF = 512
TQ_WIN = 128
MOE_ROWS = 512
T_RANK = 512
T_DISPATCH = 512
T_COMBINE = 256


def _cparams(sem):
    return pltpu.CompilerParams(dimension_semantics=sem, vmem_limit_bytes=VMEM_LIMIT)


ROW_TILES = D_MODEL // LANES
assert ROW_TILES == SUBLANES


def _load_row_tiles(ref, rows):
    return jnp.concatenate([ref[pl.ds(j, rows, stride=ROW_TILES), :] for j in range(ROW_TILES)], axis=1)


def _store_row_tiles(ref, val):
    rows = val.shape[0]
    for j in range(ROW_TILES):
        ref[pl.ds(j, rows, stride=ROW_TILES), :] = val[:, j * LANES:(j + 1) * LANES]


def _mod_kernel(c_ref, w_ref, b_ref, o_ref):
    c = c_ref[...]
    s = c * jax.nn.sigmoid(c)
    o_ref[...] = jnp.dot(s.astype(BF16), w_ref[...].astype(BF16),
                         preferred_element_type=F32) + b_ref[...]


def _modulation(cvec, ada_w, ada_b):
    d = D_MODEL
    tn = 1536
    return pl.pallas_call(
        _mod_kernel,
        grid=(DEPTH, 6 * d // tn),
        in_specs=[pl.BlockSpec((8, d), lambda l, j: (0, 0)),
                  pl.BlockSpec((None, d, tn), lambda l, j: (l, 0, j)),
                  pl.BlockSpec((None, 1, tn), lambda l, j: (l, 0, j))],
        out_specs=pl.BlockSpec((None, 8, tn), lambda l, j: (l, 0, j)),
        out_shape=jax.ShapeDtypeStruct((DEPTH, 8, 6 * d), F32),
        compiler_params=_cparams(("arbitrary", "arbitrary")),
        name="modulation",
    )(cvec, ada_w, ada_b.reshape(DEPTH, 1, 6 * d))


def _swap16_lanes(t):
    lane = lax.broadcasted_iota(I32, t.shape, 1)
    first = (lane % 32) < 16
    return jnp.where(first, pltpu.roll(t, LANES - 16, 1), pltpu.roll(t, 16, 1))


def _swap16_rows(a):
    pieces = []
    for i in range(0, a.shape[0], 32):
        pieces.append(a[i + 16:i + 32])
        pieces.append(a[i:i + 16])
    return jnp.concatenate(pieces, 0)


def _rope_tables(s):
    rows = jnp.repeat(jnp.arange(s // GRID_W, dtype=F32), GRID_W)
    cols = jnp.tile(jnp.arange(GRID_W, dtype=F32), s // GRID_W)
    axis_dim = ROPE_DIM // 2
    inv = ROPE_BASE ** (-jnp.arange(0, axis_dim, 2, dtype=F32) / axis_dim)
    ar, ac = rows[:, None] * inv, cols[:, None] * inv
    cr, sr, cc, sc = jnp.cos(ar), jnp.sin(ar), jnp.cos(ac), jnp.sin(ac)
    cos64 = jnp.concatenate([cr, cr, cc, cc], -1)
    sin64 = jnp.concatenate([-sr, sr, -sc, sc], -1)
    return cos64, sin64


def _even_inproj_kernel(x_ref, sc_ref, sh_ref, wn_ref, wt_ref, ck_ref, sk_ref, cq_ref, sq_ref,
                        qT_ref, k_ref, vT_ref, bg_ref, u_ref):
    tm = x_ref.shape[0]
    h = (x_ref[...] * (1.0 + sc_ref[...]) + sh_ref[...]).astype(BF16)
    yn = jnp.dot(h, wn_ref[...], preferred_element_type=F32)
    ck = ck_ref[...]
    sk = sk_ref[...]
    pieces = []
    for j in range(A_HEADS):
        t = yn[:, j * LANES:(j + 1) * LANES]
        pieces.append(t * ck + _swap16_lanes(t) * sk)
    k_ref[...] = jnp.concatenate(pieces, 1).astype(BF16)
    bg_ref[...] = yn[:, A_WIDTH:A_WIDTH + B_WIDTH]
    u_ref[...] = yn[:, A_WIDTH + B_WIDTH:A_WIDTH + 2 * B_WIDTH] * yn[:, A_WIDTH + 2 * B_WIDTH:]
    yt = lax.dot_general(wt_ref[...], h, (((1,), (1,)), ((), ())),
                         preferred_element_type=F32)
    qt = yt[:A_WIDTH]
    cq = jnp.concatenate([cq_ref[...]] * A_HEADS, 0)
    sq = jnp.concatenate([sq_ref[...]] * A_HEADS, 0)
    qT_ref[...] = (qt * cq + _swap16_rows(qt) * sq).astype(BF16).reshape(A_HEADS, A_V_DIM, tm)
    vT_ref[...] = yt[A_WIDTH:].astype(BF16).reshape(A_HEADS, A_V_DIM, tm)


def _even_inproj(x, x_off, b, s, sc, sh, mod_row, wn, wt, ck, sk, cq, sq, tm):
    d = x.shape[1]
    nb = s // tm
    xo = x_off // tm
    out_shape = (
        jax.ShapeDtypeStruct((b, A_HEADS, A_V_DIM, s), BF16),
        jax.ShapeDtypeStruct((b, s, A_WIDTH), BF16),
        jax.ShapeDtypeStruct((b, A_HEADS, nb, A_V_DIM, tm), BF16),
        jax.ShapeDtypeStruct((b, s, B_WIDTH), F32),
        jax.ShapeDtypeStruct((b, s, B_WIDTH), F32),
    )
    return pl.pallas_call(
        _even_inproj_kernel,
        grid=(b, nb),
        in_specs=[
            pl.BlockSpec((tm, d), lambda bi, i: (xo + bi * nb + i, 0)),
            pl.BlockSpec((None, 1, d), lambda bi, i: (mod_row(bi), 0, 0)),
            pl.BlockSpec((None, 1, d), lambda bi, i: (mod_row(bi), 0, 0)),
            pl.BlockSpec(wn.shape, lambda bi, i: (0, 0)),
            pl.BlockSpec(wt.shape, lambda bi, i: (0, 0)),
            pl.BlockSpec((tm, LANES), lambda bi, i: (i, 0)),
            pl.BlockSpec((tm, LANES), lambda bi, i: (i, 0)),
            pl.BlockSpec((LANES, tm), lambda bi, i: (0, i)),
            pl.BlockSpec((LANES, tm), lambda bi, i: (0, i)),
        ],
        out_specs=(
            pl.BlockSpec((None, A_HEADS, A_V_DIM, tm), lambda bi, i: (bi, 0, 0, i)),
            pl.BlockSpec((None, tm, A_WIDTH), lambda bi, i: (bi, i, 0)),
            pl.BlockSpec((None, A_HEADS, None, A_V_DIM, tm), lambda bi, i: (bi, 0, i, 0, 0)),
            pl.BlockSpec((None, tm, B_WIDTH), lambda bi, i: (bi, i, 0)),
            pl.BlockSpec((None, tm, B_WIDTH), lambda bi, i: (bi, i, 0)),
        ),
        out_shape=out_shape,
        compiler_params=_cparams(("parallel", "arbitrary")),
        name="even_inproj",
    )(x, sc, sh, wn, wt, ck, sk, cq, sq)


def _diff_attn_kernel(*refs, n_chunks, lam_init, with_lat):
    if with_lat:
        (qT_ref, k_ref, vT_ref, kc_ref, vTc_ref, lamp_ref, g_ref, o_ref,
         acc1, acc2, sa1, sa2, sb1, sb2) = refs
    else:
        qT_ref, kc_ref, vTc_ref, lamp_ref, g_ref, o_ref, acc1, acc2 = refs
    tq = qT_ref.shape[1]
    qT = qT_ref[...]
    row = lax.broadcasted_iota(I32, qT.shape, 0)
    zero = jnp.zeros_like(qT)
    q1 = jnp.where(row < A_QK_DIM, qT, zero)
    q2 = jnp.where(row >= A_QK_DIM, qT, zero)
    acc1[...] = jnp.zeros_like(acc1)
    acc2[...] = jnp.zeros_like(acc2)

    def one_map(s, vTc, m, l, acc):
        m_new = jnp.maximum(m, jnp.max(s, axis=0, keepdims=True))
        alpha = jnp.exp2(m - m_new)
        p = jnp.exp2(s - m_new)
        l_new = alpha * l + jnp.sum(p, axis=0, keepdims=True)
        acc[...] = alpha * acc[...] + jnp.dot(vTc, p.astype(BF16), preferred_element_type=F32)
        return m_new, l_new

    def softmax_pv(s1, s2, vTc, carry):
        m1, l1, m2, l2 = carry
        m1, l1 = one_map(s1, vTc, m1, l1, acc1)
        m2, l2 = one_map(s2, vTc, m2, l2, acc2)
        return m1, l1, m2, l2

    neg = jnp.full((1, tq), -jnp.inf, F32)
    zer = jnp.zeros((1, tq), F32)
    kc = kc_ref[...]
    carry = softmax_pv(jnp.dot(kc, q1, preferred_element_type=F32),
                       jnp.dot(kc, q2, preferred_element_type=F32), vTc_ref[...], (neg, zer, neg, zer))
    if with_lat:
        tk = vT_ref.shape[2]

        def scores(c, s1_ref, s2_ref):
            kk = k_ref[pl.ds(pl.multiple_of(c * tk, tk), tk), :]
            s1_ref[...] = jnp.dot(kk, q1, preferred_element_type=F32)
            s2_ref[...] = jnp.dot(kk, q2, preferred_element_type=F32)

        scores(0, sa1, sa2)
        bufs = ((sa1, sa2), (sb1, sb2))

        def steps(c0, count, carry, last):
            for j in range(count):
                cur, nxt = bufs[j % 2], bufs[(j + 1) % 2]
                if not (last and j == count - 1):
                    scores(c0 + j + 1, *nxt)
                carry = softmax_pv(cur[0][...], cur[1][...], vT_ref[c0 + j], carry)
            return carry

        unroll = 4
        n_body = max(n_chunks // unroll - 1, 0)
        carry = lax.fori_loop(0, n_body, lambda j, cr: steps(j * unroll, unroll, cr, False), carry)
        carry = steps(n_body * unroll, n_chunks - n_body * unroll, carry, True)
    m1, l1, m2, l2 = carry

    lf = lamp_ref[...]
    e1 = jnp.exp(jnp.sum(lf[0:1] * lf[1:2], axis=1, keepdims=True))
    e2 = jnp.exp(jnp.sum(lf[2:3] * lf[3:4], axis=1, keepdims=True))
    lam = e1 - e2 + lam_init
    o = acc1[...] / l1 - lam * (acc2[...] / l2)
    o = o * lax.rsqrt(jnp.mean(o * o, axis=0, keepdims=True) + RMS_EPS)
    o = o * g_ref[...] * (1.0 - lam_init)
    o_ref[...] = o.T.astype(BF16)


def _diff_attention(qT, k, vT, kc, vTc, lam_p, g_col, lam_init, tq):
    b, _, _, s = qT.shape
    lc = kc.shape[1]
    with_lat = k is not None
    in_specs = [pl.BlockSpec((None, None, A_V_DIM, tq), lambda bi, h, i: (bi, h, 0, i))]
    args = [qT]
    n_chunks = 0
    if with_lat:
        sk = k.shape[1]
        n_chunks, tk = vT.shape[2], vT.shape[4]
        in_specs += [pl.BlockSpec((None, sk, LANES), lambda bi, h, i: (bi, 0, h)),
                     pl.BlockSpec((None, None, n_chunks, A_V_DIM, tk), lambda bi, h, i: (bi, h, 0, 0, 0))]
        args += [k, vT]
    in_specs += [pl.BlockSpec((None, lc, LANES), lambda bi, h, i: (bi, 0, h)),
                 pl.BlockSpec((None, None, None, A_V_DIM, lc), lambda bi, h, i: (bi, h, 0, 0, 0)),
                 pl.BlockSpec(lam_p.shape, lambda bi, h, i: (0, 0)),
                 pl.BlockSpec(g_col.shape, lambda bi, h, i: (0, 0))]
    args += [kc, vTc, lam_p, g_col]
    scratch = [pltpu.VMEM((A_V_DIM, tq), F32), pltpu.VMEM((A_V_DIM, tq), F32)]
    if with_lat:
        scratch += [pltpu.VMEM((tk, tq), F32)] * 4
    return pl.pallas_call(
        functools.partial(_diff_attn_kernel, n_chunks=n_chunks, lam_init=lam_init, with_lat=with_lat),
        grid=(b, A_HEADS, s // tq),
        in_specs=in_specs,
        out_specs=pl.BlockSpec((None, tq, A_V_DIM), lambda bi, h, i: (bi, i, h)),
        out_shape=jax.ShapeDtypeStruct((b, s, A_WIDTH), BF16),
        scratch_shapes=scratch,
        compiler_params=_cparams(("parallel", "parallel", "arbitrary")),
        name="diff_attention_lat" if with_lat else "diff_attention_ctx",
    )(*args)


def _layer_norm(z, g, b):
    mu = jnp.mean(z, axis=-1, keepdims=True)
    zc = z - mu
    var = jnp.mean(zc * zc, axis=-1, keepdims=True)
    return zc * lax.rsqrt(var + LN_EPS) * g + b


def _top4_router(h2, rwT_ref, rb_ref, ids_ref, gates_ref):
    tm = h2.shape[0]
    lt = lax.dot_general(rwT_ref[...], h2.astype(BF16), (((1,), (1,)), ((), ())),
                         preferred_element_type=F32) + rb_ref[...]
    eidx = lax.broadcasted_iota(I32, lt.shape, 0)
    vals, ids = [], []
    for _ in range(TOP_K):
        mx = jnp.max(lt, axis=0, keepdims=True)
        idx = jnp.min(jnp.where(lt == mx, eidx, N_EXPERTS), axis=0, keepdims=True)
        vals.append(mx)
        ids.append(idx)
        lt = jnp.where(eidx == idx, -jnp.inf, lt)
    es = [jnp.exp(v - vals[0]) for v in vals]
    den = es[0] + es[1] + es[2] + es[3]
    ids_ref[...] = jnp.concatenate(ids, 0)
    gates_ref[...] = jnp.concatenate([e / den for e in es], 0)
    del tm


def _outproj_kernel(*refs, has_conv):
    if has_conv:
        (a_ref, bg_ref, u_ref, up_ref, un_ref, cw_ref, wa_ref, wb_ref, x_ref, g1_ref, lng_ref, lnb_ref,
         sc2_ref, sh2_ref, rwT_ref, rb_ref, x1_ref, h2_ref, ids_ref, gates_ref) = refs
    else:
        (a_ref, wa_ref, x_ref, g1_ref, lng_ref, lnb_ref,
         sc2_ref, sh2_ref, rwT_ref, rb_ref, x1_ref, h2_ref, ids_ref, gates_ref) = refs
    y = jnp.dot(a_ref[...], wa_ref[...], preferred_element_type=F32)
    if has_conv:
        tm = u_ref.shape[0]
        i = pl.program_id(1)
        nb = pl.num_programs(1)
        u = u_ref[...]
        prev = jnp.where(i > 0, up_ref[SUBLANES - 1:SUBLANES, :], 0.0)
        nxt = jnp.where(i < nb - 1, un_ref[0:1, :], 0.0)
        r = lax.broadcasted_iota(I32, u.shape, 0)
        um1 = jnp.where(r == 0, prev, pltpu.roll(u, 1, 0))
        up1 = jnp.where(r == tm - 1, nxt, pltpu.roll(u, tm - 1, 0))
        cw = cw_ref[...]
        conv = um1 * cw[0:1] + u * cw[1:2] + up1 * cw[2:3]
        b_mix = (bg_ref[...] * conv).astype(BF16)
        y = y + jnp.dot(b_mix, wb_ref[...], preferred_element_type=F32)
    x1 = _layer_norm(DN_ALPHA * x_ref[...] + g1_ref[...] * y, lng_ref[...], lnb_ref[...])
    x1_ref[...] = x1
    h2 = x1 * (1.0 + sc2_ref[...]) + sh2_ref[...]
    _store_row_tiles(h2_ref, h2)
    _top4_router(h2, rwT_ref, rb_ref, ids_ref, gates_ref)


def _outproj(a, conv_in, wa, wb, x, x_off, mod_vecs, mod_row, ln_g, ln_b, rwT, rb, tm, tok_off, prev_out):
    b, s = a.shape[0], a.shape[1]
    d = x.shape[1]
    nb = s // tm
    g1, sc2, sh2 = mod_vecs
    has_conv = conv_in is not None
    n_tok = prev_out[0].shape[0] if prev_out else b * s
    ob = tok_off // tm
    xo = x_off // tm

    def row_spec(width):
        return pl.BlockSpec((None, tm, width), lambda bi, i: (bi, i, 0))

    def vec_spec():
        return pl.BlockSpec((None, 1, d), lambda bi, i: (mod_row(bi), 0, 0))

    def full(arr):
        return pl.BlockSpec(arr.shape, lambda bi, i: (0,) * arr.ndim)

    in_specs = [row_spec(a.shape[-1])]
    args = [a]
    if has_conv:
        bg, u, cw = conv_in
        s8 = s // SUBLANES
        t8 = tm // SUBLANES
        in_specs += [row_spec(B_WIDTH), row_spec(B_WIDTH),
                     pl.BlockSpec((None, SUBLANES, B_WIDTH),
                                  lambda bi, i: (bi, jnp.maximum(i * t8 - 1, 0), 0)),
                     pl.BlockSpec((None, SUBLANES, B_WIDTH),
                                  lambda bi, i: (bi, jnp.minimum((i + 1) * t8, s8 - 1), 0)),
                     full(cw), full(wa), full(wb)]
        args += [bg, u, u, u, cw, wa, wb]
    else:
        in_specs += [full(wa)]
        args += [wa]
    in_specs += [pl.BlockSpec((tm, d), lambda bi, i: (xo + bi * nb + i, 0)),
                 vec_spec(), full(ln_g), full(ln_b), vec_spec(), vec_spec(), full(rwT), full(rb)]
    args += [x, g1, ln_g, ln_b, sc2, sh2, rwT, rb]
    n_in = len(args)
    n_alias = len(prev_out)
    in_specs += [pl.BlockSpec(memory_space=pl.ANY)] * n_alias
    args += list(prev_out)
    out_specs = (
        pl.BlockSpec((tm, d), lambda bi, i: (ob + bi * nb + i, 0)),
        pl.BlockSpec((tm * ROW_TILES, LANES), lambda bi, i: (ob + bi * nb + i, 0)),
        pl.BlockSpec((TOP_K, tm), lambda bi, i: (0, ob + bi * nb + i)),
        pl.BlockSpec((TOP_K, tm), lambda bi, i: (0, ob + bi * nb + i)),
    )
    out_shape = (
        jax.ShapeDtypeStruct((n_tok, d), F32),
        jax.ShapeDtypeStruct((n_tok * ROW_TILES, LANES), F32),
        jax.ShapeDtypeStruct((TOP_K, n_tok), I32),
        jax.ShapeDtypeStruct((TOP_K, n_tok), F32),
    )

    def kern(*refs):
        refs = refs[:n_in] + refs[n_in + n_alias:]
        _outproj_kernel(*refs, has_conv=has_conv)

    return pl.pallas_call(
        kern,
        grid=(b, nb),
        in_specs=in_specs,
        out_specs=out_specs,
        out_shape=out_shape,
        input_output_aliases={n_in + j: j for j in range(n_alias)},
        compiler_params=_cparams(("parallel", "arbitrary")),
        name="outproj_conv" if has_conv else "outproj",
    )(*args)


def _rank_kernel(ids_ref, rank_ref, cnt_ref, run_ref):
    i = pl.program_id(0)
    tr = ids_ref.shape[1]

    @pl.when(i == 0)
    def _():
        run_ref[...] = jnp.zeros_like(run_ref)

    ids = ids_ref[...]
    eidx = lax.broadcasted_iota(I32, (N_EXPERTS, tr), 0)
    hits = [eidx == ids[k:k + 1] for k in range(TOP_K)]
    member = (hits[0] | hits[1] | hits[2] | hits[3]).astype(F32)
    r = lax.broadcasted_iota(I32, (tr, tr), 0)
    c = lax.broadcasted_iota(I32, (tr, tr), 1)
    upper = (r < c).astype(BF16)
    prefix = jnp.dot(member.astype(BF16), upper, preferred_element_type=F32)
    base = run_ref[:, 0:1] + prefix
    ranks = [jnp.sum(jnp.where(hits[k], base, 0.0), axis=0, keepdims=True) for k in range(TOP_K)]
    rank_ref[...] = jnp.concatenate(ranks, 0).astype(I32)
    run_ref[...] = run_ref[...] + jnp.sum(member, axis=1, keepdims=True)
    cnt_ref[...] = run_ref[...]


def _expert_ranks(ids):
    n = ids.shape[1]
    return pl.pallas_call(
        _rank_kernel,
        grid=(n // T_RANK,),
        in_specs=[pl.BlockSpec((TOP_K, T_RANK), lambda i: (0, i))],
        out_specs=(pl.BlockSpec((TOP_K, T_RANK), lambda i: (0, i)),
                   pl.BlockSpec((N_EXPERTS, LANES), lambda i: (0, 0))),
        out_shape=(jax.ShapeDtypeStruct((TOP_K, n), I32),
                   jax.ShapeDtypeStruct((N_EXPERTS, LANES), F32)),
        scratch_shapes=[pltpu.VMEM((N_EXPERTS, LANES), F32)],
        compiler_params=_cparams(("arbitrary",)),
        name="expert_ranks",
    )(ids)


def _dest_kernel(ps_ref, ids_ref, rank_ref, dest_ref):
    ids = ids_ref[...]
    acc = rank_ref[...]
    for e in range(N_EXPERTS):
        acc = acc + jnp.where(ids == e, ps_ref[e], 0)
    dest_ref[...] = acc


def _destinations(pad_start, ids, rank):
    n = ids.shape[1]
    tn = 2048 if n % 2048 == 0 else 512
    return pl.pallas_call(
        _dest_kernel,
        grid_spec=pltpu.PrefetchScalarGridSpec(
            num_scalar_prefetch=1,
            grid=(n // tn,),
            in_specs=[pl.BlockSpec((TOP_K, tn), lambda i, ps: (0, i)),
                      pl.BlockSpec((TOP_K, tn), lambda i, ps: (0, i))],
            out_specs=pl.BlockSpec((TOP_K, tn), lambda i, ps: (0, i))),
        out_shape=jax.ShapeDtypeStruct((TOP_K, n), I32),
        compiler_params=_cparams(("arbitrary",)),
        name="destinations",
    )(pad_start, ids, rank)


_PAD_BITS = tuple(1 << j for j in reversed(range(MOE_ROWS.bit_length() - 1)))


def _tile_rows(start, n):
    return pl.ds(pl.multiple_of(start * ROW_TILES, ROW_TILES), n * ROW_TILES)


def _dispatch_kernel(pp_ref, np_ref, nu_ref, dest_ref, h_ref, xs_hbm, zeros_ref, sem, zsem):
    td = dest_ref.shape[1]
    n_blocks = xs_hbm.shape[0] // (MOE_ROWS * ROW_TILES)

    def pad_copy(e, bit):
        npad = np_ref[e]
        off = pp_ref[e] + (npad & (-2 * bit))
        return (npad & bit) != 0, pltpu.make_async_copy(zeros_ref.at[_tile_rows(0, bit)],
                                                        xs_hbm.at[_tile_rows(off, bit)], zsem)

    def blk_copy(blk):
        return pltpu.make_async_copy(zeros_ref, xs_hbm.at[_tile_rows(blk * MOE_ROWS, MOE_ROWS)], zsem)

    def pad_all(wait):
        def per_expert(e, carry):
            for bit in _PAD_BITS:
                cond, cp = pad_copy(e, bit)

                @pl.when(cond)
                def _():
                    cp.wait() if wait else cp.start()
            return carry

        def per_block(blk, carry):
            cp = blk_copy(blk)
            cp.wait() if wait else cp.start()
            return carry

        lax.fori_loop(0, N_EXPERTS, per_expert, 0)
        lax.fori_loop(nu_ref[0], n_blocks, per_block, 0)

    @pl.when(pl.program_id(0) == 0)
    def _():
        zeros_ref[...] = jnp.zeros_like(zeros_ref)
        pad_all(False)
        pad_all(True)

    def body(r, carry):
        for k in range(TOP_K):
            pltpu.make_async_copy(h_ref.at[_tile_rows(r, 1)], xs_hbm.at[_tile_rows(dest_ref[k, r], 1)],
                                  sem).start(priority=k % 2)
        return carry

    lax.fori_loop(0, td, body, 0, unroll=8)
    for k in range(TOP_K):
        pltpu.make_async_copy(h_ref, xs_hbm.at[_tile_rows(0, td)], sem).wait()


def _dispatch(pad_pos, n_pad, n_used, dest, h2, n_rows):
    n = h2.shape[0] // ROW_TILES
    return pl.pallas_call(
        _dispatch_kernel,
        grid_spec=pltpu.PrefetchScalarGridSpec(
            num_scalar_prefetch=3,
            grid=(n // T_DISPATCH,),
            in_specs=[pl.BlockSpec((TOP_K, T_DISPATCH), lambda i, *_: (0, i), memory_space=pltpu.SMEM),
                      pl.BlockSpec((T_DISPATCH * ROW_TILES, LANES), lambda i, *_: (i, 0))],
            out_specs=pl.BlockSpec(memory_space=pl.ANY),
            scratch_shapes=[pltpu.VMEM((MOE_ROWS * ROW_TILES, LANES), h2.dtype),
                            pltpu.SemaphoreType.DMA(()), pltpu.SemaphoreType.DMA(())]),
        out_shape=jax.ShapeDtypeStruct((n_rows * ROW_TILES, LANES), h2.dtype),
        compiler_params=pltpu.CompilerParams(dimension_semantics=("arbitrary",),
                                             has_side_effects=True, vmem_limit_bytes=VMEM_LIMIT),
        name="dispatch",
    )(pad_pos, n_pad, n_used, dest, h2)


def _expert_kernel(be_ref, nu_ref, xs_ref, wgu_ref, bgu_ref, wd_ref, bd_ref, ys_ref, wgu_bf, wd_bf):
    i = pl.program_id(0)
    nu = nu_ref[0]
    last = jnp.minimum(i, nu - 1)
    new_expert = (i == 0) | (be_ref[last] != be_ref[jnp.maximum(last - 1, 0)])

    @pl.when((i < nu) & new_expert)
    def _():
        wgu_bf[...] = wgu_ref[...].astype(BF16)
        wd_bf[...] = wd_ref[...].astype(BF16)

    @pl.when(i < nu)
    def _():
        x = _load_row_tiles(xs_ref, MOE_ROWS).astype(BF16)
        gu = jnp.dot(x, wgu_bf[...], preferred_element_type=F32) + bgu_ref[...]
        gate = jnp.minimum(gu[:, :D_FF], SWIGLU_LIMIT)
        lin = jnp.clip(gu[:, D_FF:], -SWIGLU_LIMIT, SWIGLU_LIMIT)
        act = gate * jax.nn.sigmoid(SWIGLU_ALPHA * gate) * (lin + 1.0)
        _store_row_tiles(ys_ref, jnp.dot(act.astype(BF16), wd_bf[...], preferred_element_type=F32)
                         + bd_ref[...])

    @pl.when(i >= nu_ref[0])
    def _():
        ys_ref[...] = jnp.zeros_like(ys_ref)


def _expert_mlp(blk_e, n_used, xs, w_gu, b_gu, w_down, b_down, layer):
    d = D_MODEL
    n_rows = xs.shape[0] // ROW_TILES
    n_blocks = n_rows // MOE_ROWS
    row_block = (MOE_ROWS * ROW_TILES, LANES)

    def blk(i, nu):
        return jnp.minimum(i, nu[0] - 1)

    return pl.pallas_call(
        _expert_kernel,
        grid_spec=pltpu.PrefetchScalarGridSpec(
            num_scalar_prefetch=2,
            grid=(n_blocks,),
            in_specs=[
                pl.BlockSpec(row_block, lambda i, be, nu: (blk(i, nu), 0)),
                pl.BlockSpec((None, None, d, 2 * D_FF), lambda i, be, nu: (layer, be[blk(i, nu)], 0, 0)),
                pl.BlockSpec((None, None, 1, 2 * D_FF), lambda i, be, nu: (layer, be[blk(i, nu)], 0, 0)),
                pl.BlockSpec((None, None, D_FF, d), lambda i, be, nu: (layer, be[blk(i, nu)], 0, 0)),
                pl.BlockSpec((None, None, 1, d), lambda i, be, nu: (layer, be[blk(i, nu)], 0, 0)),
            ],
            out_specs=pl.BlockSpec(row_block, lambda i, be, nu: (i, 0)),
            scratch_shapes=[pltpu.VMEM((d, 2 * D_FF), BF16), pltpu.VMEM((D_FF, d), BF16)]),
        out_shape=jax.ShapeDtypeStruct(xs.shape, F32),
        compiler_params=_cparams(("arbitrary",)),
        name="expert_mlp",
    )(blk_e, n_used, xs, w_gu, b_gu.reshape(DEPTH, N_EXPERTS, 1, 2 * D_FF), w_down,
      b_down.reshape(DEPTH, N_EXPERTS, 1, d))


def _combine_kernel(dest_ref, dnext_ref, gates_ref, ys_hbm, x1_ref, g2_ref, lng_ref, lnb_ref, out_ref,
                    buf, sem):
    tc = dest_ref.shape[1]
    i = pl.program_id(0)
    n = pl.num_programs(0)
    slot = i % 2

    def gather(d_ref, slot):
        def body(r, carry):
            for k in range(TOP_K):
                pltpu.make_async_copy(ys_hbm.at[_tile_rows(d_ref[k, r], 1)],
                                      buf.at[slot, k, _tile_rows(r, 1)],
                                      sem.at[slot]).start(priority=k % 2)
            return carry

        lax.fori_loop(0, tc, body, 0, unroll=8)

    @pl.when(i == 0)
    def _():
        gather(dest_ref, 0)

    @pl.when(i + 1 < n)
    def _():
        gather(dnext_ref, 1 - slot)

    for k in range(TOP_K):
        pltpu.make_async_copy(ys_hbm.at[_tile_rows(0, tc)], buf.at[slot, k], sem.at[slot]).wait()
    gpad = jnp.concatenate([gates_ref[...], jnp.zeros((LANES - TOP_K, tc), F32)], 0)
    gt = gpad.T
    f = _load_row_tiles(buf.at[slot, 0], tc) * gt[:, 0:1]
    for k in range(1, TOP_K):
        f = f + _load_row_tiles(buf.at[slot, k], tc) * gt[:, k:k + 1]
    out_ref[...] = _layer_norm(DN_ALPHA * x1_ref[...] + g2_ref[...] * f, lng_ref[...], lnb_ref[...])


def _combine(dest, gates, ys, x1, g2, row_of_block, ln_g, ln_b):
    n, d = x1.shape
    tc = T_COMBINE
    nsteps = n // tc
    return pl.pallas_call(
        _combine_kernel,
        grid=(nsteps,),
        in_specs=[pl.BlockSpec((TOP_K, tc), lambda i: (0, i), memory_space=pltpu.SMEM),
                  pl.BlockSpec((TOP_K, tc), lambda i: (0, jnp.minimum(i + 1, nsteps - 1)),
                               memory_space=pltpu.SMEM),
                  pl.BlockSpec((TOP_K, tc), lambda i: (0, i)),
                  pl.BlockSpec(memory_space=pl.ANY),
                  pl.BlockSpec((tc, d), lambda i: (i, 0)),
                  pl.BlockSpec((None, 1, d), lambda i: (row_of_block(i), 0, 0)),
                  pl.BlockSpec(ln_g.shape, lambda i: (0, 0)),
                  pl.BlockSpec(ln_b.shape, lambda i: (0, 0))],
        out_specs=pl.BlockSpec((tc, d), lambda i: (i, 0)),
        out_shape=jax.ShapeDtypeStruct((n, d), F32),
        scratch_shapes=[pltpu.VMEM((2, TOP_K, tc * ROW_TILES, LANES), F32), pltpu.SemaphoreType.DMA((2,))],
        compiler_params=_cparams(("arbitrary",)),
        name="combine",
    )(dest, dest, gates, ys, x1, g2, ln_g, ln_b)


def _moe(h2, ids, gates, x1, g2, row_of_block, ln_g, ln_b, w_gu, b_gu, w_down, b_down, layer):
    n = ids.shape[1]
    nk = n * TOP_K
    n_blocks = -(-nk // MOE_ROWS) + N_EXPERTS
    rank, cnt = _expert_ranks(ids)
    counts = cnt[:, 0].astype(I32)
    padded = (counts + MOE_ROWS - 1) // MOE_ROWS * MOE_ROWS
    pad_end = jnp.cumsum(padded)
    pad_start = (pad_end - padded).astype(I32)
    blk_start = jnp.arange(n_blocks, dtype=I32) * MOE_ROWS
    blk_e = jnp.minimum(jnp.sum((pad_end[None, :] <= blk_start[:, None]).astype(I32), axis=1),
                        N_EXPERTS - 1).astype(I32)
    n_used = (pad_end[-1:] // MOE_ROWS).astype(I32)
    dest = _destinations(pad_start, ids, rank)
    xs = _dispatch(pad_start + counts, padded - counts, n_used, dest, h2, n_blocks * MOE_ROWS)
    ys = _expert_mlp(blk_e, n_used, xs, w_gu, b_gu, w_down, b_down, layer)
    return _combine(dest, gates, ys, x1, g2, row_of_block, ln_g, ln_b)


def _odd_inproj_kernel(x_ref, sc_ref, sh_ref, wk_ref, wt_ref, ck_ref, sk_ref, cq_ref, sq_ref,
                       qT_ref, k_ref, vT_ref):
    h = (x_ref[...] * (1.0 + sc_ref[...]) + sh_ref[...]).astype(BF16)
    kk = jnp.dot(h, wk_ref[...], preferred_element_type=F32)
    k_ref[...] = (kk * ck_ref[...] + _swap16_lanes(kk) * sk_ref[...]).astype(BF16)
    yt = lax.dot_general(wt_ref[...], h, (((1,), (1,)), ((), ())),
                         preferred_element_type=F32)
    qw = C_HEADS * C_HEAD_DIM
    qt = yt[:qw]
    n_pairs = C_HEADS // C_KV_HEADS
    cq = jnp.concatenate([cq_ref[...]] * n_pairs, 0)
    sq = jnp.concatenate([sq_ref[...]] * n_pairs, 0)
    qT_ref[...] = (qt * cq + _swap16_rows(qt) * sq).astype(BF16)
    vT_ref[...] = yt[qw:].astype(BF16)


def _odd_inproj(x, x_off, b, s, sc, sh, mod_row, wk, wt, ck, sk, cq, sq, tm):
    d = x.shape[1]
    qw = C_HEADS * C_HEAD_DIM
    nb = s // tm
    xo = x_off // tm
    return pl.pallas_call(
        _odd_inproj_kernel,
        grid=(b, nb),
        in_specs=[pl.BlockSpec((tm, d), lambda bi, i: (xo + bi * nb + i, 0)),
                  pl.BlockSpec((None, 1, d), lambda bi, i: (mod_row(bi), 0, 0)),
                  pl.BlockSpec((None, 1, d), lambda bi, i: (mod_row(bi), 0, 0)),
                  pl.BlockSpec(wk.shape, lambda bi, i: (0, 0)),
                  pl.BlockSpec(wt.shape, lambda bi, i: (0, 0)),
                  pl.BlockSpec((tm, LANES), lambda bi, i: (i, 0)),
                  pl.BlockSpec((tm, LANES), lambda bi, i: (i, 0)),
                  pl.BlockSpec((LANES, tm), lambda bi, i: (0, i)),
                  pl.BlockSpec((LANES, tm), lambda bi, i: (0, i))],
        out_specs=(pl.BlockSpec((None, qw, tm), lambda bi, i: (bi, 0, i)),
                   pl.BlockSpec((None, tm, LANES), lambda bi, i: (bi, i, 0)),
                   pl.BlockSpec((None, LANES, tm), lambda bi, i: (bi, 0, i))),
        out_shape=(jax.ShapeDtypeStruct((b, qw, s), BF16),
                   jax.ShapeDtypeStruct((b, s, LANES), BF16),
                   jax.ShapeDtypeStruct((b, LANES, s), BF16)),
        compiler_params=_cparams(("parallel", "arbitrary")),
        name="odd_inproj",
    )(x, sc, sh, wk, wt, ck, sk, cq, sq)


def _win_attn_kernel(qT_ref, kp_ref, k0_ref, kn_ref, kc_ref, vp_ref, v0_ref, vn_ref, vc_ref, sink_ref,
                     o_ref, p_ref):
    tq = qT_ref.shape[1]
    i = pl.program_id(1)
    nb = pl.num_programs(1)
    n_pairs = C_HEADS // C_KV_HEADS
    qT = qT_ref[...]
    row = lax.broadcasted_iota(I32, (LANES, tq), 0)
    lo = row < C_HEAD_DIM
    zero = jnp.zeros((LANES, tq), BF16)
    cols = []
    for j in range(n_pairs):
        t = qT[j * LANES:(j + 1) * LANES]
        cols.append(jnp.where(lo, t, zero))
        cols.append(jnp.where(lo, zero, t))
    qs = jnp.concatenate(cols, 1)
    kb = jnp.concatenate([kp_ref[...], k0_ref[...], kn_ref[...], kc_ref[...]], 0)
    vbT = jnp.concatenate([vp_ref[...], v0_ref[...], vn_ref[...], vc_ref[...]], 1)
    nk = kb.shape[0]
    s = jnp.dot(kb, qs, preferred_element_type=F32)
    key = lax.broadcasted_iota(I32, (nk, tq), 0)
    qpos = lax.broadcasted_iota(I32, (nk, tq), 1)
    ok = (jnp.abs(key - tq - qpos) <= C_WINDOW)
    ok = ok & ((key >= tq) | (i > 0)) & ((key < 2 * tq) | (i < nb - 1))
    ok = ok | (key >= 3 * tq)
    sink = sink_ref[...]
    dens = []
    for hb in range(C_HEADS):
        sl = slice(hb * tq, (hb + 1) * tq)
        sb = jnp.where(ok, s[:, sl], NEG_INF)
        m = jnp.maximum(jnp.max(sb, axis=0, keepdims=True), sink[:, sl])
        p = jnp.exp2(sb - m)
        dens.append(jnp.sum(p, axis=0, keepdims=True) + jnp.exp2(sink[:, sl] - m))
        p_ref[:, sl] = p.astype(BF16)
    oT = jnp.dot(vbT, p_ref[...], preferred_element_type=F32)
    outs = []
    for j in range(n_pairs):
        a = oT[:, (2 * j) * tq:(2 * j + 1) * tq] / dens[2 * j]
        b = oT[:, (2 * j + 1) * tq:(2 * j + 2) * tq] / dens[2 * j + 1]
        outs.append(jnp.where(lo, a, b).T)
    o_ref[...] = jnp.concatenate(outs, 1).astype(BF16)


def _win_attention(qT, k, vT, kc, vcT, sink):
    b, qw, s = qT.shape
    tq = TQ_WIN
    nb = s // tq
    lc = kc.shape[1]
    nk = 3 * tq + lc

    def kblk(off):
        return pl.BlockSpec((None, tq, LANES), lambda bi, i: (bi, jnp.clip(i + off, 0, nb - 1), 0))

    def vblk(off):
        return pl.BlockSpec((None, LANES, tq), lambda bi, i: (bi, 0, jnp.clip(i + off, 0, nb - 1)))

    return pl.pallas_call(
        _win_attn_kernel,
        grid=(b, nb),
        in_specs=[pl.BlockSpec((None, qw, tq), lambda bi, i: (bi, 0, i)),
                  kblk(-1), kblk(0), kblk(1), pl.BlockSpec((None, lc, LANES), lambda bi, i: (bi, 0, 0)),
                  vblk(-1), vblk(0), vblk(1), pl.BlockSpec((None, LANES, lc), lambda bi, i: (bi, 0, 0)),
                  pl.BlockSpec(sink.shape, lambda bi, i: (0, 0))],
        out_specs=pl.BlockSpec((None, tq, qw), lambda bi, i: (bi, i, 0)),
        out_shape=jax.ShapeDtypeStruct((b, s, qw), BF16),
        scratch_shapes=[pltpu.VMEM((nk, C_HEADS * tq), BF16)],
        compiler_params=_cparams(("parallel", "arbitrary")),
        name="window_attention",
    )(qT, k, k, k, kc, vT, vT, vT, vcT, sink)


def kernel(x, c, ctx, c_ctx, ada_w, ada_b, ln_g, ln_b, ab_w_in, ab_w_out, diff_lambda, diff_subln_g,
           conv_w, c_w_in, c_w_out, c_sink, router_w, router_b, w_gu, b_gu, w_down, b_down):
    b, s, d = x.shape
    lc = ctx.shape[1]
    n_c = b * lc
    ctx_row = b

    cvec = jnp.zeros((8, d), F32).at[:b].set(c).at[b].set(c_ctx)
    mod = _modulation(cvec, ada_w, ada_b)

    def mod_vec(layer, j):
        return mod[layer, :, j * d:(j + 1) * d].reshape(8, 1, d)

    lat_row = lambda bi: bi
    ctx_rowf = lambda bi: ctx_row

    cos64, sin64 = _rope_tables(s)
    cos_l = jnp.concatenate([cos64, cos64], -1)
    sin_l = jnp.concatenate([sin64, sin64], -1)
    one_c = jnp.ones((lc, LANES), F32)
    zero_c = jnp.zeros((lc, LANES), F32)
    qscale = A_QK_DIM ** -0.5 * math.log2(math.e)

    l = 0
    w_in = ab_w_in[0]
    wn = w_in[:, A_WIDTH:].astype(BF16)
    wn = jnp.concatenate([wn[:, :A_WIDTH], wn[:, 2 * A_WIDTH:]], 1)
    wt = jnp.concatenate([w_in[:, :A_WIDTH], w_in[:, 2 * A_WIDTH:3 * A_WIDTH]], 1).T.astype(BF16)
    sc1, sh1, g1 = mod_vec(l, 1), mod_vec(l, 0), mod_vec(l, 2)
    sh2, sc2, g2 = mod_vec(l, 3), mod_vec(l, 4), mod_vec(l, 5)
    lam_init = 0.8 - 0.6 * math.exp(-0.3 * l)

    x_tok = x.reshape(b * s, d)
    ctx_tok = ctx.reshape(n_c, d)
    qT, k, vT, bg, u = _even_inproj(x_tok, 0, b, s, sc1, sh1, lat_row, wn, wt, cos_l, sin_l,
                                    (cos_l * qscale).T, (sin_l * qscale).T, TM_PROJ)
    qTc, kc, vTc, bgc, uc = _even_inproj(ctx_tok, 0, b, lc, sc1, sh1, ctx_rowf, wn, wt, one_c, zero_c,
                                         (one_c * qscale).T, zero_c.T, lc)
    g_col = diff_subln_g[0].reshape(A_V_DIM, 1)
    a_lat = _diff_attention(qT, k, vT, kc, vTc, diff_lambda[0], g_col, lam_init, TQ_DIFF)
    a_ctx = _diff_attention(qTc, None, None, kc, vTc, diff_lambda[0], g_col, lam_init, lc)

    w_out = ab_w_out[0].astype(BF16)
    wa, wb = w_out[:A_WIDTH], w_out[A_WIDTH:]
    lng0, lnb0 = ln_g[l, 0].reshape(1, d), ln_b[l, 0].reshape(1, d)
    lng1, lnb1 = ln_g[l, 1].reshape(1, d), ln_b[l, 1].reshape(1, d)
    rwT = router_w[l].T.astype(BF16)
    rb = router_b[l].reshape(N_EXPERTS, 1)
    n0 = n_c + b * s
    empty = (jnp.zeros((n0, d), F32), jnp.zeros((n0 * ROW_TILES, LANES), F32),
             jnp.zeros((TOP_K, n0), I32), jnp.zeros((TOP_K, n0), F32))
    part = _outproj(a_ctx, (bgc, uc, conv_w[0]), wa, wb, ctx_tok, 0, (g1, sc2, sh2), ctx_rowf,
                    lng0, lnb0, rwT, rb, lc, 0, empty)
    x1a, h2a, idsa, gatesa = _outproj(a_lat, (bg, u, conv_w[0]), wa, wb, x_tok, 0, (g1, sc2, sh2), lat_row,
                                      lng0, lnb0, rwT, rb, TM_PROJ, n_c, part)
    ncb = n_c // T_COMBINE
    spb = s // T_COMBINE
    row_of_block0 = lambda i: jnp.where(i < ncb, ctx_row, jnp.maximum(i - ncb, 0) // spb)
    y_all = _moe(h2a, idsa, gatesa, x1a, g2, row_of_block0, lng1, lnb1, w_gu, b_gu, w_down, b_down, l)

    l = 1
    sc1, sh1, g1 = mod_vec(l, 1), mod_vec(l, 0), mod_vec(l, 2)
    sh2, sc2, g2 = mod_vec(l, 3), mod_vec(l, 4), mod_vec(l, 5)
    g = C_HEADS // C_KV_HEADS
    perm = jnp.array([(kv * g + j) * C_HEAD_DIM + dd for j in range(g) for kv in range(C_KV_HEADS)
                      for dd in range(C_HEAD_DIM)], I32)
    w_in = c_w_in[0]
    qw = C_HEADS * C_HEAD_DIM
    kvw = C_KV_HEADS * C_HEAD_DIM
    wk_odd = w_in[:, qw:qw + kvw].astype(BF16)
    wt_odd = jnp.concatenate([w_in[:, :qw][:, perm], w_in[:, qw + kvw:]], 1).T.astype(BF16)
    wscale = C_HEAD_DIM ** -0.5 * math.log2(math.e)
    qT, k, vT = _odd_inproj(y_all, n_c, b, s, sc1, sh1, lat_row, wk_odd, wt_odd, cos_l, sin_l,
                            (cos_l * wscale).T, (sin_l * wscale).T, TM_PROJ)
    _, kc, vcT = _odd_inproj(y_all, 0, b, lc, sc1, sh1, ctx_rowf, wk_odd, wt_odd, one_c, zero_c,
                             (one_c * wscale).T, zero_c.T, lc)
    sink = c_sink[0][perm[::C_HEAD_DIM] // C_HEAD_DIM] * math.log2(math.e)
    sink = jnp.repeat(sink, TQ_WIN).reshape(1, C_HEADS * TQ_WIN)
    o = _win_attention(qT, k, vT, kc, vcT, sink)
    wa = c_w_out[0][perm].astype(BF16)
    lng0, lnb0 = ln_g[l, 0].reshape(1, d), ln_b[l, 0].reshape(1, d)
    lng1, lnb1 = ln_g[l, 1].reshape(1, d), ln_b[l, 1].reshape(1, d)
    rwT = router_w[l].T.astype(BF16)
    rb = router_b[l].reshape(N_EXPERTS, 1)
    x1a, h2a, idsa, gatesa = _outproj(o, None, wa, None, y_all, n_c, (g1, sc2, sh2), lat_row,
                                      lng0, lnb0, rwT, rb, TM_PROJ, 0, ())
    row_of_block1 = lambda i: i // spb
    y = _moe(h2a, idsa, gatesa, x1a, g2, row_of_block1, lng1, lnb1, w_gu, b_gu, w_down, b_down, l)
    return y.reshape(b, s, d)
```

```python
import functools
import math

import jax
import jax.numpy as jnp
from jax import lax
from jax.experimental import pallas as pl
from jax.experimental.pallas import tpu as pltpu

F32 = jnp.float32
BF16 = jnp.bfloat16
I32 = jnp.int32

D_MODEL = 1024
DEPTH = 2
GRID_W = 64
ROPE_DIM = 64
ROPE_BASE = 10000.0
A_HEADS = 4
A_QK_DIM = 64
A_V_DIM = 128
A_WIDTH = 512
B_WIDTH = 512
C_HEADS = 16
C_KV_HEADS = 2
C_HEAD_DIM = 64
C_WINDOW = 128
N_EXPERTS = 32
TOP_K = 4
D_FF = 1024
SWIGLU_ALPHA = 1.702
SWIGLU_LIMIT = 7.0
LN_EPS = 1e-5
RMS_EPS = 1e-5
NEG_INF = -1e30
DN_ALPHA = (2 * DEPTH) ** 0.25

LANES = 128
SUBLANES = 8
VMEM_LIMIT = 56 * 1024 * 1024

TM_PROJ = 512
TQ_DIFF = 512
TQ_WIN = 128
MOE_ROWS = 512
T_RANK = 1024
T_DISPATCH = 512
T_COMBINE = 512


def _cparams(sem):
    return pltpu.CompilerParams(dimension_semantics=sem, vmem_limit_bytes=VMEM_LIMIT)


ROW_TILES = D_MODEL // LANES
assert ROW_TILES == SUBLANES


def _load_row_tiles(ref, rows):
    return jnp.concatenate([ref[pl.ds(j, rows, stride=ROW_TILES), :] for j in range(ROW_TILES)], axis=1)


def _store_row_tiles(ref, val):
    rows = val.shape[0]
    for j in range(ROW_TILES):
        ref[pl.ds(j, rows, stride=ROW_TILES), :] = val[:, j * LANES:(j + 1) * LANES]


def _mod_kernel(c_ref, w_ref, b_ref, o_ref):
    c = c_ref[...]
    s = c * jax.nn.sigmoid(c)
    o_ref[...] = jnp.dot(s.astype(BF16), w_ref[...].astype(BF16),
                         preferred_element_type=F32) + b_ref[...]


def _modulation(cvec, ada_w, ada_b):
    d = D_MODEL
    tn = 1536
    return pl.pallas_call(
        _mod_kernel,
        grid=(DEPTH, 6 * d // tn),
        in_specs=[pl.BlockSpec((8, d), lambda l, j: (0, 0)),
                  pl.BlockSpec((None, d, tn), lambda l, j: (l, 0, j)),
                  pl.BlockSpec((None, 1, tn), lambda l, j: (l, 0, j))],
        out_specs=pl.BlockSpec((None, 8, tn), lambda l, j: (l, 0, j)),
        out_shape=jax.ShapeDtypeStruct((DEPTH, 8, 6 * d), F32),
        compiler_params=_cparams(("arbitrary", "arbitrary")),
        name="modulation",
    )(cvec, ada_w, ada_b.reshape(DEPTH, 1, 6 * d))


def _swap16_lanes(t):
    lane = lax.broadcasted_iota(I32, t.shape, 1)
    first = (lane % 32) < 16
    return jnp.where(first, pltpu.roll(t, LANES - 16, 1), pltpu.roll(t, 16, 1))


def _swap16_rows(a):
    pieces = []
    for i in range(0, a.shape[0], 32):
        pieces.append(a[i + 16:i + 32])
        pieces.append(a[i:i + 16])
    return jnp.concatenate(pieces, 0)


def _rope_tables(s):
    rows = jnp.repeat(jnp.arange(s // GRID_W, dtype=F32), GRID_W)
    cols = jnp.tile(jnp.arange(GRID_W, dtype=F32), s // GRID_W)
    axis_dim = ROPE_DIM // 2
    inv = ROPE_BASE ** (-jnp.arange(0, axis_dim, 2, dtype=F32) / axis_dim)
    ar, ac = rows[:, None] * inv, cols[:, None] * inv
    cr, sr, cc, sc = jnp.cos(ar), jnp.sin(ar), jnp.cos(ac), jnp.sin(ac)
    cos64 = jnp.concatenate([cr, cr, cc, cc], -1)
    sin64 = jnp.concatenate([-sr, sr, -sc, sc], -1)
    return cos64, sin64


def _even_inproj_kernel(x_ref, sc_ref, sh_ref, wn_ref, wt_ref, ck_ref, sk_ref, cq_ref, sq_ref,
                        qT_ref, k_ref, vT_ref, bg_ref, u_ref):
    tm = x_ref.shape[0]
    h = (x_ref[...] * (1.0 + sc_ref[...]) + sh_ref[...]).astype(BF16)
    yn = jnp.dot(h, wn_ref[...], preferred_element_type=F32)
    ck = ck_ref[...]
    sk = sk_ref[...]
    pieces = []
    for j in range(A_HEADS):
        t = yn[:, j * LANES:(j + 1) * LANES]
        pieces.append(t * ck + _swap16_lanes(t) * sk)
    k_ref[...] = jnp.concatenate(pieces, 1).astype(BF16)
    bg_ref[...] = yn[:, A_WIDTH:A_WIDTH + B_WIDTH]
    u_ref[...] = yn[:, A_WIDTH + B_WIDTH:A_WIDTH + 2 * B_WIDTH] * yn[:, A_WIDTH + 2 * B_WIDTH:]
    yt = lax.dot_general(wt_ref[...], h, (((1,), (1,)), ((), ())),
                         preferred_element_type=F32)
    qt = yt[:A_WIDTH]
    cq = jnp.concatenate([cq_ref[...]] * A_HEADS, 0)
    sq = jnp.concatenate([sq_ref[...]] * A_HEADS, 0)
    qT_ref[...] = (qt * cq + _swap16_rows(qt) * sq).astype(BF16).reshape(A_HEADS, A_V_DIM, tm)
    vT_ref[...] = yt[A_WIDTH:].astype(BF16).reshape(A_HEADS, A_V_DIM, tm)


def _even_inproj(x, x_off, b, s, sc, sh, mod_row, wn, wt, ck, sk, cq, sq, tm):
    d = x.shape[1]
    nb = s // tm
    xo = x_off // tm
    out_shape = (
        jax.ShapeDtypeStruct((b, A_HEADS, A_V_DIM, s), BF16),
        jax.ShapeDtypeStruct((b, s, A_WIDTH), BF16),
        jax.ShapeDtypeStruct((b, A_HEADS, nb, A_V_DIM, tm), BF16),
        jax.ShapeDtypeStruct((b, s, B_WIDTH), F32),
        jax.ShapeDtypeStruct((b, s, B_WIDTH), F32),
    )
    return pl.pallas_call(
        _even_inproj_kernel,
        grid=(b, nb),
        in_specs=[
            pl.BlockSpec((tm, d), lambda bi, i: (xo + bi * nb + i, 0)),
            pl.BlockSpec((None, 1, d), lambda bi, i: (mod_row(bi), 0, 0)),
            pl.BlockSpec((None, 1, d), lambda bi, i: (mod_row(bi), 0, 0)),
            pl.BlockSpec(wn.shape, lambda bi, i: (0, 0)),
            pl.BlockSpec(wt.shape, lambda bi, i: (0, 0)),
            pl.BlockSpec((tm, LANES), lambda bi, i: (i, 0)),
            pl.BlockSpec((tm, LANES), lambda bi, i: (i, 0)),
            pl.BlockSpec((LANES, tm), lambda bi, i: (0, i)),
            pl.BlockSpec((LANES, tm), lambda bi, i: (0, i)),
        ],
        out_specs=(
            pl.BlockSpec((None, A_HEADS, A_V_DIM, tm), lambda bi, i: (bi, 0, 0, i)),
            pl.BlockSpec((None, tm, A_WIDTH), lambda bi, i: (bi, i, 0)),
            pl.BlockSpec((None, A_HEADS, None, A_V_DIM, tm), lambda bi, i: (bi, 0, i, 0, 0)),
            pl.BlockSpec((None, tm, B_WIDTH), lambda bi, i: (bi, i, 0)),
            pl.BlockSpec((None, tm, B_WIDTH), lambda bi, i: (bi, i, 0)),
        ),
        out_shape=out_shape,
        compiler_params=_cparams(("parallel", "arbitrary")),
        name="even_inproj",
    )(x, sc, sh, wn, wt, ck, sk, cq, sq)


def _diff_attn_kernel(*refs, n_chunks, lam_init, with_lat):
    if with_lat:
        (qT_ref, k_ref, vT_ref, kc_ref, vTc_ref, lamp_ref, g_ref, o_ref,
         acc1, acc2, sa1, sa2, sb1, sb2) = refs
    else:
        qT_ref, kc_ref, vTc_ref, lamp_ref, g_ref, o_ref, acc1, acc2 = refs
    tq = qT_ref.shape[1]
    qT = qT_ref[...]
    row = lax.broadcasted_iota(I32, qT.shape, 0)
    zero = jnp.zeros_like(qT)
    q1 = jnp.where(row < A_QK_DIM, qT, zero)
    q2 = jnp.where(row >= A_QK_DIM, qT, zero)
    acc1[...] = jnp.zeros_like(acc1)
    acc2[...] = jnp.zeros_like(acc2)

    def one_map(s, vTc, m, l, acc):
        m_new = jnp.maximum(m, jnp.max(s, axis=0, keepdims=True))
        alpha = jnp.exp2(m - m_new)
        p = jnp.exp2(s - m_new)
        l_new = alpha * l + jnp.sum(p, axis=0, keepdims=True)
        acc[...] = alpha * acc[...] + jnp.dot(vTc, p.astype(BF16), preferred_element_type=F32)
        return m_new, l_new

    def softmax_pv(s1, s2, vTc, carry):
        m1, l1, m2, l2 = carry
        m1, l1 = one_map(s1, vTc, m1, l1, acc1)
        m2, l2 = one_map(s2, vTc, m2, l2, acc2)
        return m1, l1, m2, l2

    neg = jnp.full((1, tq), -jnp.inf, F32)
    zer = jnp.zeros((1, tq), F32)
    kc = kc_ref[...]
    carry = softmax_pv(jnp.dot(kc, q1, preferred_element_type=F32),
                       jnp.dot(kc, q2, preferred_element_type=F32), vTc_ref[...], (neg, zer, neg, zer))
    if with_lat:
        tk = vT_ref.shape[2]

        def scores(c, s1_ref, s2_ref):
            kk = k_ref[pl.ds(pl.multiple_of(c * tk, tk), tk), :]
            s1_ref[...] = jnp.dot(kk, q1, preferred_element_type=F32)
            s2_ref[...] = jnp.dot(kk, q2, preferred_element_type=F32)

        scores(0, sa1, sa2)
        bufs = ((sa1, sa2), (sb1, sb2))

        def steps(c0, count, carry, last):
            for j in range(count):
                cur, nxt = bufs[j % 2], bufs[(j + 1) % 2]
                if not (last and j == count - 1):
                    scores(c0 + j + 1, *nxt)
                carry = softmax_pv(cur[0][...], cur[1][...], vT_ref[c0 + j], carry)
            return carry

        unroll = 4
        n_body = max(n_chunks // unroll - 1, 0)
        carry = lax.fori_loop(0, n_body, lambda j, cr: steps(j * unroll, unroll, cr, False), carry)
        carry = steps(n_body * unroll, n_chunks - n_body * unroll, carry, True)
    m1, l1, m2, l2 = carry

    lf = lamp_ref[...]
    e1 = jnp.exp(jnp.sum(lf[0:1] * lf[1:2], axis=1, keepdims=True))
    e2 = jnp.exp(jnp.sum(lf[2:3] * lf[3:4], axis=1, keepdims=True))
    lam = e1 - e2 + lam_init
    o = acc1[...] / l1 - lam * (acc2[...] / l2)
    o = o * lax.rsqrt(jnp.mean(o * o, axis=0, keepdims=True) + RMS_EPS)
    o = o * g_ref[...] * (1.0 - lam_init)
    o_ref[...] = o.T.astype(BF16)


def _diff_attention(qT, k, vT, kc, vTc, lam_p, g_col, lam_init, tq):
    b, _, _, s = qT.shape
    lc = kc.shape[1]
    with_lat = k is not None
    in_specs = [pl.BlockSpec((None, None, A_V_DIM, tq), lambda bi, h, i: (bi, h, 0, i))]
    args = [qT]
    n_chunks = 0
    if with_lat:
        sk = k.shape[1]
        n_chunks, tk = vT.shape[2], vT.shape[4]
        in_specs += [pl.BlockSpec((None, sk, LANES), lambda bi, h, i: (bi, 0, h)),
                     pl.BlockSpec((None, None, n_chunks, A_V_DIM, tk), lambda bi, h, i: (bi, h, 0, 0, 0))]
        args += [k, vT]
    in_specs += [pl.BlockSpec((None, lc, LANES), lambda bi, h, i: (bi, 0, h)),
                 pl.BlockSpec((None, None, None, A_V_DIM, lc), lambda bi, h, i: (bi, h, 0, 0, 0)),
                 pl.BlockSpec(lam_p.shape, lambda bi, h, i: (0, 0)),
                 pl.BlockSpec(g_col.shape, lambda bi, h, i: (0, 0))]
    args += [kc, vTc, lam_p, g_col]
    scratch = [pltpu.VMEM((A_V_DIM, tq), F32), pltpu.VMEM((A_V_DIM, tq), F32)]
    if with_lat:
        scratch += [pltpu.VMEM((tk, tq), F32)] * 4
    return pl.pallas_call(
        functools.partial(_diff_attn_kernel, n_chunks=n_chunks, lam_init=lam_init, with_lat=with_lat),
        grid=(b, A_HEADS, s // tq),
        in_specs=in_specs,
        out_specs=pl.BlockSpec((None, tq, A_V_DIM), lambda bi, h, i: (bi, i, h)),
        out_shape=jax.ShapeDtypeStruct((b, s, A_WIDTH), BF16),
        scratch_shapes=scratch,
        compiler_params=_cparams(("parallel", "parallel", "arbitrary")),
        name="diff_attention_lat" if with_lat else "diff_attention_ctx",
    )(*args)


def _layer_norm(z, g, b):
    mu = jnp.mean(z, axis=-1, keepdims=True)
    zc = z - mu
    var = jnp.mean(zc * zc, axis=-1, keepdims=True)
    return zc * lax.rsqrt(var + LN_EPS) * g + b


def _top4_router(h2, rwT_ref, rb_ref, ids_ref, gates_ref):
    tm = h2.shape[0]
    lt = lax.dot_general(rwT_ref[...], h2.astype(BF16), (((1,), (1,)), ((), ())),
                         preferred_element_type=F32) + rb_ref[...]
    eidx = lax.broadcasted_iota(I32, lt.shape, 0)
    vals, ids = [], []
    for _ in range(TOP_K):
        mx = jnp.max(lt, axis=0, keepdims=True)
        idx = jnp.min(jnp.where(lt == mx, eidx, N_EXPERTS), axis=0, keepdims=True)
        vals.append(mx)
        ids.append(idx)
        lt = jnp.where(eidx == idx, -jnp.inf, lt)
    es = [jnp.exp(v - vals[0]) for v in vals]
    den = es[0] + es[1] + es[2] + es[3]
    ids_ref[...] = jnp.concatenate(ids, 0)
    gates_ref[...] = jnp.concatenate([e / den for e in es], 0)
    del tm


def _outproj_kernel(*refs, has_conv):
    if has_conv:
        (a_ref, bg_ref, u_ref, up_ref, un_ref, cw_ref, wa_ref, wb_ref, x_ref, g1_ref, lng_ref, lnb_ref,
         sc2_ref, sh2_ref, rwT_ref, rb_ref, x1_ref, h2_ref, ids_ref, gates_ref) = refs
    else:
        (a_ref, wa_ref, x_ref, g1_ref, lng_ref, lnb_ref,
         sc2_ref, sh2_ref, rwT_ref, rb_ref, x1_ref, h2_ref, ids_ref, gates_ref) = refs
    y = jnp.dot(a_ref[...], wa_ref[...], preferred_element_type=F32)
    if has_conv:
        tm = u_ref.shape[0]
        i = pl.program_id(1)
        nb = pl.num_programs(1)
        u = u_ref[...]
        prev = jnp.where(i > 0, up_ref[SUBLANES - 1:SUBLANES, :], 0.0)
        nxt = jnp.where(i < nb - 1, un_ref[0:1, :], 0.0)
        r = lax.broadcasted_iota(I32, u.shape, 0)
        um1 = jnp.where(r == 0, prev, pltpu.roll(u, 1, 0))
        up1 = jnp.where(r == tm - 1, nxt, pltpu.roll(u, tm - 1, 0))
        cw = cw_ref[...]
        conv = um1 * cw[0:1] + u * cw[1:2] + up1 * cw[2:3]
        b_mix = (bg_ref[...] * conv).astype(BF16)
        y = y + jnp.dot(b_mix, wb_ref[...], preferred_element_type=F32)
    x1 = _layer_norm(DN_ALPHA * x_ref[...] + g1_ref[...] * y, lng_ref[...], lnb_ref[...])
    x1_ref[...] = x1
    h2 = x1 * (1.0 + sc2_ref[...]) + sh2_ref[...]
    _store_row_tiles(h2_ref, h2)
    _top4_router(h2, rwT_ref, rb_ref, ids_ref, gates_ref)


def _outproj(a, conv_in, wa, wb, x, x_off, mod_vecs, mod_row, ln_g, ln_b, rwT, rb, tm, tok_off, prev_out):
    b, s = a.shape[0], a.shape[1]
    d = x.shape[1]
    nb = s // tm
    g1, sc2, sh2 = mod_vecs
    has_conv = conv_in is not None
    n_tok = prev_out[0].shape[0] if prev_out else b * s
    ob = tok_off // tm
    xo = x_off // tm

    def row_spec(width):
        return pl.BlockSpec((None, tm, width), lambda bi, i: (bi, i, 0))

    def vec_spec():
        return pl.BlockSpec((None, 1, d), lambda bi, i: (mod_row(bi), 0, 0))

    def full(arr):
        return pl.BlockSpec(arr.shape, lambda bi, i: (0,) * arr.ndim)

    in_specs = [row_spec(a.shape[-1])]
    args = [a]
    if has_conv:
        bg, u, cw = conv_in
        s8 = s // SUBLANES
        t8 = tm // SUBLANES
        in_specs += [row_spec(B_WIDTH), row_spec(B_WIDTH),
                     pl.BlockSpec((None, SUBLANES, B_WIDTH),
                                  lambda bi, i: (bi, jnp.maximum(i * t8 - 1, 0), 0)),
                     pl.BlockSpec((None, SUBLANES, B_WIDTH),
                                  lambda bi, i: (bi, jnp.minimum((i + 1) * t8, s8 - 1), 0)),
                     full(cw), full(wa), full(wb)]
        args += [bg, u, u, u, cw, wa, wb]
    else:
        in_specs += [full(wa)]
        args += [wa]
    in_specs += [pl.BlockSpec((tm, d), lambda bi, i: (xo + bi * nb + i, 0)),
                 vec_spec(), full(ln_g), full(ln_b), vec_spec(), vec_spec(), full(rwT), full(rb)]
    args += [x, g1, ln_g, ln_b, sc2, sh2, rwT, rb]
    n_in = len(args)
    n_alias = len(prev_out)
    in_specs += [pl.BlockSpec(memory_space=pl.ANY)] * n_alias
    args += list(prev_out)
    out_specs = (
        pl.BlockSpec((tm, d), lambda bi, i: (ob + bi * nb + i, 0)),
        pl.BlockSpec((tm * ROW_TILES, LANES), lambda bi, i: (ob + bi * nb + i, 0)),
        pl.BlockSpec((TOP_K, tm), lambda bi, i: (0, ob + bi * nb + i)),
        pl.BlockSpec((TOP_K, tm), lambda bi, i: (0, ob + bi * nb + i)),
    )
    out_shape = (
        jax.ShapeDtypeStruct((n_tok, d), F32),
        jax.ShapeDtypeStruct((n_tok * ROW_TILES, LANES), F32),
        jax.ShapeDtypeStruct((TOP_K, n_tok), I32),
        jax.ShapeDtypeStruct((TOP_K, n_tok), F32),
    )

    def kern(*refs):
        refs = refs[:n_in] + refs[n_in + n_alias:]
        _outproj_kernel(*refs, has_conv=has_conv)

    return pl.pallas_call(
        kern,
        grid=(b, nb),
        in_specs=in_specs,
        out_specs=out_specs,
        out_shape=out_shape,
        input_output_aliases={n_in + j: j for j in range(n_alias)},
        compiler_params=_cparams(("parallel", "arbitrary")),
        name="outproj_conv" if has_conv else "outproj",
    )(*args)


def _rank_kernel(ids_ref, rank_ref, cnt_ref, run_ref):
    i = pl.program_id(0)
    tr = ids_ref.shape[1]

    @pl.when(i == 0)
    def _():
        run_ref[...] = jnp.zeros_like(run_ref)

    ids = ids_ref[...]
    eidx = lax.broadcasted_iota(I32, (N_EXPERTS, tr), 0)
    hits = [eidx == ids[k:k + 1] for k in range(TOP_K)]
    member = (hits[0] | hits[1] | hits[2] | hits[3]).astype(F32)
    r = lax.broadcasted_iota(I32, (tr, tr), 0)
    c = lax.broadcasted_iota(I32, (tr, tr), 1)
    upper = (r < c).astype(BF16)
    prefix = jnp.dot(member.astype(BF16), upper, preferred_element_type=F32)
    base = run_ref[:, 0:1] + prefix
    ranks = [jnp.sum(jnp.where(hits[k], base, 0.0), axis=0, keepdims=True) for k in range(TOP_K)]
    rank_ref[...] = jnp.concatenate(ranks, 0).astype(I32)
    run_ref[...] = run_ref[...] + jnp.sum(member, axis=1, keepdims=True)
    cnt_ref[...] = run_ref[...]


def _expert_ranks(ids):
    n = ids.shape[1]
    return pl.pallas_call(
        _rank_kernel,
        grid=(n // T_RANK,),
        in_specs=[pl.BlockSpec((TOP_K, T_RANK), lambda i: (0, i))],
        out_specs=(pl.BlockSpec((TOP_K, T_RANK), lambda i: (0, i)),
                   pl.BlockSpec((N_EXPERTS, LANES), lambda i: (0, 0))),
        out_shape=(jax.ShapeDtypeStruct((TOP_K, n), I32),
                   jax.ShapeDtypeStruct((N_EXPERTS, LANES), F32)),
        scratch_shapes=[pltpu.VMEM((N_EXPERTS, LANES), F32)],
        compiler_params=_cparams(("arbitrary",)),
        name="expert_ranks",
    )(ids)


def _dest_kernel(ps_ref, ids_ref, rank_ref, dest_ref):
    ids = ids_ref[...]
    acc = rank_ref[...]
    for e in range(N_EXPERTS):
        acc = acc + jnp.where(ids == e, ps_ref[e], 0)
    dest_ref[...] = acc


def _destinations(pad_start, ids, rank):
    n = ids.shape[1]
    tn = next(t for t in (16384, 11264, 8192, 4096, 2048, 1024, 512) if n % t == 0)
    return pl.pallas_call(
        _dest_kernel,
        grid_spec=pltpu.PrefetchScalarGridSpec(
            num_scalar_prefetch=1,
            grid=(n // tn,),
            in_specs=[pl.BlockSpec((TOP_K, tn), lambda i, ps: (0, i)),
                      pl.BlockSpec((TOP_K, tn), lambda i, ps: (0, i))],
            out_specs=pl.BlockSpec((TOP_K, tn), lambda i, ps: (0, i))),
        out_shape=jax.ShapeDtypeStruct((TOP_K, n), I32),
        compiler_params=_cparams(("arbitrary",)),
        name="destinations",
    )(pad_start, ids, rank)


_PAD_BITS = tuple(1 << j for j in reversed(range(MOE_ROWS.bit_length() - 1)))


def _tile_rows(start, n):
    return pl.ds(pl.multiple_of(start * ROW_TILES, ROW_TILES), n * ROW_TILES)


def _dispatch_kernel(pp_ref, np_ref, nu_ref, dest_ref, h_ref, xs_hbm, zeros_ref, sem, zsem):
    td = dest_ref.shape[1]
    n_blocks = xs_hbm.shape[0] // (MOE_ROWS * ROW_TILES)

    def pad_copy(e, bit):
        npad = np_ref[e]
        off = pp_ref[e] + (npad & (-2 * bit))
        return (npad & bit) != 0, pltpu.make_async_copy(zeros_ref.at[_tile_rows(0, bit)],
                                                        xs_hbm.at[_tile_rows(off, bit)], zsem)

    def blk_copy(blk):
        return pltpu.make_async_copy(zeros_ref, xs_hbm.at[_tile_rows(blk * MOE_ROWS, MOE_ROWS)], zsem)

    def pad_all(wait):
        def per_expert(e, carry):
            for bit in _PAD_BITS:
                cond, cp = pad_copy(e, bit)

                @pl.when(cond)
                def _():
                    cp.wait() if wait else cp.start()
            return carry

        def per_block(blk, carry):
            cp = blk_copy(blk)
            cp.wait() if wait else cp.start()
            return carry

        lax.fori_loop(0, N_EXPERTS, per_expert, 0)
        lax.fori_loop(nu_ref[0], n_blocks, per_block, 0)

    @pl.when(pl.program_id(0) == 0)
    def _():
        zeros_ref[...] = jnp.zeros_like(zeros_ref)
        pad_all(False)
        pad_all(True)

    def body(r, carry):
        for k in range(TOP_K):
            pltpu.make_async_copy(h_ref.at[_tile_rows(r, 1)], xs_hbm.at[_tile_rows(dest_ref[k, r], 1)],
                                  sem).start(priority=k % 2)
        return carry

    lax.fori_loop(0, td, body, 0, unroll=8)
    for k in range(TOP_K):
        pltpu.make_async_copy(h_ref, xs_hbm.at[_tile_rows(0, td)], sem).wait()


def _dispatch(pad_pos, n_pad, n_used, dest, h2, n_rows):
    n = h2.shape[0] // ROW_TILES
    return pl.pallas_call(
        _dispatch_kernel,
        grid_spec=pltpu.PrefetchScalarGridSpec(
            num_scalar_prefetch=3,
            grid=(n // T_DISPATCH,),
            in_specs=[pl.BlockSpec((TOP_K, T_DISPATCH), lambda i, *_: (0, i), memory_space=pltpu.SMEM),
                      pl.BlockSpec((T_DISPATCH * ROW_TILES, LANES), lambda i, *_: (i, 0))],
            out_specs=pl.BlockSpec(memory_space=pl.ANY),
            scratch_shapes=[pltpu.VMEM((MOE_ROWS * ROW_TILES, LANES), h2.dtype),
                            pltpu.SemaphoreType.DMA(()), pltpu.SemaphoreType.DMA(())]),
        out_shape=jax.ShapeDtypeStruct((n_rows * ROW_TILES, LANES), h2.dtype),
        compiler_params=pltpu.CompilerParams(dimension_semantics=("arbitrary",),
                                             has_side_effects=True, vmem_limit_bytes=VMEM_LIMIT),
        name="dispatch",
    )(pad_pos, n_pad, n_used, dest, h2)


def _expert_kernel(be_ref, nu_ref, xs_ref, wgu_ref, bgu_ref, wd_ref, bd_ref, ys_ref, wgu_bf, wd_bf):
    i = pl.program_id(0)
    nu = nu_ref[0]
    last = jnp.minimum(i, nu - 1)
    new_expert = (i == 0) | (be_ref[last] != be_ref[jnp.maximum(last - 1, 0)])

    @pl.when((i < nu) & new_expert)
    def _():
        wgu_bf[...] = wgu_ref[...].astype(BF16)
        wd_bf[...] = wd_ref[...].astype(BF16)

    @pl.when(i < nu)
    def _():
        x = _load_row_tiles(xs_ref, MOE_ROWS).astype(BF16)
        gu = jnp.dot(x, wgu_bf[...], preferred_element_type=F32) + bgu_ref[...]
        gate = jnp.minimum(gu[:, :D_FF], SWIGLU_LIMIT)
        lin = jnp.clip(gu[:, D_FF:], -SWIGLU_LIMIT, SWIGLU_LIMIT)
        act = gate * jax.nn.sigmoid(SWIGLU_ALPHA * gate) * (lin + 1.0)
        _store_row_tiles(ys_ref, jnp.dot(act.astype(BF16), wd_bf[...], preferred_element_type=F32)
                         + bd_ref[...])

    @pl.when(i >= nu_ref[0])
    def _():
        ys_ref[...] = jnp.zeros_like(ys_ref)


def _expert_mlp(blk_e, n_used, xs, w_gu, b_gu, w_down, b_down, layer):
    d = D_MODEL
    n_rows = xs.shape[0] // ROW_TILES
    n_blocks = n_rows // MOE_ROWS
    row_block = (MOE_ROWS * ROW_TILES, LANES)

    def blk(i, nu):
        return jnp.minimum(i, nu[0] - 1)

    return pl.pallas_call(
        _expert_kernel,
        grid_spec=pltpu.PrefetchScalarGridSpec(
            num_scalar_prefetch=2,
            grid=(n_blocks,),
            in_specs=[
                pl.BlockSpec(row_block, lambda i, be, nu: (blk(i, nu), 0)),
                pl.BlockSpec((None, None, d, 2 * D_FF), lambda i, be, nu: (layer, be[blk(i, nu)], 0, 0)),
                pl.BlockSpec((None, None, 1, 2 * D_FF), lambda i, be, nu: (layer, be[blk(i, nu)], 0, 0)),
                pl.BlockSpec((None, None, D_FF, d), lambda i, be, nu: (layer, be[blk(i, nu)], 0, 0)),
                pl.BlockSpec((None, None, 1, d), lambda i, be, nu: (layer, be[blk(i, nu)], 0, 0)),
            ],
            out_specs=pl.BlockSpec(row_block, lambda i, be, nu: (i, 0)),
            scratch_shapes=[pltpu.VMEM((d, 2 * D_FF), BF16), pltpu.VMEM((D_FF, d), BF16)]),
        out_shape=jax.ShapeDtypeStruct(xs.shape, F32),
        compiler_params=_cparams(("arbitrary",)),
        name="expert_mlp",
    )(blk_e, n_used, xs, w_gu, b_gu.reshape(DEPTH, N_EXPERTS, 1, 2 * D_FF), w_down,
      b_down.reshape(DEPTH, N_EXPERTS, 1, d))


COMBINE_PIECE = 16


def _combine_kernel(dest_ref, dnext_ref, gates_ref, ys_hbm, x1_ref, g2_ref, lng_ref, lnb_ref, out_ref,
                    buf, gt_ref, sem):
    tc = dest_ref.shape[1]
    i = pl.program_id(0)
    n = pl.num_programs(0)
    slot = i % 2

    def start_rows(d_ref, slot, r0, count):
        for r in range(count):
            for k in range(TOP_K):
                pltpu.make_async_copy(ys_hbm.at[_tile_rows(d_ref[k, r0 + r], 1)],
                                      buf.at[slot, k, _tile_rows(r0 + r, 1)],
                                      sem.at[slot]).start(priority=k % 2)

    def wait_slot(slot):
        for k in range(TOP_K):
            pltpu.make_async_copy(ys_hbm.at[_tile_rows(0, tc)], buf.at[slot, k], sem.at[slot]).wait()

    @pl.when(i == 0)
    def _():
        lax.fori_loop(0, tc // SUBLANES,
                      lambda j, c: (start_rows(dest_ref, 0, j * SUBLANES, SUBLANES), c)[1], 0)

    wait_slot(slot)
    gpad = jnp.concatenate([gates_ref[...], jnp.zeros((LANES - TOP_K, tc), F32)], 0)
    gt_ref[...] = gpad.T

    for r0 in range(0, tc, COMBINE_PIECE):
        rows = [jnp.concatenate(
            [buf[slot, k, pl.ds(r0 * ROW_TILES + j, COMBINE_PIECE, stride=ROW_TILES), :]
             for j in range(ROW_TILES)], axis=1) for k in range(TOP_K)]
        gt = gt_ref[pl.ds(r0, COMBINE_PIECE), :]
        x1 = x1_ref[pl.ds(r0, COMBINE_PIECE), :]
        start_rows(dnext_ref, 1 - slot, r0, COMBINE_PIECE)
        f = rows[0] * gt[:, 0:1]
        for k in range(1, TOP_K):
            f = f + rows[k] * gt[:, k:k + 1]
        z = DN_ALPHA * x1 + g2_ref[...] * f
        out_ref[pl.ds(r0, COMBINE_PIECE), :] = _layer_norm(z, lng_ref[...], lnb_ref[...])

    @pl.when(i == n - 1)
    def _():
        wait_slot(1 - slot)


def _combine(dest, gates, ys, x1, g2, row_of_block, ln_g, ln_b):
    n, d = x1.shape
    tc = T_COMBINE
    nsteps = n // tc
    return pl.pallas_call(
        _combine_kernel,
        grid=(nsteps,),
        in_specs=[pl.BlockSpec((TOP_K, tc), lambda i: (0, i), memory_space=pltpu.SMEM),
                  pl.BlockSpec((TOP_K, tc), lambda i: (0, jnp.minimum(i + 1, nsteps - 1)),
                               memory_space=pltpu.SMEM),
                  pl.BlockSpec((TOP_K, tc), lambda i: (0, i)),
                  pl.BlockSpec(memory_space=pl.ANY),
                  pl.BlockSpec((tc, d), lambda i: (i, 0)),
                  pl.BlockSpec((None, 1, d), lambda i: (row_of_block(i), 0, 0)),
                  pl.BlockSpec(ln_g.shape, lambda i: (0, 0)),
                  pl.BlockSpec(ln_b.shape, lambda i: (0, 0))],
        out_specs=pl.BlockSpec((tc, d), lambda i: (i, 0)),
        out_shape=jax.ShapeDtypeStruct((n, d), F32),
        scratch_shapes=[pltpu.VMEM((2, TOP_K, tc * ROW_TILES, LANES), F32), pltpu.VMEM((tc, LANES), F32),
                        pltpu.SemaphoreType.DMA((2,))],
        compiler_params=_cparams(("arbitrary",)),
        name="combine",
    )(dest, dest, gates, ys, x1, g2, ln_g, ln_b)


def _moe(h2, ids, gates, x1, g2, row_of_block, ln_g, ln_b, w_gu, b_gu, w_down, b_down, layer):
    n = ids.shape[1]
    nk = n * TOP_K
    n_blocks = -(-nk // MOE_ROWS) + N_EXPERTS
    rank, cnt = _expert_ranks(ids)
    counts = cnt[:, 0].astype(I32)
    padded = (counts + MOE_ROWS - 1) // MOE_ROWS * MOE_ROWS
    pad_end = jnp.cumsum(padded)
    pad_start = (pad_end - padded).astype(I32)
    blk_start = jnp.arange(n_blocks, dtype=I32) * MOE_ROWS
    blk_e = jnp.minimum(jnp.sum((pad_end[None, :] <= blk_start[:, None]).astype(I32), axis=1),
                        N_EXPERTS - 1).astype(I32)
    n_used = (pad_end[-1:] // MOE_ROWS).astype(I32)
    dest = _destinations(pad_start, ids, rank)
    xs = _dispatch(pad_start + counts, padded - counts, n_used, dest, h2, n_blocks * MOE_ROWS)
    ys = _expert_mlp(blk_e, n_used, xs, w_gu, b_gu, w_down, b_down, layer)
    return _combine(dest, gates, ys, x1, g2, row_of_block, ln_g, ln_b)


def _odd_inproj_kernel(x_ref, sc_ref, sh_ref, wk_ref, wt_ref, ck_ref, sk_ref, cq_ref, sq_ref,
                       qT_ref, k_ref, vT_ref):
    h = (x_ref[...] * (1.0 + sc_ref[...]) + sh_ref[...]).astype(BF16)
    kk = jnp.dot(h, wk_ref[...], preferred_element_type=F32)
    k_ref[...] = (kk * ck_ref[...] + _swap16_lanes(kk) * sk_ref[...]).astype(BF16)
    yt = lax.dot_general(wt_ref[...], h, (((1,), (1,)), ((), ())),
                         preferred_element_type=F32)
    qw = C_HEADS * C_HEAD_DIM
    qt = yt[:qw]
    n_pairs = C_HEADS // C_KV_HEADS
    cq = jnp.concatenate([cq_ref[...]] * n_pairs, 0)
    sq = jnp.concatenate([sq_ref[...]] * n_pairs, 0)
    qT_ref[...] = (qt * cq + _swap16_rows(qt) * sq).astype(BF16)
    vT_ref[...] = yt[qw:].astype(BF16)


def _odd_inproj(x, x_off, b, s, sc, sh, mod_row, wk, wt, ck, sk, cq, sq, tm):
    d = x.shape[1]
    qw = C_HEADS * C_HEAD_DIM
    nb = s // tm
    xo = x_off // tm
    return pl.pallas_call(
        _odd_inproj_kernel,
        grid=(b, nb),
        in_specs=[pl.BlockSpec((tm, d), lambda bi, i: (xo + bi * nb + i, 0)),
                  pl.BlockSpec((None, 1, d), lambda bi, i: (mod_row(bi), 0, 0)),
                  pl.BlockSpec((None, 1, d), lambda bi, i: (mod_row(bi), 0, 0)),
                  pl.BlockSpec(wk.shape, lambda bi, i: (0, 0)),
                  pl.BlockSpec(wt.shape, lambda bi, i: (0, 0)),
                  pl.BlockSpec((tm, LANES), lambda bi, i: (i, 0)),
                  pl.BlockSpec((tm, LANES), lambda bi, i: (i, 0)),
                  pl.BlockSpec((LANES, tm), lambda bi, i: (0, i)),
                  pl.BlockSpec((LANES, tm), lambda bi, i: (0, i))],
        out_specs=(pl.BlockSpec((None, qw, tm), lambda bi, i: (bi, 0, i)),
                   pl.BlockSpec((None, tm, LANES), lambda bi, i: (bi, i, 0)),
                   pl.BlockSpec((None, LANES, tm), lambda bi, i: (bi, 0, i))),
        out_shape=(jax.ShapeDtypeStruct((b, qw, s), BF16),
                   jax.ShapeDtypeStruct((b, s, LANES), BF16),
                   jax.ShapeDtypeStruct((b, LANES, s), BF16)),
        compiler_params=_cparams(("parallel", "arbitrary")),
        name="odd_inproj",
    )(x, sc, sh, wk, wt, ck, sk, cq, sq)


def _win_attn_kernel(qT_ref, kp_ref, k0_ref, kn_ref, kc_ref, vp_ref, v0_ref, vn_ref, vc_ref, sink_ref,
                     o_ref, p_ref):
    tq = qT_ref.shape[1]
    i = pl.program_id(1)
    nb = pl.num_programs(1)
    n_pairs = C_HEADS // C_KV_HEADS
    qT = qT_ref[...]
    row = lax.broadcasted_iota(I32, (LANES, tq), 0)
    lo = row < C_HEAD_DIM
    zero = jnp.zeros((LANES, tq), BF16)
    cols = []
    for j in range(n_pairs):
        t = qT[j * LANES:(j + 1) * LANES]
        cols.append(jnp.where(lo, t, zero))
        cols.append(jnp.where(lo, zero, t))
    qs = jnp.concatenate(cols, 1)
    kb = jnp.concatenate([kp_ref[...], k0_ref[...], kn_ref[...], kc_ref[...]], 0)
    vbT = jnp.concatenate([vp_ref[...], v0_ref[...], vn_ref[...], vc_ref[...]], 1)
    nk = kb.shape[0]
    s = jnp.dot(kb, qs, preferred_element_type=F32)
    key = lax.broadcasted_iota(I32, (nk, tq), 0)
    qpos = lax.broadcasted_iota(I32, (nk, tq), 1)
    ok = (jnp.abs(key - tq - qpos) <= C_WINDOW)
    ok = ok & ((key >= tq) | (i > 0)) & ((key < 2 * tq) | (i < nb - 1))
    ok = ok | (key >= 3 * tq)
    sink = sink_ref[...]
    dens = []
    for hb in range(C_HEADS):
        sl = slice(hb * tq, (hb + 1) * tq)
        sb = jnp.where(ok, s[:, sl], NEG_INF)
        m = jnp.maximum(jnp.max(sb, axis=0, keepdims=True), sink[:, sl])
        p = jnp.exp2(sb - m)
        dens.append(jnp.sum(p, axis=0, keepdims=True) + jnp.exp2(sink[:, sl] - m))
        p_ref[:, sl] = p.astype(BF16)
    oT = jnp.dot(vbT, p_ref[...], preferred_element_type=F32)
    outs = []
    for j in range(n_pairs):
        a = oT[:, (2 * j) * tq:(2 * j + 1) * tq] / dens[2 * j]
        b = oT[:, (2 * j + 1) * tq:(2 * j + 2) * tq] / dens[2 * j + 1]
        outs.append(jnp.where(lo, a, b).T)
    o_ref[...] = jnp.concatenate(outs, 1).astype(BF16)


def _win_attention(qT, k, vT, kc, vcT, sink):
    b, qw, s = qT.shape
    tq = TQ_WIN
    nb = s // tq
    lc = kc.shape[1]
    nk = 3 * tq + lc

    def kblk(off):
        return pl.BlockSpec((None, tq, LANES), lambda bi, i: (bi, jnp.clip(i + off, 0, nb - 1), 0))

    def vblk(off):
        return pl.BlockSpec((None, LANES, tq), lambda bi, i: (bi, 0, jnp.clip(i + off, 0, nb - 1)))

    return pl.pallas_call(
        _win_attn_kernel,
        grid=(b, nb),
        in_specs=[pl.BlockSpec((None, qw, tq), lambda bi, i: (bi, 0, i)),
                  kblk(-1), kblk(0), kblk(1), pl.BlockSpec((None, lc, LANES), lambda bi, i: (bi, 0, 0)),
                  vblk(-1), vblk(0), vblk(1), pl.BlockSpec((None, LANES, lc), lambda bi, i: (bi, 0, 0)),
                  pl.BlockSpec(sink.shape, lambda bi, i: (0, 0))],
        out_specs=pl.BlockSpec((None, tq, qw), lambda bi, i: (bi, i, 0)),
        out_shape=jax.ShapeDtypeStruct((b, s, qw), BF16),
        scratch_shapes=[pltpu.VMEM((nk, C_HEADS * tq), BF16)],
        compiler_params=_cparams(("parallel", "arbitrary")),
        name="window_attention",
    )(qT, k, k, k, kc, vT, vT, vT, vcT, sink)


def kernel(x, c, ctx, c_ctx, ada_w, ada_b, ln_g, ln_b, ab_w_in, ab_w_out, diff_lambda, diff_subln_g,
           conv_w, c_w_in, c_w_out, c_sink, router_w, router_b, w_gu, b_gu, w_down, b_down):
    b, s, d = x.shape
    lc = ctx.shape[1]
    n_c = b * lc
    ctx_row = b

    cvec = jnp.zeros((8, d), F32).at[:b].set(c).at[b].set(c_ctx)
    mod = _modulation(cvec, ada_w, ada_b)

    def mod_vec(layer, j):
        return mod[layer, :, j * d:(j + 1) * d].reshape(8, 1, d)

    lat_row = lambda bi: bi
    ctx_rowf = lambda bi: ctx_row

    cos64, sin64 = _rope_tables(s)
    cos_l = jnp.concatenate([cos64, cos64], -1)
    sin_l = jnp.concatenate([sin64, sin64], -1)
    one_c = jnp.ones((lc, LANES), F32)
    zero_c = jnp.zeros((lc, LANES), F32)
    qscale = A_QK_DIM ** -0.5 * math.log2(math.e)

    l = 0
    w_in = ab_w_in[0]
    wn = w_in[:, A_WIDTH:].astype(BF16)
    wn = jnp.concatenate([wn[:, :A_WIDTH], wn[:, 2 * A_WIDTH:]], 1)
    wt = jnp.concatenate([w_in[:, :A_WIDTH], w_in[:, 2 * A_WIDTH:3 * A_WIDTH]], 1).T.astype(BF16)
    sc1, sh1, g1 = mod_vec(l, 1), mod_vec(l, 0), mod_vec(l, 2)
    sh2, sc2, g2 = mod_vec(l, 3), mod_vec(l, 4), mod_vec(l, 5)
    lam_init = 0.8 - 0.6 * math.exp(-0.3 * l)

    x_tok = x.reshape(b * s, d)
    ctx_tok = ctx.reshape(n_c, d)
    qT, k, vT, bg, u = _even_inproj(x_tok, 0, b, s, sc1, sh1, lat_row, wn, wt, cos_l, sin_l,
                                    (cos_l * qscale).T, (sin_l * qscale).T, TM_PROJ)
    qTc, kc, vTc, bgc, uc = _even_inproj(ctx_tok, 0, b, lc, sc1, sh1, ctx_rowf, wn, wt, one_c, zero_c,
                                         (one_c * qscale).T, zero_c.T, lc)
    g_col = diff_subln_g[0].reshape(A_V_DIM, 1)
    a_lat = _diff_attention(qT, k, vT, kc, vTc, diff_lambda[0], g_col, lam_init, TQ_DIFF)
    a_ctx = _diff_attention(qTc, None, None, kc, vTc, diff_lambda[0], g_col, lam_init, lc)

    w_out = ab_w_out[0].astype(BF16)
    wa, wb = w_out[:A_WIDTH], w_out[A_WIDTH:]
    lng0, lnb0 = ln_g[l, 0].reshape(1, d), ln_b[l, 0].reshape(1, d)
    lng1, lnb1 = ln_g[l, 1].reshape(1, d), ln_b[l, 1].reshape(1, d)
    rwT = router_w[l].T.astype(BF16)
    rb = router_b[l].reshape(N_EXPERTS, 1)
    n0 = n_c + b * s
    empty = (jnp.zeros((n0, d), F32), jnp.zeros((n0 * ROW_TILES, LANES), F32),
             jnp.zeros((TOP_K, n0), I32), jnp.zeros((TOP_K, n0), F32))
    part = _outproj(a_ctx, (bgc, uc, conv_w[0]), wa, wb, ctx_tok, 0, (g1, sc2, sh2), ctx_rowf,
                    lng0, lnb0, rwT, rb, lc, 0, empty)
    x1a, h2a, idsa, gatesa = _outproj(a_lat, (bg, u, conv_w[0]), wa, wb, x_tok, 0, (g1, sc2, sh2), lat_row,
                                      lng0, lnb0, rwT, rb, TM_PROJ, n_c, part)
    ncb = n_c // T_COMBINE
    spb = s // T_COMBINE
    row_of_block0 = lambda i: jnp.where(i < ncb, ctx_row, jnp.maximum(i - ncb, 0) // spb)
    y_all = _moe(h2a, idsa, gatesa, x1a, g2, row_of_block0, lng1, lnb1, w_gu, b_gu, w_down, b_down, l)

    l = 1
    sc1, sh1, g1 = mod_vec(l, 1), mod_vec(l, 0), mod_vec(l, 2)
    sh2, sc2, g2 = mod_vec(l, 3), mod_vec(l, 4), mod_vec(l, 5)
    g = C_HEADS // C_KV_HEADS
    perm = jnp.array([(kv * g + j) * C_HEAD_DIM + dd for j in range(g) for kv in range(C_KV_HEADS)
                      for dd in range(C_HEAD_DIM)], I32)
    w_in = c_w_in[0]
    qw = C_HEADS * C_HEAD_DIM
    kvw = C_KV_HEADS * C_HEAD_DIM
    wk_odd = w_in[:, qw:qw + kvw].astype(BF16)
    wt_odd = jnp.concatenate([w_in[:, :qw][:, perm], w_in[:, qw + kvw:]], 1).T.astype(BF16)
    wscale = C_HEAD_DIM ** -0.5 * math.log2(math.e)
    qT, k, vT = _odd_inproj(y_all, n_c, b, s, sc1, sh1, lat_row, wk_odd, wt_odd, cos_l, sin_l,
                            (cos_l * wscale).T, (sin_l * wscale).T, TM_PROJ)
    _, kc, vcT = _odd_inproj(y_all, 0, b, lc, sc1, sh1, ctx_rowf, wk_odd, wt_odd, one_c, zero_c,
                             (one_c * wscale).T, zero_c.T, lc)
    sink = c_sink[0][perm[::C_HEAD_DIM] // C_HEAD_DIM] * math.log2(math.e)
    sink = jnp.repeat(sink, TQ_WIN).reshape(1, C_HEADS * TQ_WIN)
    o = _win_attention(qT, k, vT, kc, vcT, sink)
    wa = c_w_out[0][perm].astype(BF16)
    lng0, lnb0 = ln_g[l, 0].reshape(1, d), ln_b[l, 0].reshape(1, d)
    lng1, lnb1 = ln_g[l, 1].reshape(1, d), ln_b[l, 1].reshape(1, d)
    rwT = router_w[l].T.astype(BF16)
    rb = router_b[l].reshape(N_EXPERTS, 1)
    x1a, h2a, idsa, gatesa = _outproj(o, None, wa, None, y_all, n_c, (g1, sc2, sh2), lat_row,
                                      lng0, lnb0, rwT, rb, TM_PROJ, 0, ())
    row_of_block1 = lambda i: i // spb
    y = _moe(h2a, idsa, gatesa, x1a, g2, row_of_block1, lng1, lnb1, w_gu, b_gu, w_down, b_down, l)
    return y.reshape(b, s, d)
```

```python
import functools
import math

import jax
import jax.numpy as jnp
from jax import lax
from jax.experimental import pallas as pl
from jax.experimental.pallas import tpu as pltpu

F32 = jnp.float32
BF16 = jnp.bfloat16
I32 = jnp.int32

D_MODEL = 1024
DEPTH = 2
GRID_W = 64
ROPE_DIM = 64
ROPE_BASE = 10000.0
A_HEADS = 4
A_QK_DIM = 64
A_V_DIM = 128
A_WIDTH = 512
B_WIDTH = 512
C_HEADS = 16
C_KV_HEADS = 2
C_HEAD_DIM = 64
C_WINDOW = 128
N_EXPERTS = 32
TOP_K = 4
D_FF = 1024
SWIGLU_ALPHA = 1.702
SWIGLU_LIMIT = 7.0
LN_EPS = 1e-5
RMS_EPS = 1e-5
NEG_INF = -1e30
DN_ALPHA = (2 * DEPTH) ** 0.25

LANES = 128
SUBLANES = 8
VMEM_LIMIT = 56 * 1024 * 1024

TM_PROJ = 512
TQ_DIFF = 512
TK_DIFF = 512
TQ_WIN = 128
MOE_ROWS = 512
T_RANK = 1024
T_DISPATCH = 1024
T_COMBINE = 512


def _cparams(sem):
    return pltpu.CompilerParams(dimension_semantics=sem, vmem_limit_bytes=VMEM_LIMIT)


ROW_TILES = D_MODEL // LANES
assert ROW_TILES == SUBLANES


def _load_row_tiles(ref, rows):
    return jnp.concatenate([ref[pl.ds(j, rows, stride=ROW_TILES), :] for j in range(ROW_TILES)], axis=1)


def _store_row_tiles(ref, val):
    rows = val.shape[0]
    for j in range(ROW_TILES):
        ref[pl.ds(j, rows, stride=ROW_TILES), :] = val[:, j * LANES:(j + 1) * LANES]


def _mod_kernel(c_ref, w_ref, b_ref, o_ref):
    c = c_ref[...]
    s = c * jax.nn.sigmoid(c)
    o_ref[...] = jnp.dot(s.astype(BF16), w_ref[...].astype(BF16),
                         preferred_element_type=F32) + b_ref[...]


def _modulation(cvec, ada_w, ada_b):
    d = D_MODEL
    tn = 1536
    return pl.pallas_call(
        _mod_kernel,
        grid=(DEPTH, 6 * d // tn),
        in_specs=[pl.BlockSpec((8, d), lambda l, j: (0, 0)),
                  pl.BlockSpec((None, d, tn), lambda l, j: (l, 0, j)),
                  pl.BlockSpec((None, 1, tn), lambda l, j: (l, 0, j))],
        out_specs=pl.BlockSpec((None, 8, tn), lambda l, j: (l, 0, j)),
        out_shape=jax.ShapeDtypeStruct((DEPTH, 8, 6 * d), F32),
        compiler_params=_cparams(("arbitrary", "arbitrary")),
        name="modulation",
    )(cvec, ada_w, ada_b.reshape(DEPTH, 1, 6 * d))


def _swap16_lanes(t):
    lane = lax.broadcasted_iota(I32, t.shape, 1)
    first = (lane % 32) < 16
    return jnp.where(first, pltpu.roll(t, LANES - 16, 1), pltpu.roll(t, 16, 1))


def _swap16_rows(a):
    pieces = []
    for i in range(0, a.shape[0], 32):
        pieces.append(a[i + 16:i + 32])
        pieces.append(a[i:i + 16])
    return jnp.concatenate(pieces, 0)


def _rope_tables(s):
    rows = jnp.repeat(jnp.arange(s // GRID_W, dtype=F32), GRID_W)
    cols = jnp.tile(jnp.arange(GRID_W, dtype=F32), s // GRID_W)
    axis_dim = ROPE_DIM // 2
    inv = ROPE_BASE ** (-jnp.arange(0, axis_dim, 2, dtype=F32) / axis_dim)
    ar, ac = rows[:, None] * inv, cols[:, None] * inv
    cr, sr, cc, sc = jnp.cos(ar), jnp.sin(ar), jnp.cos(ac), jnp.sin(ac)
    cos64 = jnp.concatenate([cr, cr, cc, cc], -1)
    sin64 = jnp.concatenate([-sr, sr, -sc, sc], -1)
    return cos64, sin64


def _even_inproj_kernel(x_ref, sc_ref, sh_ref, wn_ref, wt_ref, ck_ref, sk_ref, cq_ref, sq_ref,
                        qT_ref, k_ref, vT_ref, bg_ref, u_ref):
    tm = x_ref.shape[0]
    h = (x_ref[...] * (1.0 + sc_ref[...]) + sh_ref[...]).astype(BF16)
    yn = jnp.dot(h, wn_ref[...], preferred_element_type=F32)
    ck = ck_ref[...]
    sk = sk_ref[...]
    pieces = []
    for j in range(A_HEADS):
        t = yn[:, j * LANES:(j + 1) * LANES]
        pieces.append(t * ck + _swap16_lanes(t) * sk)
    k_ref[...] = jnp.concatenate(pieces, 1).astype(BF16)
    bg_ref[...] = yn[:, A_WIDTH:A_WIDTH + B_WIDTH]
    u_ref[...] = yn[:, A_WIDTH + B_WIDTH:A_WIDTH + 2 * B_WIDTH] * yn[:, A_WIDTH + 2 * B_WIDTH:]
    yt = lax.dot_general(wt_ref[...], h, (((1,), (1,)), ((), ())),
                         preferred_element_type=F32)
    qt = yt[:A_WIDTH]
    cq = jnp.concatenate([cq_ref[...]] * A_HEADS, 0)
    sq = jnp.concatenate([sq_ref[...]] * A_HEADS, 0)
    qT_ref[...] = (qt * cq + _swap16_rows(qt) * sq).astype(BF16).reshape(A_HEADS, A_V_DIM, tm)
    vT_ref[...] = yt[A_WIDTH:].astype(BF16).reshape(A_HEADS, A_V_DIM, tm)


def _even_inproj(x, x_off, b, s, sc, sh, mod_row, wn, wt, ck, sk, cq, sq, tm):
    d = x.shape[1]
    nb = s // tm
    xo = x_off // tm
    tk = min(s, TK_DIFF)
    per_chunk = tk // tm
    out_shape = (
        jax.ShapeDtypeStruct((b, A_HEADS, A_V_DIM, s), BF16),
        jax.ShapeDtypeStruct((b, s, A_WIDTH), BF16),
        jax.ShapeDtypeStruct((b, A_HEADS, s // tk, A_V_DIM, tk), BF16),
        jax.ShapeDtypeStruct((b, s, B_WIDTH), F32),
        jax.ShapeDtypeStruct((b, s, B_WIDTH), F32),
    )
    return pl.pallas_call(
        _even_inproj_kernel,
        grid=(b, nb),
        in_specs=[
            pl.BlockSpec((tm, d), lambda bi, i: (xo + bi * nb + i, 0)),
            pl.BlockSpec((None, 1, d), lambda bi, i: (mod_row(bi), 0, 0)),
            pl.BlockSpec((None, 1, d), lambda bi, i: (mod_row(bi), 0, 0)),
            pl.BlockSpec(wn.shape, lambda bi, i: (0, 0)),
            pl.BlockSpec(wt.shape, lambda bi, i: (0, 0)),
            pl.BlockSpec((tm, LANES), lambda bi, i: (i, 0)),
            pl.BlockSpec((tm, LANES), lambda bi, i: (i, 0)),
            pl.BlockSpec((LANES, tm), lambda bi, i: (0, i)),
            pl.BlockSpec((LANES, tm), lambda bi, i: (0, i)),
        ],
        out_specs=(
            pl.BlockSpec((None, A_HEADS, A_V_DIM, tm), lambda bi, i: (bi, 0, 0, i)),
            pl.BlockSpec((None, tm, A_WIDTH), lambda bi, i: (bi, i, 0)),
            pl.BlockSpec((None, A_HEADS, None, A_V_DIM, tm),
                         lambda bi, i: (bi, 0, i // per_chunk, 0, i % per_chunk)),
            pl.BlockSpec((None, tm, B_WIDTH), lambda bi, i: (bi, i, 0)),
            pl.BlockSpec((None, tm, B_WIDTH), lambda bi, i: (bi, i, 0)),
        ),
        out_shape=out_shape,
        compiler_params=_cparams(("parallel", "arbitrary")),
        name="even_inproj",
    )(x, sc, sh, wn, wt, ck, sk, cq, sq)


def _diff_attn_kernel(*refs, n_chunks, lam_init, with_lat):
    if with_lat:
        (qT_ref, k_ref, vT_ref, kc_ref, vTc_ref, lamp_ref, g_ref, o_ref,
         acc1, acc2, sa1, sa2, sb1, sb2) = refs
    else:
        qT_ref, kc_ref, vTc_ref, lamp_ref, g_ref, o_ref, acc1, acc2 = refs
    tq = qT_ref.shape[1]
    qT = qT_ref[...]
    row = lax.broadcasted_iota(I32, qT.shape, 0)
    zero = jnp.zeros_like(qT)
    q1 = jnp.where(row < A_QK_DIM, qT, zero)
    q2 = jnp.where(row >= A_QK_DIM, qT, zero)
    acc1[...] = jnp.zeros_like(acc1)
    acc2[...] = jnp.zeros_like(acc2)

    def one_map(s, vTc, m, l, acc):
        m_new = jnp.maximum(m, jnp.max(s, axis=0, keepdims=True))
        alpha = jnp.exp2(m - m_new)
        p = jnp.exp2(s - m_new)
        l_new = alpha * l + jnp.sum(p, axis=0, keepdims=True)
        acc[...] = alpha * acc[...] + jnp.dot(vTc, p.astype(BF16), preferred_element_type=F32)
        return m_new, l_new

    def softmax_pv(s1, s2, vTc, carry):
        m1, l1, m2, l2 = carry
        m1, l1 = one_map(s1, vTc, m1, l1, acc1)
        m2, l2 = one_map(s2, vTc, m2, l2, acc2)
        return m1, l1, m2, l2

    neg = jnp.full((1, tq), -jnp.inf, F32)
    zer = jnp.zeros((1, tq), F32)
    kc = kc_ref[...]
    carry = softmax_pv(jnp.dot(kc, q1, preferred_element_type=F32),
                       jnp.dot(kc, q2, preferred_element_type=F32), vTc_ref[...], (neg, zer, neg, zer))
    if with_lat:
        tk = vT_ref.shape[2]

        def scores(c, s1_ref, s2_ref):
            kk = k_ref[pl.ds(pl.multiple_of(c * tk, tk), tk), :]
            s1_ref[...] = jnp.dot(kk, q1, preferred_element_type=F32)
            s2_ref[...] = jnp.dot(kk, q2, preferred_element_type=F32)

        scores(0, sa1, sa2)
        bufs = ((sa1, sa2), (sb1, sb2))

        def steps(c0, count, carry, last):
            for j in range(count):
                cur, nxt = bufs[j % 2], bufs[(j + 1) % 2]
                if not (last and j == count - 1):
                    scores(c0 + j + 1, *nxt)
                carry = softmax_pv(cur[0][...], cur[1][...], vT_ref[c0 + j], carry)
            return carry

        unroll = 4
        n_body = max(n_chunks // unroll - 1, 0)
        carry = lax.fori_loop(0, n_body, lambda j, cr: steps(j * unroll, unroll, cr, False), carry)
        carry = steps(n_body * unroll, n_chunks - n_body * unroll, carry, True)
    m1, l1, m2, l2 = carry

    lf = lamp_ref[...]
    e1 = jnp.exp(jnp.sum(lf[0:1] * lf[1:2], axis=1, keepdims=True))
    e2 = jnp.exp(jnp.sum(lf[2:3] * lf[3:4], axis=1, keepdims=True))
    lam = e1 - e2 + lam_init
    o = acc1[...] / l1 - lam * (acc2[...] / l2)
    o = o * lax.rsqrt(jnp.mean(o * o, axis=0, keepdims=True) + RMS_EPS)
    o = o * g_ref[...] * (1.0 - lam_init)
    o_ref[...] = o.T.astype(BF16)


def _diff_attention(qT, k, vT, kc, vTc, lam_p, g_col, lam_init, tq):
    b, _, _, s = qT.shape
    lc = kc.shape[1]
    with_lat = k is not None
    in_specs = [pl.BlockSpec((None, None, A_V_DIM, tq), lambda bi, h, i: (bi, h, 0, i))]
    args = [qT]
    n_chunks = 0
    if with_lat:
        sk = k.shape[1]
        n_chunks, tk = vT.shape[2], vT.shape[4]
        in_specs += [pl.BlockSpec((None, sk, LANES), lambda bi, h, i: (bi, 0, h)),
                     pl.BlockSpec((None, None, n_chunks, A_V_DIM, tk), lambda bi, h, i: (bi, h, 0, 0, 0))]
        args += [k, vT]
    in_specs += [pl.BlockSpec((None, lc, LANES), lambda bi, h, i: (bi, 0, h)),
                 pl.BlockSpec((None, None, None, A_V_DIM, lc), lambda bi, h, i: (bi, h, 0, 0, 0)),
                 pl.BlockSpec(lam_p.shape, lambda bi, h, i: (0, 0)),
                 pl.BlockSpec(g_col.shape, lambda bi, h, i: (0, 0))]
    args += [kc, vTc, lam_p, g_col]
    scratch = [pltpu.VMEM((A_V_DIM, tq), F32), pltpu.VMEM((A_V_DIM, tq), F32)]
    if with_lat:
        scratch += [pltpu.VMEM((tk, tq), F32)] * 4
    return pl.pallas_call(
        functools.partial(_diff_attn_kernel, n_chunks=n_chunks, lam_init=lam_init, with_lat=with_lat),
        grid=(b, A_HEADS, s // tq),
        in_specs=in_specs,
        out_specs=pl.BlockSpec((None, tq, A_V_DIM), lambda bi, h, i: (bi, i, h)),
        out_shape=jax.ShapeDtypeStruct((b, s, A_WIDTH), BF16),
        scratch_shapes=scratch,
        compiler_params=_cparams(("parallel", "parallel", "arbitrary")),
        name="diff_attention_lat" if with_lat else "diff_attention_ctx",
    )(*args)


def _layer_norm(z, g, b):
    mu = jnp.mean(z, axis=-1, keepdims=True)
    zc = z - mu
    var = jnp.mean(zc * zc, axis=-1, keepdims=True)
    return zc * lax.rsqrt(var + LN_EPS) * g + b


def _top4_router(h2, rwT_ref, rb_ref, ids_ref, gates_ref):
    tm = h2.shape[0]
    lt = lax.dot_general(rwT_ref[...], h2.astype(BF16), (((1,), (1,)), ((), ())),
                         preferred_element_type=F32) + rb_ref[...]
    eidx = lax.broadcasted_iota(I32, lt.shape, 0)
    vals, ids = [], []
    for _ in range(TOP_K):
        mx = jnp.max(lt, axis=0, keepdims=True)
        idx = jnp.min(jnp.where(lt == mx, eidx, N_EXPERTS), axis=0, keepdims=True)
        vals.append(mx)
        ids.append(idx)
        lt = jnp.where(eidx == idx, -jnp.inf, lt)
    es = [jnp.exp(v - vals[0]) for v in vals]
    den = es[0] + es[1] + es[2] + es[3]
    ids_ref[...] = jnp.concatenate(ids, 0)
    gates_ref[...] = jnp.concatenate([e / den for e in es], 0)
    del tm


def _outproj_kernel(*refs, has_conv):
    if has_conv:
        (a_ref, bg_ref, u_ref, up_ref, un_ref, cw_ref, wa_ref, wb_ref, x_ref, g1_ref, lng_ref, lnb_ref,
         sc2_ref, sh2_ref, rwT_ref, rb_ref, x1_ref, h2_ref, ids_ref, gates_ref) = refs
    else:
        (a_ref, wa_ref, x_ref, g1_ref, lng_ref, lnb_ref,
         sc2_ref, sh2_ref, rwT_ref, rb_ref, x1_ref, h2_ref, ids_ref, gates_ref) = refs
    y = jnp.dot(a_ref[...], wa_ref[...], preferred_element_type=F32)
    if has_conv:
        tm = u_ref.shape[0]
        i = pl.program_id(1)
        nb = pl.num_programs(1)
        u = u_ref[...]
        prev = jnp.where(i > 0, up_ref[SUBLANES - 1:SUBLANES, :], 0.0)
        nxt = jnp.where(i < nb - 1, un_ref[0:1, :], 0.0)
        r = lax.broadcasted_iota(I32, u.shape, 0)
        um1 = jnp.where(r == 0, prev, pltpu.roll(u, 1, 0))
        up1 = jnp.where(r == tm - 1, nxt, pltpu.roll(u, tm - 1, 0))
        cw = cw_ref[...]
        conv = um1 * cw[0:1] + u * cw[1:2] + up1 * cw[2:3]
        b_mix = (bg_ref[...] * conv).astype(BF16)
        y = y + jnp.dot(b_mix, wb_ref[...], preferred_element_type=F32)
    x1 = _layer_norm(DN_ALPHA * x_ref[...] + g1_ref[...] * y, lng_ref[...], lnb_ref[...])
    x1_ref[...] = x1
    h2 = x1 * (1.0 + sc2_ref[...]) + sh2_ref[...]
    _store_row_tiles(h2_ref, h2)
    _top4_router(h2, rwT_ref, rb_ref, ids_ref, gates_ref)


def _outproj(a, conv_in, wa, wb, x, x_off, mod_vecs, mod_row, ln_g, ln_b, rwT, rb, tm, tok_off, prev_out):
    b, s = a.shape[0], a.shape[1]
    d = x.shape[1]
    nb = s // tm
    g1, sc2, sh2 = mod_vecs
    has_conv = conv_in is not None
    n_tok = prev_out[0].shape[0] if prev_out else b * s
    ob = tok_off // tm
    xo = x_off // tm

    def row_spec(width):
        return pl.BlockSpec((None, tm, width), lambda bi, i: (bi, i, 0))

    def vec_spec():
        return pl.BlockSpec((None, 1, d), lambda bi, i: (mod_row(bi), 0, 0))

    def full(arr):
        return pl.BlockSpec(arr.shape, lambda bi, i: (0,) * arr.ndim)

    in_specs = [row_spec(a.shape[-1])]
    args = [a]
    if has_conv:
        bg, u, cw = conv_in
        s8 = s // SUBLANES
        t8 = tm // SUBLANES
        in_specs += [row_spec(B_WIDTH), row_spec(B_WIDTH),
                     pl.BlockSpec((None, SUBLANES, B_WIDTH),
                                  lambda bi, i: (bi, jnp.maximum(i * t8 - 1, 0), 0)),
                     pl.BlockSpec((None, SUBLANES, B_WIDTH),
                                  lambda bi, i: (bi, jnp.minimum((i + 1) * t8, s8 - 1), 0)),
                     full(cw), full(wa), full(wb)]
        args += [bg, u, u, u, cw, wa, wb]
    else:
        in_specs += [full(wa)]
        args += [wa]
    in_specs += [pl.BlockSpec((tm, d), lambda bi, i: (xo + bi * nb + i, 0)),
                 vec_spec(), full(ln_g), full(ln_b), vec_spec(), vec_spec(), full(rwT), full(rb)]
    args += [x, g1, ln_g, ln_b, sc2, sh2, rwT, rb]
    n_in = len(args)
    n_alias = len(prev_out)
    in_specs += [pl.BlockSpec(memory_space=pl.ANY)] * n_alias
    args += list(prev_out)
    out_specs = (
        pl.BlockSpec((tm, d), lambda bi, i: (ob + bi * nb + i, 0)),
        pl.BlockSpec((tm * ROW_TILES, LANES), lambda bi, i: (ob + bi * nb + i, 0)),
        pl.BlockSpec((TOP_K, tm), lambda bi, i: (0, ob + bi * nb + i)),
        pl.BlockSpec((TOP_K, tm), lambda bi, i: (0, ob + bi * nb + i)),
    )
    out_shape = (
        jax.ShapeDtypeStruct((n_tok, d), F32),
        jax.ShapeDtypeStruct((n_tok * ROW_TILES, LANES), F32),
        jax.ShapeDtypeStruct((TOP_K, n_tok), I32),
        jax.ShapeDtypeStruct((TOP_K, n_tok), F32),
    )

    def kern(*refs):
        refs = refs[:n_in] + refs[n_in + n_alias:]
        _outproj_kernel(*refs, has_conv=has_conv)

    return pl.pallas_call(
        kern,
        grid=(b, nb),
        in_specs=in_specs,
        out_specs=out_specs,
        out_shape=out_shape,
        input_output_aliases={n_in + j: j for j in range(n_alias)},
        compiler_params=_cparams(("parallel", "arbitrary")),
        name="outproj_conv" if has_conv else "outproj",
    )(*args)


def _rank_kernel(ids_ref, rank_ref, cnt_ref, run_ref):
    i = pl.program_id(0)
    tr = ids_ref.shape[1]

    @pl.when(i == 0)
    def _():
        run_ref[...] = jnp.zeros_like(run_ref)

    ids = ids_ref[...]
    eidx = lax.broadcasted_iota(I32, (N_EXPERTS, tr), 0)
    hits = [eidx == ids[k:k + 1] for k in range(TOP_K)]
    member = (hits[0] | hits[1] | hits[2] | hits[3]).astype(F32)
    r = lax.broadcasted_iota(I32, (tr, tr), 0)
    c = lax.broadcasted_iota(I32, (tr, tr), 1)
    upper = (r < c).astype(BF16)
    prefix = jnp.dot(member.astype(BF16), upper, preferred_element_type=F32)
    base = run_ref[:, 0:1] + prefix
    ranks = [jnp.sum(jnp.where(hits[k], base, 0.0), axis=0, keepdims=True) for k in range(TOP_K)]
    rank_ref[...] = jnp.concatenate(ranks, 0).astype(I32)
    run_ref[...] = run_ref[...] + jnp.sum(member, axis=1, keepdims=True)
    cnt_ref[...] = run_ref[...]


def _expert_ranks(ids):
    n = ids.shape[1]
    return pl.pallas_call(
        _rank_kernel,
        grid=(n // T_RANK,),
        in_specs=[pl.BlockSpec((TOP_K, T_RANK), lambda i: (0, i))],
        out_specs=(pl.BlockSpec((TOP_K, T_RANK), lambda i: (0, i)),
                   pl.BlockSpec((N_EXPERTS, LANES), lambda i: (0, 0))),
        out_shape=(jax.ShapeDtypeStruct((TOP_K, n), I32),
                   jax.ShapeDtypeStruct((N_EXPERTS, LANES), F32)),
        scratch_shapes=[pltpu.VMEM((N_EXPERTS, LANES), F32)],
        compiler_params=_cparams(("arbitrary",)),
        name="expert_ranks",
    )(ids)


def _dest_kernel(ps_ref, ids_ref, rank_ref, dest_ref):
    ids = ids_ref[...]
    acc = rank_ref[...]
    for e in range(N_EXPERTS):
        acc = acc + jnp.where(ids == e, ps_ref[e], 0)
    dest_ref[...] = acc


def _destinations(pad_start, ids, rank):
    n = ids.shape[1]
    tn = next(t for t in (16384, 11264, 8192, 4096, 2048, 1024, 512) if n % t == 0)
    return pl.pallas_call(
        _dest_kernel,
        grid_spec=pltpu.PrefetchScalarGridSpec(
            num_scalar_prefetch=1,
            grid=(n // tn,),
            in_specs=[pl.BlockSpec((TOP_K, tn), lambda i, ps: (0, i)),
                      pl.BlockSpec((TOP_K, tn), lambda i, ps: (0, i))],
            out_specs=pl.BlockSpec((TOP_K, tn), lambda i, ps: (0, i))),
        out_shape=jax.ShapeDtypeStruct((TOP_K, n), I32),
        compiler_params=_cparams(("arbitrary",)),
        name="destinations",
    )(pad_start, ids, rank)


_PAD_BITS = tuple(1 << j for j in reversed(range(MOE_ROWS.bit_length() - 1)))


def _tile_rows(start, n):
    return pl.ds(pl.multiple_of(start * ROW_TILES, ROW_TILES), n * ROW_TILES)


def _dispatch_kernel(pp_ref, np_ref, nu_ref, dest_ref, h_ref, xs_hbm, zeros_ref, sem, zsem):
    td = dest_ref.shape[1]
    n_blocks = xs_hbm.shape[0] // (MOE_ROWS * ROW_TILES)

    def pad_copy(e, bit):
        npad = np_ref[e]
        off = pp_ref[e] + (npad & (-2 * bit))
        return (npad & bit) != 0, pltpu.make_async_copy(zeros_ref.at[_tile_rows(0, bit)],
                                                        xs_hbm.at[_tile_rows(off, bit)], zsem)

    def blk_copy(blk):
        return pltpu.make_async_copy(zeros_ref, xs_hbm.at[_tile_rows(blk * MOE_ROWS, MOE_ROWS)], zsem)

    def pad_all(wait):
        def per_expert(e, carry):
            for bit in _PAD_BITS:
                cond, cp = pad_copy(e, bit)

                @pl.when(cond)
                def _():
                    cp.wait() if wait else cp.start()
            return carry

        def per_block(blk, carry):
            cp = blk_copy(blk)
            cp.wait() if wait else cp.start()
            return carry

        lax.fori_loop(0, N_EXPERTS, per_expert, 0)
        lax.fori_loop(nu_ref[0], n_blocks, per_block, 0)

    @pl.when(pl.program_id(0) == 0)
    def _():
        zeros_ref[...] = jnp.zeros_like(zeros_ref)
        pad_all(False)
        pad_all(True)

    def body(r, carry):
        for k in range(TOP_K):
            pltpu.make_async_copy(h_ref.at[_tile_rows(r, 1)], xs_hbm.at[_tile_rows(dest_ref[k, r], 1)],
                                  sem).start(priority=k % 2)
        return carry

    lax.fori_loop(0, td, body, 0, unroll=8)
    for k in range(TOP_K):
        pltpu.make_async_copy(h_ref, xs_hbm.at[_tile_rows(0, td)], sem).wait()


def _dispatch(pad_pos, n_pad, n_used, dest, h2, n_rows):
    n = h2.shape[0] // ROW_TILES
    return pl.pallas_call(
        _dispatch_kernel,
        grid_spec=pltpu.PrefetchScalarGridSpec(
            num_scalar_prefetch=3,
            grid=(n // T_DISPATCH,),
            in_specs=[pl.BlockSpec((TOP_K, T_DISPATCH), lambda i, *_: (0, i), memory_space=pltpu.SMEM),
                      pl.BlockSpec((T_DISPATCH * ROW_TILES, LANES), lambda i, *_: (i, 0))],
            out_specs=pl.BlockSpec(memory_space=pl.ANY),
            scratch_shapes=[pltpu.VMEM((MOE_ROWS * ROW_TILES, LANES), h2.dtype),
                            pltpu.SemaphoreType.DMA(()), pltpu.SemaphoreType.DMA(())]),
        out_shape=jax.ShapeDtypeStruct((n_rows * ROW_TILES, LANES), h2.dtype),
        compiler_params=pltpu.CompilerParams(dimension_semantics=("arbitrary",),
                                             has_side_effects=True, vmem_limit_bytes=VMEM_LIMIT),
        name="dispatch",
    )(pad_pos, n_pad, n_used, dest, h2)


def _expert_kernel(be_ref, nu_ref, xs_ref, wgu_ref, bgu_ref, wd_ref, bd_ref, ys_ref, wgu_bf, wd_bf):
    i = pl.program_id(0)
    nu = nu_ref[0]
    last = jnp.minimum(i, nu - 1)
    new_expert = (i == 0) | (be_ref[last] != be_ref[jnp.maximum(last - 1, 0)])

    @pl.when((i < nu) & new_expert)
    def _():
        wgu_bf[...] = wgu_ref[...].astype(BF16)
        wd_bf[...] = wd_ref[...].astype(BF16)

    @pl.when(i < nu)
    def _():
        x = _load_row_tiles(xs_ref, MOE_ROWS).astype(BF16)
        gu = jnp.dot(x, wgu_bf[...], preferred_element_type=F32) + bgu_ref[...]
        gate = jnp.minimum(gu[:, :D_FF], SWIGLU_LIMIT)
        lin = jnp.clip(gu[:, D_FF:], -SWIGLU_LIMIT, SWIGLU_LIMIT)
        act = gate * jax.nn.sigmoid(SWIGLU_ALPHA * gate) * (lin + 1.0)
        _store_row_tiles(ys_ref, jnp.dot(act.astype(BF16), wd_bf[...], preferred_element_type=F32)
                         + bd_ref[...])

    @pl.when(i >= nu_ref[0])
    def _():
        ys_ref[...] = jnp.zeros_like(ys_ref)


def _expert_mlp(blk_e, n_used, xs, w_gu, b_gu, w_down, b_down, layer):
    d = D_MODEL
    n_rows = xs.shape[0] // ROW_TILES
    n_blocks = n_rows // MOE_ROWS
    row_block = (MOE_ROWS * ROW_TILES, LANES)

    def blk(i, nu):
        return jnp.minimum(i, nu[0] - 1)

    return pl.pallas_call(
        _expert_kernel,
        grid_spec=pltpu.PrefetchScalarGridSpec(
            num_scalar_prefetch=2,
            grid=(n_blocks,),
            in_specs=[
                pl.BlockSpec(row_block, lambda i, be, nu: (blk(i, nu), 0)),
                pl.BlockSpec((None, None, d, 2 * D_FF), lambda i, be, nu: (layer, be[blk(i, nu)], 0, 0)),
                pl.BlockSpec((None, None, 1, 2 * D_FF), lambda i, be, nu: (layer, be[blk(i, nu)], 0, 0)),
                pl.BlockSpec((None, None, D_FF, d), lambda i, be, nu: (layer, be[blk(i, nu)], 0, 0)),
                pl.BlockSpec((None, None, 1, d), lambda i, be, nu: (layer, be[blk(i, nu)], 0, 0)),
            ],
            out_specs=pl.BlockSpec(row_block, lambda i, be, nu: (i, 0)),
            scratch_shapes=[pltpu.VMEM((d, 2 * D_FF), BF16), pltpu.VMEM((D_FF, d), BF16)]),
        out_shape=jax.ShapeDtypeStruct(xs.shape, F32),
        compiler_params=_cparams(("arbitrary",)),
        name="expert_mlp",
    )(blk_e, n_used, xs, w_gu, b_gu.reshape(DEPTH, N_EXPERTS, 1, 2 * D_FF), w_down,
      b_down.reshape(DEPTH, N_EXPERTS, 1, d))


COMBINE_PIECE = 32


def _combine_kernel(dest_ref, dnext_ref, gates_ref, ys_hbm, x1_ref, g2_ref, lng_ref, lnb_ref, out_ref,
                    buf, gt_ref, sem):
    tc = dest_ref.shape[1]
    i = pl.program_id(0)
    n = pl.num_programs(0)
    slot = i % 2

    def start_rows(d_ref, slot, r0, count):
        for r in range(count):
            for k in range(TOP_K):
                pltpu.make_async_copy(ys_hbm.at[_tile_rows(d_ref[k, r0 + r], 1)],
                                      buf.at[slot, k, _tile_rows(r0 + r, 1)],
                                      sem.at[slot]).start(priority=k % 2)

    def wait_slot(slot):
        for k in range(TOP_K):
            pltpu.make_async_copy(ys_hbm.at[_tile_rows(0, tc)], buf.at[slot, k], sem.at[slot]).wait()

    @pl.when(i == 0)
    def _():
        lax.fori_loop(0, tc // SUBLANES,
                      lambda j, c: (start_rows(dest_ref, 0, j * SUBLANES, SUBLANES), c)[1], 0)

    wait_slot(slot)
    gpad = jnp.concatenate([gates_ref[...], jnp.zeros((LANES - TOP_K, tc), F32)], 0)
    gt_ref[...] = gpad.T

    for r0 in range(0, tc, COMBINE_PIECE):
        rows = [jnp.concatenate(
            [buf[slot, k, pl.ds(r0 * ROW_TILES + j, COMBINE_PIECE, stride=ROW_TILES), :]
             for j in range(ROW_TILES)], axis=1) for k in range(TOP_K)]
        gt = gt_ref[pl.ds(r0, COMBINE_PIECE), :]
        x1 = x1_ref[pl.ds(r0, COMBINE_PIECE), :]
        start_rows(dnext_ref, 1 - slot, r0, COMBINE_PIECE)
        f = rows[0] * gt[:, 0:1]
        for k in range(1, TOP_K):
            f = f + rows[k] * gt[:, k:k + 1]
        z = DN_ALPHA * x1 + g2_ref[...] * f
        out_ref[pl.ds(r0, COMBINE_PIECE), :] = _layer_norm(z, lng_ref[...], lnb_ref[...])

    @pl.when(i == n - 1)
    def _():
        wait_slot(1 - slot)


def _combine(dest, gates, ys, x1, g2, row_of_block, ln_g, ln_b):
    n, d = x1.shape
    tc = T_COMBINE
    nsteps = n // tc
    return pl.pallas_call(
        _combine_kernel,
        grid=(nsteps,),
        in_specs=[pl.BlockSpec((TOP_K, tc), lambda i: (0, i), memory_space=pltpu.SMEM),
                  pl.BlockSpec((TOP_K, tc), lambda i: (0, jnp.minimum(i + 1, nsteps - 1)),
                               memory_space=pltpu.SMEM),
                  pl.BlockSpec((TOP_K, tc), lambda i: (0, i)),
                  pl.BlockSpec(memory_space=pl.ANY),
                  pl.BlockSpec((tc, d), lambda i: (i, 0)),
                  pl.BlockSpec((None, 1, d), lambda i: (row_of_block(i), 0, 0)),
                  pl.BlockSpec(ln_g.shape, lambda i: (0, 0)),
                  pl.BlockSpec(ln_b.shape, lambda i: (0, 0))],
        out_specs=pl.BlockSpec((tc, d), lambda i: (i, 0)),
        out_shape=jax.ShapeDtypeStruct((n, d), F32),
        scratch_shapes=[pltpu.VMEM((2, TOP_K, tc * ROW_TILES, LANES), F32), pltpu.VMEM((tc, LANES), F32),
                        pltpu.SemaphoreType.DMA((2,))],
        compiler_params=_cparams(("arbitrary",)),
        name="combine",
    )(dest, dest, gates, ys, x1, g2, ln_g, ln_b)


def _moe(h2, ids, gates, x1, g2, row_of_block, ln_g, ln_b, w_gu, b_gu, w_down, b_down, layer):
    n = ids.shape[1]
    nk = n * TOP_K
    n_blocks = -(-nk // MOE_ROWS) + N_EXPERTS
    rank, cnt = _expert_ranks(ids)
    counts = cnt[:, 0].astype(I32)
    padded = (counts + MOE_ROWS - 1) // MOE_ROWS * MOE_ROWS
    pad_end = jnp.cumsum(padded)
    pad_start = (pad_end - padded).astype(I32)
    blk_start = jnp.arange(n_blocks, dtype=I32) * MOE_ROWS
    blk_e = jnp.minimum(jnp.sum((pad_end[None, :] <= blk_start[:, None]).astype(I32), axis=1),
                        N_EXPERTS - 1).astype(I32)
    n_used = (pad_end[-1:] // MOE_ROWS).astype(I32)
    dest = _destinations(pad_start, ids, rank)
    xs = _dispatch(pad_start + counts, padded - counts, n_used, dest, h2, n_blocks * MOE_ROWS)
    ys = _expert_mlp(blk_e, n_used, xs, w_gu, b_gu, w_down, b_down, layer)
    return _combine(dest, gates, ys, x1, g2, row_of_block, ln_g, ln_b)


def _odd_inproj_kernel(x_ref, sc_ref, sh_ref, wk_ref, wt_ref, ck_ref, sk_ref, cq_ref, sq_ref,
                       qT_ref, k_ref, vT_ref):
    h = (x_ref[...] * (1.0 + sc_ref[...]) + sh_ref[...]).astype(BF16)
    kk = jnp.dot(h, wk_ref[...], preferred_element_type=F32)
    k_ref[...] = (kk * ck_ref[...] + _swap16_lanes(kk) * sk_ref[...]).astype(BF16)
    yt = lax.dot_general(wt_ref[...], h, (((1,), (1,)), ((), ())),
                         preferred_element_type=F32)
    qw = C_HEADS * C_HEAD_DIM
    qt = yt[:qw]
    n_pairs = C_HEADS // C_KV_HEADS
    cq = jnp.concatenate([cq_ref[...]] * n_pairs, 0)
    sq = jnp.concatenate([sq_ref[...]] * n_pairs, 0)
    qT_ref[...] = (qt * cq + _swap16_rows(qt) * sq).astype(BF16)
    vT_ref[...] = yt[qw:].astype(BF16)


def _odd_inproj(x, x_off, b, s, sc, sh, mod_row, wk, wt, ck, sk, cq, sq, tm):
    d = x.shape[1]
    qw = C_HEADS * C_HEAD_DIM
    nb = s // tm
    xo = x_off // tm
    return pl.pallas_call(
        _odd_inproj_kernel,
        grid=(b, nb),
        in_specs=[pl.BlockSpec((tm, d), lambda bi, i: (xo + bi * nb + i, 0)),
                  pl.BlockSpec((None, 1, d), lambda bi, i: (mod_row(bi), 0, 0)),
                  pl.BlockSpec((None, 1, d), lambda bi, i: (mod_row(bi), 0, 0)),
                  pl.BlockSpec(wk.shape, lambda bi, i: (0, 0)),
                  pl.BlockSpec(wt.shape, lambda bi, i: (0, 0)),
                  pl.BlockSpec((tm, LANES), lambda bi, i: (i, 0)),
                  pl.BlockSpec((tm, LANES), lambda bi, i: (i, 0)),
                  pl.BlockSpec((LANES, tm), lambda bi, i: (0, i)),
                  pl.BlockSpec((LANES, tm), lambda bi, i: (0, i))],
        out_specs=(pl.BlockSpec((None, qw, tm), lambda bi, i: (bi, 0, i)),
                   pl.BlockSpec((None, tm, LANES), lambda bi, i: (bi, i, 0)),
                   pl.BlockSpec((None, LANES, tm), lambda bi, i: (bi, 0, i))),
        out_shape=(jax.ShapeDtypeStruct((b, qw, s), BF16),
                   jax.ShapeDtypeStruct((b, s, LANES), BF16),
                   jax.ShapeDtypeStruct((b, LANES, s), BF16)),
        compiler_params=_cparams(("parallel", "arbitrary")),
        name="odd_inproj",
    )(x, sc, sh, wk, wt, ck, sk, cq, sq)


def _win_attn_kernel(qT_ref, kp_ref, k0_ref, kn_ref, kc_ref, vp_ref, v0_ref, vn_ref, vc_ref, sink_ref,
                     o_ref, p_ref):
    tq = qT_ref.shape[1]
    i = pl.program_id(1)
    nb = pl.num_programs(1)
    n_pairs = C_HEADS // C_KV_HEADS
    qT = qT_ref[...]
    row = lax.broadcasted_iota(I32, (LANES, tq), 0)
    lo = row < C_HEAD_DIM
    zero = jnp.zeros((LANES, tq), BF16)
    cols = []
    for j in range(n_pairs):
        t = qT[j * LANES:(j + 1) * LANES]
        cols.append(jnp.where(lo, t, zero))
        cols.append(jnp.where(lo, zero, t))
    qs = jnp.concatenate(cols, 1)
    kb = jnp.concatenate([kp_ref[...], k0_ref[...], kn_ref[...], kc_ref[...]], 0)
    vbT = jnp.concatenate([vp_ref[...], v0_ref[...], vn_ref[...], vc_ref[...]], 1)
    nk = kb.shape[0]
    s = jnp.dot(kb, qs, preferred_element_type=F32)
    key = lax.broadcasted_iota(I32, (nk, tq), 0)
    qpos = lax.broadcasted_iota(I32, (nk, tq), 1)
    ok = (jnp.abs(key - tq - qpos) <= C_WINDOW)
    ok = ok & ((key >= tq) | (i > 0)) & ((key < 2 * tq) | (i < nb - 1))
    ok = ok | (key >= 3 * tq)
    sink = sink_ref[...]
    dens = []
    for hb in range(C_HEADS):
        sl = slice(hb * tq, (hb + 1) * tq)
        sb = jnp.where(ok, s[:, sl], NEG_INF)
        m = jnp.maximum(jnp.max(sb, axis=0, keepdims=True), sink[:, sl])
        p = jnp.exp2(sb - m)
        dens.append(jnp.sum(p, axis=0, keepdims=True) + jnp.exp2(sink[:, sl] - m))
        p_ref[:, sl] = p.astype(BF16)
    oT = jnp.dot(vbT, p_ref[...], preferred_element_type=F32)
    outs = []
    for j in range(n_pairs):
        a = oT[:, (2 * j) * tq:(2 * j + 1) * tq] / dens[2 * j]
        b = oT[:, (2 * j + 1) * tq:(2 * j + 2) * tq] / dens[2 * j + 1]
        outs.append(jnp.where(lo, a, b).T)
    o_ref[...] = jnp.concatenate(outs, 1).astype(BF16)


def _win_attention(qT, k, vT, kc, vcT, sink):
    b, qw, s = qT.shape
    tq = TQ_WIN
    nb = s // tq
    lc = kc.shape[1]
    nk = 3 * tq + lc

    def kblk(off):
        return pl.BlockSpec((None, tq, LANES), lambda bi, i: (bi, jnp.clip(i + off, 0, nb - 1), 0))

    def vblk(off):
        return pl.BlockSpec((None, LANES, tq), lambda bi, i: (bi, 0, jnp.clip(i + off, 0, nb - 1)))

    return pl.pallas_call(
        _win_attn_kernel,
        grid=(b, nb),
        in_specs=[pl.BlockSpec((None, qw, tq), lambda bi, i: (bi, 0, i)),
                  kblk(-1), kblk(0), kblk(1), pl.BlockSpec((None, lc, LANES), lambda bi, i: (bi, 0, 0)),
                  vblk(-1), vblk(0), vblk(1), pl.BlockSpec((None, LANES, lc), lambda bi, i: (bi, 0, 0)),
                  pl.BlockSpec(sink.shape, lambda bi, i: (0, 0))],
        out_specs=pl.BlockSpec((None, tq, qw), lambda bi, i: (bi, i, 0)),
        out_shape=jax.ShapeDtypeStruct((b, s, qw), BF16),
        scratch_shapes=[pltpu.VMEM((nk, C_HEADS * tq), BF16)],
        compiler_params=_cparams(("parallel", "arbitrary")),
        name="window_attention",
    )(qT, k, k, k, kc, vT, vT, vT, vcT, sink)


def kernel(x, c, ctx, c_ctx, ada_w, ada_b, ln_g, ln_b, ab_w_in, ab_w_out, diff_lambda, diff_subln_g,
           conv_w, c_w_in, c_w_out, c_sink, router_w, router_b, w_gu, b_gu, w_down, b_down):
    b, s, d = x.shape
    lc = ctx.shape[1]
    n_c = b * lc
    ctx_row = b

    cvec = jnp.zeros((8, d), F32).at[:b].set(c).at[b].set(c_ctx)
    mod = _modulation(cvec, ada_w, ada_b)

    def mod_vec(layer, j):
        return mod[layer, :, j * d:(j + 1) * d].reshape(8, 1, d)

    lat_row = lambda bi: bi
    ctx_rowf = lambda bi: ctx_row

    cos64, sin64 = _rope_tables(s)
    cos_l = jnp.concatenate([cos64, cos64], -1)
    sin_l = jnp.concatenate([sin64, sin64], -1)
    one_c = jnp.ones((lc, LANES), F32)
    zero_c = jnp.zeros((lc, LANES), F32)
    qscale = A_QK_DIM ** -0.5 * math.log2(math.e)

    l = 0
    w_in = ab_w_in[0]
    wn = w_in[:, A_WIDTH:].astype(BF16)
    wn = jnp.concatenate([wn[:, :A_WIDTH], wn[:, 2 * A_WIDTH:]], 1)
    wt = jnp.concatenate([w_in[:, :A_WIDTH], w_in[:, 2 * A_WIDTH:3 * A_WIDTH]], 1).T.astype(BF16)
    sc1, sh1, g1 = mod_vec(l, 1), mod_vec(l, 0), mod_vec(l, 2)
    sh2, sc2, g2 = mod_vec(l, 3), mod_vec(l, 4), mod_vec(l, 5)
    lam_init = 0.8 - 0.6 * math.exp(-0.3 * l)

    x_tok = x.reshape(b * s, d)
    ctx_tok = ctx.reshape(n_c, d)
    qT, k, vT, bg, u = _even_inproj(x_tok, 0, b, s, sc1, sh1, lat_row, wn, wt, cos_l, sin_l,
                                    (cos_l * qscale).T, (sin_l * qscale).T, TM_PROJ)
    qTc, kc, vTc, bgc, uc = _even_inproj(ctx_tok, 0, b, lc, sc1, sh1, ctx_rowf, wn, wt, one_c, zero_c,
                                         (one_c * qscale).T, zero_c.T, lc)
    g_col = diff_subln_g[0].reshape(A_V_DIM, 1)
    a_lat = _diff_attention(qT, k, vT, kc, vTc, diff_lambda[0], g_col, lam_init, TQ_DIFF)
    a_ctx = _diff_attention(qTc, None, None, kc, vTc, diff_lambda[0], g_col, lam_init, lc)

    w_out = ab_w_out[0].astype(BF16)
    wa, wb = w_out[:A_WIDTH], w_out[A_WIDTH:]
    lng0, lnb0 = ln_g[l, 0].reshape(1, d), ln_b[l, 0].reshape(1, d)
    lng1, lnb1 = ln_g[l, 1].reshape(1, d), ln_b[l, 1].reshape(1, d)
    rwT = router_w[l].T.astype(BF16)
    rb = router_b[l].reshape(N_EXPERTS, 1)
    n0 = n_c + b * s
    empty = (jnp.zeros((n0, d), F32), jnp.zeros((n0 * ROW_TILES, LANES), F32),
             jnp.zeros((TOP_K, n0), I32), jnp.zeros((TOP_K, n0), F32))
    part = _outproj(a_ctx, (bgc, uc, conv_w[0]), wa, wb, ctx_tok, 0, (g1, sc2, sh2), ctx_rowf,
                    lng0, lnb0, rwT, rb, lc, 0, empty)
    x1a, h2a, idsa, gatesa = _outproj(a_lat, (bg, u, conv_w[0]), wa, wb, x_tok, 0, (g1, sc2, sh2), lat_row,
                                      lng0, lnb0, rwT, rb, TM_PROJ, n_c, part)
    ncb = n_c // T_COMBINE
    spb = s // T_COMBINE
    row_of_block0 = lambda i: jnp.where(i < ncb, ctx_row, jnp.maximum(i - ncb, 0) // spb)
    y_all = _moe(h2a, idsa, gatesa, x1a, g2, row_of_block0, lng1, lnb1, w_gu, b_gu, w_down, b_down, l)

    l = 1
    sc1, sh1, g1 = mod_vec(l, 1), mod_vec(l, 0), mod_vec(l, 2)
    sh2, sc2, g2 = mod_vec(l, 3), mod_vec(l, 4), mod_vec(l, 5)
    g = C_HEADS // C_KV_HEADS
    perm = jnp.array([(kv * g + j) * C_HEAD_DIM + dd for j in range(g) for kv in range(C_KV_HEADS)
                      for dd in range(C_HEAD_DIM)], I32)
    w_in = c_w_in[0]
    qw = C_HEADS * C_HEAD_DIM
    kvw = C_KV_HEADS * C_HEAD_DIM
    wk_odd = w_in[:, qw:qw + kvw].astype(BF16)
    wt_odd = jnp.concatenate([w_in[:, :qw][:, perm], w_in[:, qw + kvw:]], 1).T.astype(BF16)
    wscale = C_HEAD_DIM ** -0.5 * math.log2(math.e)
    qT, k, vT = _odd_inproj(y_all, n_c, b, s, sc1, sh1, lat_row, wk_odd, wt_odd, cos_l, sin_l,
                            (cos_l * wscale).T, (sin_l * wscale).T, TM_PROJ)
    _, kc, vcT = _odd_inproj(y_all, 0, b, lc, sc1, sh1, ctx_rowf, wk_odd, wt_odd, one_c, zero_c,
                             (one_c * wscale).T, zero_c.T, lc)
    sink = c_sink[0][perm[::C_HEAD_DIM] // C_HEAD_DIM] * math.log2(math.e)
    sink = jnp.repeat(sink, TQ_WIN).reshape(1, C_HEADS * TQ_WIN)
    o = _win_attention(qT, k, vT, kc, vcT, sink)
    wa = c_w_out[0][perm].astype(BF16)
    lng0, lnb0 = ln_g[l, 0].reshape(1, d), ln_b[l, 0].reshape(1, d)
    lng1, lnb1 = ln_g[l, 1].reshape(1, d), ln_b[l, 1].reshape(1, d)
    rwT = router_w[l].T.astype(BF16)
    rb = router_b[l].reshape(N_EXPERTS, 1)
    x1a, h2a, idsa, gatesa = _outproj(o, None, wa, None, y_all, n_c, (g1, sc2, sh2), lat_row,
                                      lng0, lnb0, rwT, rb, TM_PROJ, 0, ())
    row_of_block1 = lambda i: i // spb
    y = _moe(h2a, idsa, gatesa, x1a, g2, row_of_block1, lng1, lnb1, w_gu, b_gu, w_down, b_down, l)
    return y.reshape(b, s, d)
```

```python
import functools
import math

import jax
import jax.numpy as jnp
from jax import lax
from jax.experimental import pallas as pl
from jax.experimental.pallas import tpu as pltpu

F32 = jnp.float32
BF16 = jnp.bfloat16
I32 = jnp.int32

D_MODEL = 1024
DEPTH = 2
GRID_W = 64
ROPE_DIM = 64
ROPE_BASE = 10000.0
A_HEADS = 4
A_QK_DIM = 64
A_V_DIM = 128
A_WIDTH = 512
B_WIDTH = 512
C_HEADS = 16
C_KV_HEADS = 2
C_HEAD_DIM = 64
C_WINDOW = 128
N_EXPERTS = 32
TOP_K = 4
D_FF = 1024
SWIGLU_ALPHA = 1.702
SWIGLU_LIMIT = 7.0
LN_EPS = 1e-5
RMS_EPS = 1e-5
NEG_INF = -1e30
DN_ALPHA = (2 * DEPTH) ** 0.25

LANES = 128
SUBLANES = 8
VMEM_LIMIT = 56 * 1024 * 1024

TM_PROJ = 512
TQ_DIFF = 512
TK_DIFF = 512
TQ_WIN = 128
MOE_ROWS = 512
T_RANK = 1024
T_DISPATCH = 1024
T_COMBINE = 512


def _cparams(sem):
    return pltpu.CompilerParams(dimension_semantics=sem, vmem_limit_bytes=VMEM_LIMIT)


ROW_TILES = D_MODEL // LANES
assert ROW_TILES == SUBLANES


def _load_row_tiles(ref, rows):
    return jnp.concatenate([ref[pl.ds(j, rows, stride=ROW_TILES), :] for j in range(ROW_TILES)], axis=1)


def _store_row_tiles(ref, val):
    rows = val.shape[0]
    for j in range(ROW_TILES):
        ref[pl.ds(j, rows, stride=ROW_TILES), :] = val[:, j * LANES:(j + 1) * LANES]


def _mod_kernel(c_ref, w_ref, b_ref, o_ref):
    c = c_ref[...]
    s = c * jax.nn.sigmoid(c)
    o_ref[...] = jnp.dot(s.astype(BF16), w_ref[...].astype(BF16),
                         preferred_element_type=F32) + b_ref[...]


def _modulation(cvec, ada_w, ada_b):
    d = D_MODEL
    tn = 1536
    return pl.pallas_call(
        _mod_kernel,
        grid=(DEPTH, 6 * d // tn),
        in_specs=[pl.BlockSpec((8, d), lambda l, j: (0, 0)),
                  pl.BlockSpec((None, d, tn), lambda l, j: (l, 0, j)),
                  pl.BlockSpec((None, 1, tn), lambda l, j: (l, 0, j))],
        out_specs=pl.BlockSpec((None, 8, tn), lambda l, j: (l, 0, j)),
        out_shape=jax.ShapeDtypeStruct((DEPTH, 8, 6 * d), F32),
        compiler_params=_cparams(("arbitrary", "arbitrary")),
        name="modulation",
    )(cvec, ada_w, ada_b.reshape(DEPTH, 1, 6 * d))


def _swap16_lanes(t):
    lane = lax.broadcasted_iota(I32, t.shape, 1)
    first = (lane % 32) < 16
    return jnp.where(first, pltpu.roll(t, LANES - 16, 1), pltpu.roll(t, 16, 1))


def _swap16_rows(a):
    pieces = []
    for i in range(0, a.shape[0], 32):
        pieces.append(a[i + 16:i + 32])
        pieces.append(a[i:i + 16])
    return jnp.concatenate(pieces, 0)


def _rope_tables(s):
    rows = jnp.repeat(jnp.arange(s // GRID_W, dtype=F32), GRID_W)
    cols = jnp.tile(jnp.arange(GRID_W, dtype=F32), s // GRID_W)
    axis_dim = ROPE_DIM // 2
    inv = ROPE_BASE ** (-jnp.arange(0, axis_dim, 2, dtype=F32) / axis_dim)
    ar, ac = rows[:, None] * inv, cols[:, None] * inv
    cr, sr, cc, sc = jnp.cos(ar), jnp.sin(ar), jnp.cos(ac), jnp.sin(ac)
    cos64 = jnp.concatenate([cr, cr, cc, cc], -1)
    sin64 = jnp.concatenate([-sr, sr, -sc, sc], -1)
    return cos64, sin64


def _even_inproj_kernel(x_ref, sc_ref, sh_ref, wn_ref, wt_ref, ck_ref, sk_ref, cq_ref, sq_ref,
                        qT_ref, k_ref, vT_ref, bg_ref, u_ref):
    tm = x_ref.shape[0]
    h = (x_ref[...] * (1.0 + sc_ref[...]) + sh_ref[...]).astype(BF16)
    yn = jnp.dot(h, wn_ref[...], preferred_element_type=F32)
    ck = ck_ref[...]
    sk = sk_ref[...]
    pieces = []
    for j in range(A_HEADS):
        t = yn[:, j * LANES:(j + 1) * LANES]
        pieces.append(t * ck + _swap16_lanes(t) * sk)
    k_ref[...] = jnp.concatenate(pieces, 1).astype(BF16)
    bg_ref[...] = yn[:, A_WIDTH:A_WIDTH + B_WIDTH]
    u_ref[...] = yn[:, A_WIDTH + B_WIDTH:A_WIDTH + 2 * B_WIDTH] * yn[:, A_WIDTH + 2 * B_WIDTH:]
    yt = lax.dot_general(wt_ref[...], h, (((1,), (1,)), ((), ())),
                         preferred_element_type=F32)
    qt = yt[:A_WIDTH]
    cq = jnp.concatenate([cq_ref[...]] * A_HEADS, 0)
    sq = jnp.concatenate([sq_ref[...]] * A_HEADS, 0)
    qT_ref[...] = (qt * cq + _swap16_rows(qt) * sq).astype(BF16).reshape(A_HEADS, A_V_DIM, tm)
    vT_ref[...] = yt[A_WIDTH:].astype(BF16).reshape(A_HEADS, A_V_DIM, tm)


def _even_inproj(x, x_off, b, s, sc, sh, mod_row, wn, wt, ck, sk, cq, sq, tm):
    d = x.shape[1]
    nb = s // tm
    xo = x_off // tm
    tk = min(s, TK_DIFF)
    per_chunk = tk // tm
    out_shape = (
        jax.ShapeDtypeStruct((b, A_HEADS, A_V_DIM, s), BF16),
        jax.ShapeDtypeStruct((b, s, A_WIDTH), BF16),
        jax.ShapeDtypeStruct((b, A_HEADS, s // tk, A_V_DIM, tk), BF16),
        jax.ShapeDtypeStruct((b, s, B_WIDTH), F32),
        jax.ShapeDtypeStruct((b, s, B_WIDTH), F32),
    )
    return pl.pallas_call(
        _even_inproj_kernel,
        grid=(b, nb),
        in_specs=[
            pl.BlockSpec((tm, d), lambda bi, i: (xo + bi * nb + i, 0)),
            pl.BlockSpec((None, 1, d), lambda bi, i: (mod_row(bi), 0, 0)),
            pl.BlockSpec((None, 1, d), lambda bi, i: (mod_row(bi), 0, 0)),
            pl.BlockSpec(wn.shape, lambda bi, i: (0, 0)),
            pl.BlockSpec(wt.shape, lambda bi, i: (0, 0)),
            pl.BlockSpec((tm, LANES), lambda bi, i: (i, 0)),
            pl.BlockSpec((tm, LANES), lambda bi, i: (i, 0)),
            pl.BlockSpec((LANES, tm), lambda bi, i: (0, i)),
            pl.BlockSpec((LANES, tm), lambda bi, i: (0, i)),
        ],
        out_specs=(
            pl.BlockSpec((None, A_HEADS, A_V_DIM, tm), lambda bi, i: (bi, 0, 0, i)),
            pl.BlockSpec((None, tm, A_WIDTH), lambda bi, i: (bi, i, 0)),
            pl.BlockSpec((None, A_HEADS, None, A_V_DIM, tm),
                         lambda bi, i: (bi, 0, i // per_chunk, 0, i % per_chunk)),
            pl.BlockSpec((None, tm, B_WIDTH), lambda bi, i: (bi, i, 0)),
            pl.BlockSpec((None, tm, B_WIDTH), lambda bi, i: (bi, i, 0)),
        ),
        out_shape=out_shape,
        compiler_params=_cparams(("parallel", "arbitrary")),
        name="even_inproj",
    )(x, sc, sh, wn, wt, ck, sk, cq, sq)


def _diff_attn_kernel(*refs, n_chunks, lam_init, with_lat):
    if with_lat:
        (qT_ref, k_ref, vT_ref, kc_ref, vTc_ref, lamp_ref, g_ref, o_ref,
         acc1, acc2, sa1, sa2, sb1, sb2) = refs
    else:
        qT_ref, kc_ref, vTc_ref, lamp_ref, g_ref, o_ref, acc1, acc2 = refs
    tq = qT_ref.shape[1]
    qT = qT_ref[...]
    row = lax.broadcasted_iota(I32, qT.shape, 0)
    zero = jnp.zeros_like(qT)
    q1 = jnp.where(row < A_QK_DIM, qT, zero)
    q2 = jnp.where(row >= A_QK_DIM, qT, zero)
    acc1[...] = jnp.zeros_like(acc1)
    acc2[...] = jnp.zeros_like(acc2)

    def one_map(s, vTc, m, l, acc):
        m_new = jnp.maximum(m, jnp.max(s, axis=0, keepdims=True))
        alpha = jnp.exp2(m - m_new)
        p = jnp.exp2(s - m_new)
        l_new = alpha * l + jnp.sum(p, axis=0, keepdims=True)
        acc[...] = alpha * acc[...] + jnp.dot(vTc, p.astype(BF16), preferred_element_type=F32)
        return m_new, l_new

    def softmax_pv(s1, s2, vTc, carry):
        m1, l1, m2, l2 = carry
        m1, l1 = one_map(s1, vTc, m1, l1, acc1)
        m2, l2 = one_map(s2, vTc, m2, l2, acc2)
        return m1, l1, m2, l2

    neg = jnp.full((1, tq), -jnp.inf, F32)
    zer = jnp.zeros((1, tq), F32)
    kc = kc_ref[...]
    carry = softmax_pv(jnp.dot(kc, q1, preferred_element_type=F32),
                       jnp.dot(kc, q2, preferred_element_type=F32), vTc_ref[...], (neg, zer, neg, zer))
    if with_lat:
        tk = vT_ref.shape[2]

        def scores(c, s1_ref, s2_ref):
            kk = k_ref[pl.ds(pl.multiple_of(c * tk, tk), tk), :]
            s1_ref[...] = jnp.dot(kk, q1, preferred_element_type=F32)
            s2_ref[...] = jnp.dot(kk, q2, preferred_element_type=F32)

        scores(0, sa1, sa2)
        bufs = ((sa1, sa2), (sb1, sb2))

        def steps(c0, count, carry, last):
            for j in range(count):
                cur, nxt = bufs[j % 2], bufs[(j + 1) % 2]
                if not (last and j == count - 1):
                    scores(c0 + j + 1, *nxt)
                carry = softmax_pv(cur[0][...], cur[1][...], vT_ref[c0 + j], carry)
            return carry

        unroll = 4
        n_body = max(n_chunks // unroll - 1, 0)
        carry = lax.fori_loop(0, n_body, lambda j, cr: steps(j * unroll, unroll, cr, False), carry)
        carry = steps(n_body * unroll, n_chunks - n_body * unroll, carry, True)
    m1, l1, m2, l2 = carry

    lf = lamp_ref[...]
    e1 = jnp.exp(jnp.sum(lf[0:1] * lf[1:2], axis=1, keepdims=True))
    e2 = jnp.exp(jnp.sum(lf[2:3] * lf[3:4], axis=1, keepdims=True))
    lam = e1 - e2 + lam_init
    o = acc1[...] / l1 - lam * (acc2[...] / l2)
    o = o * lax.rsqrt(jnp.mean(o * o, axis=0, keepdims=True) + RMS_EPS)
    o = o * g_ref[...] * (1.0 - lam_init)
    o_ref[...] = o.T.astype(BF16)


def _diff_attention(qT, k, vT, kc, vTc, lam_p, g_col, lam_init, tq):
    b, _, _, s = qT.shape
    lc = kc.shape[1]
    with_lat = k is not None
    in_specs = [pl.BlockSpec((None, None, A_V_DIM, tq), lambda bi, h, i: (bi, h, 0, i))]
    args = [qT]
    n_chunks = 0
    if with_lat:
        sk = k.shape[1]
        n_chunks, tk = vT.shape[2], vT.shape[4]
        in_specs += [pl.BlockSpec((None, sk, LANES), lambda bi, h, i: (bi, 0, h)),
                     pl.BlockSpec((None, None, n_chunks, A_V_DIM, tk), lambda bi, h, i: (bi, h, 0, 0, 0))]
        args += [k, vT]
    in_specs += [pl.BlockSpec((None, lc, LANES), lambda bi, h, i: (bi, 0, h)),
                 pl.BlockSpec((None, None, None, A_V_DIM, lc), lambda bi, h, i: (bi, h, 0, 0, 0)),
                 pl.BlockSpec(lam_p.shape, lambda bi, h, i: (0, 0)),
                 pl.BlockSpec(g_col.shape, lambda bi, h, i: (0, 0))]
    args += [kc, vTc, lam_p, g_col]
    scratch = [pltpu.VMEM((A_V_DIM, tq), F32), pltpu.VMEM((A_V_DIM, tq), F32)]
    if with_lat:
        scratch += [pltpu.VMEM((tk, tq), F32)] * 4
    return pl.pallas_call(
        functools.partial(_diff_attn_kernel, n_chunks=n_chunks, lam_init=lam_init, with_lat=with_lat),
        grid=(b, A_HEADS, s // tq),
        in_specs=in_specs,
        out_specs=pl.BlockSpec((None, tq, A_V_DIM), lambda bi, h, i: (bi, i, h)),
        out_shape=jax.ShapeDtypeStruct((b, s, A_WIDTH), BF16),
        scratch_shapes=scratch,
        compiler_params=_cparams(("parallel", "parallel", "arbitrary")),
        name="diff_attention_lat" if with_lat else "diff_attention_ctx",
    )(*args)


def _layer_norm(z, g, b):
    mu = jnp.mean(z, axis=-1, keepdims=True)
    zc = z - mu
    var = jnp.mean(zc * zc, axis=-1, keepdims=True)
    return zc * lax.rsqrt(var + LN_EPS) * g + b


def _top4_router(h2, rwT_ref, rb_ref, ids_ref, gates_ref):
    tm = h2.shape[0]
    lt = lax.dot_general(rwT_ref[...], h2.astype(BF16), (((1,), (1,)), ((), ())),
                         preferred_element_type=F32) + rb_ref[...]
    eidx = lax.broadcasted_iota(I32, lt.shape, 0)
    vals, ids = [], []
    for _ in range(TOP_K):
        mx = jnp.max(lt, axis=0, keepdims=True)
        idx = jnp.min(jnp.where(lt == mx, eidx, N_EXPERTS), axis=0, keepdims=True)
        vals.append(mx)
        ids.append(idx)
        lt = jnp.where(eidx == idx, -jnp.inf, lt)
    es = [jnp.exp(v - vals[0]) for v in vals]
    den = es[0] + es[1] + es[2] + es[3]
    ids_ref[...] = jnp.concatenate(ids, 0)
    gates_ref[...] = jnp.concatenate([e / den for e in es], 0)
    del tm


def _outproj_kernel(*refs, has_conv):
    if has_conv:
        (a_ref, bg_ref, u_ref, up_ref, un_ref, cw_ref, wa_ref, wb_ref, x_ref, g1_ref, lng_ref, lnb_ref,
         sc2_ref, sh2_ref, rwT_ref, rb_ref, x1_ref, h2_ref, ids_ref, gates_ref) = refs
    else:
        (a_ref, wa_ref, x_ref, g1_ref, lng_ref, lnb_ref,
         sc2_ref, sh2_ref, rwT_ref, rb_ref, x1_ref, h2_ref, ids_ref, gates_ref) = refs
    y = jnp.dot(a_ref[...], wa_ref[...], preferred_element_type=F32)
    if has_conv:
        tm = u_ref.shape[0]
        i = pl.program_id(1)
        nb = pl.num_programs(1)
        u = u_ref[...]
        prev = jnp.where(i > 0, up_ref[SUBLANES - 1:SUBLANES, :], 0.0)
        nxt = jnp.where(i < nb - 1, un_ref[0:1, :], 0.0)
        r = lax.broadcasted_iota(I32, u.shape, 0)
        um1 = jnp.where(r == 0, prev, pltpu.roll(u, 1, 0))
        up1 = jnp.where(r == tm - 1, nxt, pltpu.roll(u, tm - 1, 0))
        cw = cw_ref[...]
        conv = um1 * cw[0:1] + u * cw[1:2] + up1 * cw[2:3]
        b_mix = (bg_ref[...] * conv).astype(BF16)
        y = y + jnp.dot(b_mix, wb_ref[...], preferred_element_type=F32)
    x1 = _layer_norm(DN_ALPHA * x_ref[...] + g1_ref[...] * y, lng_ref[...], lnb_ref[...])
    x1_ref[...] = x1
    h2 = x1 * (1.0 + sc2_ref[...]) + sh2_ref[...]
    _store_row_tiles(h2_ref, h2)
    _top4_router(h2, rwT_ref, rb_ref, ids_ref, gates_ref)


def _outproj(a, conv_in, wa, wb, x, x_off, mod_vecs, mod_row, ln_g, ln_b, rwT, rb, tm, tok_off, prev_out):
    b, s = a.shape[0], a.shape[1]
    d = x.shape[1]
    nb = s // tm
    g1, sc2, sh2 = mod_vecs
    has_conv = conv_in is not None
    n_tok = prev_out[0].shape[0] if prev_out else b * s
    ob = tok_off // tm
    xo = x_off // tm

    def row_spec(width):
        return pl.BlockSpec((None, tm, width), lambda bi, i: (bi, i, 0))

    def vec_spec():
        return pl.BlockSpec((None, 1, d), lambda bi, i: (mod_row(bi), 0, 0))

    def full(arr):
        return pl.BlockSpec(arr.shape, lambda bi, i: (0,) * arr.ndim)

    in_specs = [row_spec(a.shape[-1])]
    args = [a]
    if has_conv:
        bg, u, cw = conv_in
        s8 = s // SUBLANES
        t8 = tm // SUBLANES
        in_specs += [row_spec(B_WIDTH), row_spec(B_WIDTH),
                     pl.BlockSpec((None, SUBLANES, B_WIDTH),
                                  lambda bi, i: (bi, jnp.maximum(i * t8 - 1, 0), 0)),
                     pl.BlockSpec((None, SUBLANES, B_WIDTH),
                                  lambda bi, i: (bi, jnp.minimum((i + 1) * t8, s8 - 1), 0)),
                     full(cw), full(wa), full(wb)]
        args += [bg, u, u, u, cw, wa, wb]
    else:
        in_specs += [full(wa)]
        args += [wa]
    in_specs += [pl.BlockSpec((tm, d), lambda bi, i: (xo + bi * nb + i, 0)),
                 vec_spec(), full(ln_g), full(ln_b), vec_spec(), vec_spec(), full(rwT), full(rb)]
    args += [x, g1, ln_g, ln_b, sc2, sh2, rwT, rb]
    n_in = len(args)
    n_alias = len(prev_out)
    in_specs += [pl.BlockSpec(memory_space=pl.ANY)] * n_alias
    args += list(prev_out)
    out_specs = (
        pl.BlockSpec((tm, d), lambda bi, i: (ob + bi * nb + i, 0)),
        pl.BlockSpec((tm * ROW_TILES, LANES), lambda bi, i: (ob + bi * nb + i, 0)),
        pl.BlockSpec((TOP_K, tm), lambda bi, i: (0, ob + bi * nb + i)),
        pl.BlockSpec((TOP_K, tm), lambda bi, i: (0, ob + bi * nb + i)),
    )
    out_shape = (
        jax.ShapeDtypeStruct((n_tok, d), F32),
        jax.ShapeDtypeStruct((n_tok * ROW_TILES, LANES), F32),
        jax.ShapeDtypeStruct((TOP_K, n_tok), I32),
        jax.ShapeDtypeStruct((TOP_K, n_tok), F32),
    )

    def kern(*refs):
        refs = refs[:n_in] + refs[n_in + n_alias:]
        _outproj_kernel(*refs, has_conv=has_conv)

    return pl.pallas_call(
        kern,
        grid=(b, nb),
        in_specs=in_specs,
        out_specs=out_specs,
        out_shape=out_shape,
        input_output_aliases={n_in + j: j for j in range(n_alias)},
        compiler_params=_cparams(("parallel", "arbitrary")),
        name="outproj_conv" if has_conv else "outproj",
    )(*args)


def _rank_kernel(ids_ref, rank_ref, cnt_ref, run_ref):
    i = pl.program_id(0)
    tr = ids_ref.shape[1]

    @pl.when(i == 0)
    def _():
        run_ref[...] = jnp.zeros_like(run_ref)

    ids = ids_ref[...]
    eidx = lax.broadcasted_iota(I32, (N_EXPERTS, tr), 0)
    hits = [eidx == ids[k:k + 1] for k in range(TOP_K)]
    member = (hits[0] | hits[1] | hits[2] | hits[3]).astype(F32)
    r = lax.broadcasted_iota(I32, (tr, tr), 0)
    c = lax.broadcasted_iota(I32, (tr, tr), 1)
    upper = (r < c).astype(BF16)
    prefix = jnp.dot(member.astype(BF16), upper, preferred_element_type=F32)
    base = run_ref[:, 0:1] + prefix
    ranks = [jnp.sum(jnp.where(hits[k], base, 0.0), axis=0, keepdims=True) for k in range(TOP_K)]
    rank_ref[...] = jnp.concatenate(ranks, 0).astype(I32)
    run_ref[...] = run_ref[...] + jnp.sum(member, axis=1, keepdims=True)
    cnt_ref[...] = run_ref[...]


def _expert_ranks(ids):
    n = ids.shape[1]
    return pl.pallas_call(
        _rank_kernel,
        grid=(n // T_RANK,),
        in_specs=[pl.BlockSpec((TOP_K, T_RANK), lambda i: (0, i))],
        out_specs=(pl.BlockSpec((TOP_K, T_RANK), lambda i: (0, i)),
                   pl.BlockSpec((N_EXPERTS, LANES), lambda i: (0, 0))),
        out_shape=(jax.ShapeDtypeStruct((TOP_K, n), I32),
                   jax.ShapeDtypeStruct((N_EXPERTS, LANES), F32)),
        scratch_shapes=[pltpu.VMEM((N_EXPERTS, LANES), F32)],
        compiler_params=_cparams(("arbitrary",)),
        name="expert_ranks",
    )(ids)


def _dest_kernel(ps_ref, ids_ref, rank_ref, dest_ref):
    ids = ids_ref[...]
    acc = rank_ref[...]
    for e in range(N_EXPERTS):
        acc = acc + jnp.where(ids == e, ps_ref[e], 0)
    dest_ref[...] = acc


def _destinations(pad_start, ids, rank):
    n = ids.shape[1]
    tn = next(t for t in (16384, 11264, 8192, 4096, 2048, 1024, 512) if n % t == 0)
    return pl.pallas_call(
        _dest_kernel,
        grid_spec=pltpu.PrefetchScalarGridSpec(
            num_scalar_prefetch=1,
            grid=(n // tn,),
            in_specs=[pl.BlockSpec((TOP_K, tn), lambda i, ps: (0, i)),
                      pl.BlockSpec((TOP_K, tn), lambda i, ps: (0, i))],
            out_specs=pl.BlockSpec((TOP_K, tn), lambda i, ps: (0, i))),
        out_shape=jax.ShapeDtypeStruct((TOP_K, n), I32),
        compiler_params=_cparams(("arbitrary",)),
        name="destinations",
    )(pad_start, ids, rank)


_PAD_BITS = tuple(1 << j for j in reversed(range(MOE_ROWS.bit_length() - 1)))


def _tile_rows(start, n):
    return pl.ds(pl.multiple_of(start * ROW_TILES, ROW_TILES), n * ROW_TILES)


def _dispatch_kernel(pp_ref, np_ref, nu_ref, dest_ref, *refs, part_steps):
    h_refs = refs[:len(part_steps)]
    xs_hbm, zeros_ref, sem, zsem = refs[len(part_steps):]
    td = dest_ref.shape[1]
    n_blocks = xs_hbm.shape[0] // (MOE_ROWS * ROW_TILES)

    def pad_copy(e, bit):
        npad = np_ref[e]
        off = pp_ref[e] + (npad & (-2 * bit))
        return (npad & bit) != 0, pltpu.make_async_copy(zeros_ref.at[_tile_rows(0, bit)],
                                                        xs_hbm.at[_tile_rows(off, bit)], zsem)

    def blk_copy(blk):
        return pltpu.make_async_copy(zeros_ref, xs_hbm.at[_tile_rows(blk * MOE_ROWS, MOE_ROWS)], zsem)

    def pad_all(wait):
        def per_expert(e, carry):
            for bit in _PAD_BITS:
                cond, cp = pad_copy(e, bit)

                @pl.when(cond)
                def _():
                    cp.wait() if wait else cp.start()
            return carry

        def per_block(blk, carry):
            cp = blk_copy(blk)
            cp.wait() if wait else cp.start()
            return carry

        lax.fori_loop(0, N_EXPERTS, per_expert, 0)
        lax.fori_loop(nu_ref[0], n_blocks, per_block, 0)

    @pl.when(pl.program_id(0) == 0)
    def _():
        zeros_ref[...] = jnp.zeros_like(zeros_ref)
        pad_all(False)
        pad_all(True)

    def scatter(h_ref):
        def body(r, carry):
            for k in range(TOP_K):
                pltpu.make_async_copy(h_ref.at[_tile_rows(r, 1)], xs_hbm.at[_tile_rows(dest_ref[k, r], 1)],
                                      sem).start(priority=k % 2)
            return carry

        lax.fori_loop(0, td, body, 0, unroll=8)
        for k in range(TOP_K):
            pltpu.make_async_copy(h_ref, xs_hbm.at[_tile_rows(0, td)], sem).wait()

    i = pl.program_id(0)
    for h_ref, (lo, hi) in zip(h_refs, part_steps):
        pl.when((i >= lo) & (i < hi))(functools.partial(scatter, h_ref))


def _part_steps(parts, rows_per_step):
    steps, lo = [], 0
    for p in parts:
        n = p.shape[0] // rows_per_step
        steps.append((lo, lo + n))
        lo += n
    return tuple(steps)


def _part_spec(block, lo, hi):
    return pl.BlockSpec(block, lambda i, *_: (jnp.clip(i - lo, 0, hi - lo - 1), 0))


def _dispatch(pad_pos, n_pad, n_used, dest, h2_parts, n_rows):
    block = (T_DISPATCH * ROW_TILES, LANES)
    steps = _part_steps(h2_parts, block[0])
    return pl.pallas_call(
        functools.partial(_dispatch_kernel, part_steps=steps),
        grid_spec=pltpu.PrefetchScalarGridSpec(
            num_scalar_prefetch=3,
            grid=(steps[-1][1],),
            in_specs=[pl.BlockSpec((TOP_K, T_DISPATCH), lambda i, *_: (0, i), memory_space=pltpu.SMEM)]
            + [_part_spec(block, lo, hi) for lo, hi in steps],
            out_specs=pl.BlockSpec(memory_space=pl.ANY),
            scratch_shapes=[pltpu.VMEM((MOE_ROWS * ROW_TILES, LANES), F32),
                            pltpu.SemaphoreType.DMA(()), pltpu.SemaphoreType.DMA(())]),
        out_shape=jax.ShapeDtypeStruct((n_rows * ROW_TILES, LANES), F32),
        compiler_params=pltpu.CompilerParams(dimension_semantics=("arbitrary",),
                                             has_side_effects=True, vmem_limit_bytes=VMEM_LIMIT),
        name="dispatch",
    )(pad_pos, n_pad, n_used, dest, *h2_parts)


def _expert_kernel(be_ref, nu_ref, xs_ref, wgu_ref, bgu_ref, wd_ref, bd_ref, ys_ref, wgu_bf, wd_bf):
    i = pl.program_id(0)
    nu = nu_ref[0]
    last = jnp.minimum(i, nu - 1)
    new_expert = (i == 0) | (be_ref[last] != be_ref[jnp.maximum(last - 1, 0)])

    @pl.when((i < nu) & new_expert)
    def _():
        wgu_bf[...] = wgu_ref[...].astype(BF16)
        wd_bf[...] = wd_ref[...].astype(BF16)

    @pl.when(i < nu)
    def _():
        x = _load_row_tiles(xs_ref, MOE_ROWS).astype(BF16)
        gu = jnp.dot(x, wgu_bf[...], preferred_element_type=F32) + bgu_ref[...]
        gate = jnp.minimum(gu[:, :D_FF], SWIGLU_LIMIT)
        lin = jnp.clip(gu[:, D_FF:], -SWIGLU_LIMIT, SWIGLU_LIMIT)
        act = gate * jax.nn.sigmoid(SWIGLU_ALPHA * gate) * (lin + 1.0)
        _store_row_tiles(ys_ref, jnp.dot(act.astype(BF16), wd_bf[...], preferred_element_type=F32)
                         + bd_ref[...])

    @pl.when(i >= nu_ref[0])
    def _():
        ys_ref[...] = jnp.zeros_like(ys_ref)


def _expert_mlp(blk_e, n_used, xs, w_gu, b_gu, w_down, b_down, layer):
    d = D_MODEL
    n_rows = xs.shape[0] // ROW_TILES
    n_blocks = n_rows // MOE_ROWS
    row_block = (MOE_ROWS * ROW_TILES, LANES)

    def blk(i, nu):
        return jnp.minimum(i, nu[0] - 1)

    return pl.pallas_call(
        _expert_kernel,
        grid_spec=pltpu.PrefetchScalarGridSpec(
            num_scalar_prefetch=2,
            grid=(n_blocks,),
            in_specs=[
                pl.BlockSpec(row_block, lambda i, be, nu: (blk(i, nu), 0)),
                pl.BlockSpec((None, None, d, 2 * D_FF), lambda i, be, nu: (layer, be[blk(i, nu)], 0, 0)),
                pl.BlockSpec((None, None, 1, 2 * D_FF), lambda i, be, nu: (layer, be[blk(i, nu)], 0, 0)),
                pl.BlockSpec((None, None, D_FF, d), lambda i, be, nu: (layer, be[blk(i, nu)], 0, 0)),
                pl.BlockSpec((None, None, 1, d), lambda i, be, nu: (layer, be[blk(i, nu)], 0, 0)),
            ],
            out_specs=pl.BlockSpec(row_block, lambda i, be, nu: (i, 0)),
            scratch_shapes=[pltpu.VMEM((d, 2 * D_FF), BF16), pltpu.VMEM((D_FF, d), BF16)]),
        out_shape=jax.ShapeDtypeStruct(xs.shape, F32),
        compiler_params=_cparams(("arbitrary",)),
        name="expert_mlp",
    )(blk_e, n_used, xs, w_gu, b_gu.reshape(DEPTH, N_EXPERTS, 1, 2 * D_FF), w_down,
      b_down.reshape(DEPTH, N_EXPERTS, 1, d))


COMBINE_PIECE = 32


def _combine_kernel(dest_ref, dnext_ref, gates_ref, ys_hbm, *refs, part_steps):
    x1_refs = refs[:len(part_steps)]
    g2_ref, lng_ref, lnb_ref, out_ref, buf, gt_ref, sem = refs[len(part_steps):]
    tc = dest_ref.shape[1]
    i = pl.program_id(0)
    n = pl.num_programs(0)
    slot = i % 2

    def start_rows(d_ref, slot, r0, count):
        for r in range(count):
            for k in range(TOP_K):
                pltpu.make_async_copy(ys_hbm.at[_tile_rows(d_ref[k, r0 + r], 1)],
                                      buf.at[slot, k, _tile_rows(r0 + r, 1)],
                                      sem.at[slot]).start(priority=k % 2)

    def wait_slot(slot):
        for k in range(TOP_K):
            pltpu.make_async_copy(ys_hbm.at[_tile_rows(0, tc)], buf.at[slot, k], sem.at[slot]).wait()

    @pl.when(i == 0)
    def _():
        lax.fori_loop(0, tc // SUBLANES,
                      lambda j, c: (start_rows(dest_ref, 0, j * SUBLANES, SUBLANES), c)[1], 0)

    wait_slot(slot)
    gpad = jnp.concatenate([gates_ref[...], jnp.zeros((LANES - TOP_K, tc), F32)], 0)
    gt_ref[...] = gpad.T

    for r0 in range(0, tc, COMBINE_PIECE):
        rows = [jnp.concatenate(
            [buf[slot, k, pl.ds(r0 * ROW_TILES + j, COMBINE_PIECE, stride=ROW_TILES), :]
             for j in range(ROW_TILES)], axis=1) for k in range(TOP_K)]
        gt = gt_ref[pl.ds(r0, COMBINE_PIECE), :]
        x1 = x1_refs[0][pl.ds(r0, COMBINE_PIECE), :]
        for x1_ref, (lo, _) in zip(x1_refs[1:], part_steps[1:]):
            x1 = jnp.where(i >= lo, x1_ref[pl.ds(r0, COMBINE_PIECE), :], x1)
        start_rows(dnext_ref, 1 - slot, r0, COMBINE_PIECE)
        f = rows[0] * gt[:, 0:1]
        for k in range(1, TOP_K):
            f = f + rows[k] * gt[:, k:k + 1]
        z = DN_ALPHA * x1 + g2_ref[...] * f
        out_ref[pl.ds(r0, COMBINE_PIECE), :] = _layer_norm(z, lng_ref[...], lnb_ref[...])

    @pl.when(i == n - 1)
    def _():
        wait_slot(1 - slot)


def _combine(dest, gates, ys, x1_parts, g2, row_of_block, ln_g, ln_b):
    d = x1_parts[0].shape[1]
    tc = T_COMBINE
    steps = _part_steps(x1_parts, tc)
    nsteps = steps[-1][1]
    n = nsteps * tc
    return pl.pallas_call(
        functools.partial(_combine_kernel, part_steps=steps),
        grid=(nsteps,),
        in_specs=[pl.BlockSpec((TOP_K, tc), lambda i: (0, i), memory_space=pltpu.SMEM),
                  pl.BlockSpec((TOP_K, tc), lambda i: (0, jnp.minimum(i + 1, nsteps - 1)),
                               memory_space=pltpu.SMEM),
                  pl.BlockSpec((TOP_K, tc), lambda i: (0, i)),
                  pl.BlockSpec(memory_space=pl.ANY)]
        + [_part_spec((tc, d), lo, hi) for lo, hi in steps]
        + [pl.BlockSpec((None, 1, d), lambda i: (row_of_block(i), 0, 0)),
           pl.BlockSpec(ln_g.shape, lambda i: (0, 0)),
           pl.BlockSpec(ln_b.shape, lambda i: (0, 0))],
        out_specs=pl.BlockSpec((tc, d), lambda i: (i, 0)),
        out_shape=jax.ShapeDtypeStruct((n, d), F32),
        scratch_shapes=[pltpu.VMEM((2, TOP_K, tc * ROW_TILES, LANES), F32), pltpu.VMEM((tc, LANES), F32),
                        pltpu.SemaphoreType.DMA((2,))],
        compiler_params=_cparams(("arbitrary",)),
        name="combine",
    )(dest, dest, gates, ys, *x1_parts, g2, ln_g, ln_b)


def _moe(h2, ids, gates, x1, g2, row_of_block, ln_g, ln_b, w_gu, b_gu, w_down, b_down, layer):
    n = ids.shape[1]
    nk = n * TOP_K
    n_blocks = -(-nk // MOE_ROWS) + N_EXPERTS
    rank, cnt = _expert_ranks(ids)
    counts = cnt[:, 0].astype(I32)
    padded = (counts + MOE_ROWS - 1) // MOE_ROWS * MOE_ROWS
    pad_end = jnp.cumsum(padded)
    pad_start = (pad_end - padded).astype(I32)
    blk_start = jnp.arange(n_blocks, dtype=I32) * MOE_ROWS
    blk_e = jnp.minimum(jnp.sum((pad_end[None, :] <= blk_start[:, None]).astype(I32), axis=1),
                        N_EXPERTS - 1).astype(I32)
    n_used = (pad_end[-1:] // MOE_ROWS).astype(I32)
    dest = _destinations(pad_start, ids, rank)
    xs = _dispatch(pad_start + counts, padded - counts, n_used, dest, h2, n_blocks * MOE_ROWS)
    ys = _expert_mlp(blk_e, n_used, xs, w_gu, b_gu, w_down, b_down, layer)
    return _combine(dest, gates, ys, x1, g2, row_of_block, ln_g, ln_b)


def _odd_inproj_kernel(x_ref, sc_ref, sh_ref, wk_ref, wt_ref, ck_ref, sk_ref, cq_ref, sq_ref,
                       qT_ref, k_ref, vT_ref):
    h = (x_ref[...] * (1.0 + sc_ref[...]) + sh_ref[...]).astype(BF16)
    kk = jnp.dot(h, wk_ref[...], preferred_element_type=F32)
    k_ref[...] = (kk * ck_ref[...] + _swap16_lanes(kk) * sk_ref[...]).astype(BF16)
    yt = lax.dot_general(wt_ref[...], h, (((1,), (1,)), ((), ())),
                         preferred_element_type=F32)
    qw = C_HEADS * C_HEAD_DIM
    qt = yt[:qw]
    n_pairs = C_HEADS // C_KV_HEADS
    cq = jnp.concatenate([cq_ref[...]] * n_pairs, 0)
    sq = jnp.concatenate([sq_ref[...]] * n_pairs, 0)
    qT_ref[...] = (qt * cq + _swap16_rows(qt) * sq).astype(BF16)
    vT_ref[...] = yt[qw:].astype(BF16)


def _odd_inproj(x, x_off, b, s, sc, sh, mod_row, wk, wt, ck, sk, cq, sq, tm):
    d = x.shape[1]
    qw = C_HEADS * C_HEAD_DIM
    nb = s // tm
    xo = x_off // tm
    return pl.pallas_call(
        _odd_inproj_kernel,
        grid=(b, nb),
        in_specs=[pl.BlockSpec((tm, d), lambda bi, i: (xo + bi * nb + i, 0)),
                  pl.BlockSpec((None, 1, d), lambda bi, i: (mod_row(bi), 0, 0)),
                  pl.BlockSpec((None, 1, d), lambda bi, i: (mod_row(bi), 0, 0)),
                  pl.BlockSpec(wk.shape, lambda bi, i: (0, 0)),
                  pl.BlockSpec(wt.shape, lambda bi, i: (0, 0)),
                  pl.BlockSpec((tm, LANES), lambda bi, i: (i, 0)),
                  pl.BlockSpec((tm, LANES), lambda bi, i: (i, 0)),
                  pl.BlockSpec((LANES, tm), lambda bi, i: (0, i)),
                  pl.BlockSpec((LANES, tm), lambda bi, i: (0, i))],
        out_specs=(pl.BlockSpec((None, qw, tm), lambda bi, i: (bi, 0, i)),
                   pl.BlockSpec((None, tm, LANES), lambda bi, i: (bi, i, 0)),
                   pl.BlockSpec((None, LANES, tm), lambda bi, i: (bi, 0, i))),
        out_shape=(jax.ShapeDtypeStruct((b, qw, s), BF16),
                   jax.ShapeDtypeStruct((b, s, LANES), BF16),
                   jax.ShapeDtypeStruct((b, LANES, s), BF16)),
        compiler_params=_cparams(("parallel", "arbitrary")),
        name="odd_inproj",
    )(x, sc, sh, wk, wt, ck, sk, cq, sq)


def _win_attn_kernel(qT_ref, kp_ref, k0_ref, kn_ref, kc_ref, vp_ref, v0_ref, vn_ref, vc_ref, sink_ref,
                     o_ref, p_ref):
    tq = qT_ref.shape[1]
    i = pl.program_id(1)
    nb = pl.num_programs(1)
    n_pairs = C_HEADS // C_KV_HEADS
    qT = qT_ref[...]
    row = lax.broadcasted_iota(I32, (LANES, tq), 0)
    lo = row < C_HEAD_DIM
    zero = jnp.zeros((LANES, tq), BF16)
    cols = []
    for j in range(n_pairs):
        t = qT[j * LANES:(j + 1) * LANES]
        cols.append(jnp.where(lo, t, zero))
        cols.append(jnp.where(lo, zero, t))
    qs = jnp.concatenate(cols, 1)
    kb = jnp.concatenate([kp_ref[...], k0_ref[...], kn_ref[...], kc_ref[...]], 0)
    vbT = jnp.concatenate([vp_ref[...], v0_ref[...], vn_ref[...], vc_ref[...]], 1)
    nk = kb.shape[0]
    s = jnp.dot(kb, qs, preferred_element_type=F32)
    key = lax.broadcasted_iota(I32, (nk, tq), 0)
    qpos = lax.broadcasted_iota(I32, (nk, tq), 1)
    ok = (jnp.abs(key - tq - qpos) <= C_WINDOW)
    ok = ok & ((key >= tq) | (i > 0)) & ((key < 2 * tq) | (i < nb - 1))
    ok = ok | (key >= 3 * tq)
    sink = sink_ref[...]
    dens = []
    for hb in range(C_HEADS):
        sl = slice(hb * tq, (hb + 1) * tq)
        sb = jnp.where(ok, s[:, sl], NEG_INF)
        m = jnp.maximum(jnp.max(sb, axis=0, keepdims=True), sink[:, sl])
        p = jnp.exp2(sb - m)
        dens.append(jnp.sum(p, axis=0, keepdims=True) + jnp.exp2(sink[:, sl] - m))
        p_ref[:, sl] = p.astype(BF16)
    oT = jnp.dot(vbT, p_ref[...], preferred_element_type=F32)
    outs = []
    for j in range(n_pairs):
        a = oT[:, (2 * j) * tq:(2 * j + 1) * tq] / dens[2 * j]
        b = oT[:, (2 * j + 1) * tq:(2 * j + 2) * tq] / dens[2 * j + 1]
        outs.append(jnp.where(lo, a, b).T)
    o_ref[...] = jnp.concatenate(outs, 1).astype(BF16)


def _win_attention(qT, k, vT, kc, vcT, sink):
    b, qw, s = qT.shape
    tq = TQ_WIN
    nb = s // tq
    lc = kc.shape[1]
    nk = 3 * tq + lc

    def kblk(off):
        return pl.BlockSpec((None, tq, LANES), lambda bi, i: (bi, jnp.clip(i + off, 0, nb - 1), 0))

    def vblk(off):
        return pl.BlockSpec((None, LANES, tq), lambda bi, i: (bi, 0, jnp.clip(i + off, 0, nb - 1)))

    return pl.pallas_call(
        _win_attn_kernel,
        grid=(b, nb),
        in_specs=[pl.BlockSpec((None, qw, tq), lambda bi, i: (bi, 0, i)),
                  kblk(-1), kblk(0), kblk(1), pl.BlockSpec((None, lc, LANES), lambda bi, i: (bi, 0, 0)),
                  vblk(-1), vblk(0), vblk(1), pl.BlockSpec((None, LANES, lc), lambda bi, i: (bi, 0, 0)),
                  pl.BlockSpec(sink.shape, lambda bi, i: (0, 0))],
        out_specs=pl.BlockSpec((None, tq, qw), lambda bi, i: (bi, i, 0)),
        out_shape=jax.ShapeDtypeStruct((b, s, qw), BF16),
        scratch_shapes=[pltpu.VMEM((nk, C_HEADS * tq), BF16)],
        compiler_params=_cparams(("parallel", "arbitrary")),
        name="window_attention",
    )(qT, k, k, k, kc, vT, vT, vT, vcT, sink)


def kernel(x, c, ctx, c_ctx, ada_w, ada_b, ln_g, ln_b, ab_w_in, ab_w_out, diff_lambda, diff_subln_g,
           conv_w, c_w_in, c_w_out, c_sink, router_w, router_b, w_gu, b_gu, w_down, b_down):
    b, s, d = x.shape
    lc = ctx.shape[1]
    n_c = b * lc
    ctx_row = b

    cvec = jnp.zeros((8, d), F32).at[:b].set(c).at[b].set(c_ctx)
    mod = _modulation(cvec, ada_w, ada_b)

    def mod_vec(layer, j):
        return mod[layer, :, j * d:(j + 1) * d].reshape(8, 1, d)

    lat_row = lambda bi: bi
    ctx_rowf = lambda bi: ctx_row

    cos64, sin64 = _rope_tables(s)
    cos_l = jnp.concatenate([cos64, cos64], -1)
    sin_l = jnp.concatenate([sin64, sin64], -1)
    one_c = jnp.ones((lc, LANES), F32)
    zero_c = jnp.zeros((lc, LANES), F32)
    qscale = A_QK_DIM ** -0.5 * math.log2(math.e)

    l = 0
    w_in = ab_w_in[0]
    wn = w_in[:, A_WIDTH:].astype(BF16)
    wn = jnp.concatenate([wn[:, :A_WIDTH], wn[:, 2 * A_WIDTH:]], 1)
    wt = jnp.concatenate([w_in[:, :A_WIDTH], w_in[:, 2 * A_WIDTH:3 * A_WIDTH]], 1).T.astype(BF16)
    sc1, sh1, g1 = mod_vec(l, 1), mod_vec(l, 0), mod_vec(l, 2)
    sh2, sc2, g2 = mod_vec(l, 3), mod_vec(l, 4), mod_vec(l, 5)
    lam_init = 0.8 - 0.6 * math.exp(-0.3 * l)

    x_tok = x.reshape(b * s, d)
    ctx_tok = ctx.reshape(n_c, d)
    qT, k, vT, bg, u = _even_inproj(x_tok, 0, b, s, sc1, sh1, lat_row, wn, wt, cos_l, sin_l,
                                    (cos_l * qscale).T, (sin_l * qscale).T, TM_PROJ)
    qTc, kc, vTc, bgc, uc = _even_inproj(ctx_tok, 0, b, lc, sc1, sh1, ctx_rowf, wn, wt, one_c, zero_c,
                                         (one_c * qscale).T, zero_c.T, lc)
    g_col = diff_subln_g[0].reshape(A_V_DIM, 1)
    a_lat = _diff_attention(qT, k, vT, kc, vTc, diff_lambda[0], g_col, lam_init, TQ_DIFF)
    a_ctx = _diff_attention(qTc, None, None, kc, vTc, diff_lambda[0], g_col, lam_init, lc)

    w_out = ab_w_out[0].astype(BF16)
    wa, wb = w_out[:A_WIDTH], w_out[A_WIDTH:]
    lng0, lnb0 = ln_g[l, 0].reshape(1, d), ln_b[l, 0].reshape(1, d)
    lng1, lnb1 = ln_g[l, 1].reshape(1, d), ln_b[l, 1].reshape(1, d)
    rwT = router_w[l].T.astype(BF16)
    rb = router_b[l].reshape(N_EXPERTS, 1)
    x1c, h2c, idsc, gatesc = _outproj(a_ctx, (bgc, uc, conv_w[0]), wa, wb, ctx_tok, 0, (g1, sc2, sh2),
                                      ctx_rowf, lng0, lnb0, rwT, rb, lc, 0, ())
    x1l, h2l, idsl, gatesl = _outproj(a_lat, (bg, u, conv_w[0]), wa, wb, x_tok, 0, (g1, sc2, sh2),
                                      lat_row, lng0, lnb0, rwT, rb, TM_PROJ, 0, ())
    ncb = n_c // T_COMBINE
    spb = s // T_COMBINE
    row_of_block0 = lambda i: jnp.where(i < ncb, ctx_row, jnp.maximum(i - ncb, 0) // spb)
    y_all = _moe([h2c, h2l], jnp.concatenate([idsc, idsl], 1), jnp.concatenate([gatesc, gatesl], 1),
                 [x1c, x1l], g2, row_of_block0, lng1, lnb1, w_gu, b_gu, w_down, b_down, l)

    l = 1
    sc1, sh1, g1 = mod_vec(l, 1), mod_vec(l, 0), mod_vec(l, 2)
    sh2, sc2, g2 = mod_vec(l, 3), mod_vec(l, 4), mod_vec(l, 5)
    g = C_HEADS // C_KV_HEADS
    perm = jnp.array([(kv * g + j) * C_HEAD_DIM + dd for j in range(g) for kv in range(C_KV_HEADS)
                      for dd in range(C_HEAD_DIM)], I32)
    w_in = c_w_in[0]
    qw = C_HEADS * C_HEAD_DIM
    kvw = C_KV_HEADS * C_HEAD_DIM
    wk_odd = w_in[:, qw:qw + kvw].astype(BF16)
    wt_odd = jnp.concatenate([w_in[:, :qw][:, perm], w_in[:, qw + kvw:]], 1).T.astype(BF16)
    wscale = C_HEAD_DIM ** -0.5 * math.log2(math.e)
    qT, k, vT = _odd_inproj(y_all, n_c, b, s, sc1, sh1, lat_row, wk_odd, wt_odd, cos_l, sin_l,
                            (cos_l * wscale).T, (sin_l * wscale).T, TM_PROJ)
    _, kc, vcT = _odd_inproj(y_all, 0, b, lc, sc1, sh1, ctx_rowf, wk_odd, wt_odd, one_c, zero_c,
                             (one_c * wscale).T, zero_c.T, lc)
    sink = c_sink[0][perm[::C_HEAD_DIM] // C_HEAD_DIM] * math.log2(math.e)
    sink = jnp.repeat(sink, TQ_WIN).reshape(1, C_HEADS * TQ_WIN)
    o = _win_attention(qT, k, vT, kc, vcT, sink)
    wa = c_w_out[0][perm].astype(BF16)
    lng0, lnb0 = ln_g[l, 0].reshape(1, d), ln_b[l, 0].reshape(1, d)
    lng1, lnb1 = ln_g[l, 1].reshape(1, d), ln_b[l, 1].reshape(1, d)
    rwT = router_w[l].T.astype(BF16)
    rb = router_b[l].reshape(N_EXPERTS, 1)
    x1a, h2a, idsa, gatesa = _outproj(o, None, wa, None, y_all, n_c, (g1, sc2, sh2), lat_row,
                                      lng0, lnb0, rwT, rb, TM_PROJ, 0, ())
    row_of_block1 = lambda i: i // spb
    y = _moe([h2a], idsa, gatesa, [x1a], g2, row_of_block1, lng1, lnb1, w_gu, b_gu, w_down, b_down, l)
    return y.reshape(b, s, d)
```

```python
import functools
import math

import jax
import jax.numpy as jnp
from jax import lax
from jax.experimental import pallas as pl
from jax.experimental.pallas import tpu as pltpu

F32 = jnp.float32
BF16 = jnp.bfloat16
I32 = jnp.int32

D_MODEL = 1024
DEPTH = 2
GRID_W = 64
ROPE_DIM = 64
ROPE_BASE = 10000.0
A_HEADS = 4
A_QK_DIM = 64
A_V_DIM = 128
A_WIDTH = 512
B_WIDTH = 512
C_HEADS = 16
C_KV_HEADS = 2
C_HEAD_DIM = 64
C_WINDOW = 128
N_EXPERTS = 32
TOP_K = 4
D_FF = 1024
SWIGLU_ALPHA = 1.702
SWIGLU_LIMIT = 7.0
LN_EPS = 1e-5
RMS_EPS = 1e-5
NEG_INF = -1e30
DN_ALPHA = (2 * DEPTH) ** 0.25

LANES = 128
SUBLANES = 8
VMEM_LIMIT = 56 * 1024 * 1024

TM_PROJ = 512
TQ_DIFF = 512
TK_DIFF = 512
TQ_WIN = 128
MOE_ROWS = 512
T_RANK = 1024
T_DISPATCH = 1024
T_COMBINE = 512


def _cparams(sem):
    return pltpu.CompilerParams(dimension_semantics=sem, vmem_limit_bytes=VMEM_LIMIT)


ROW_TILES = D_MODEL // LANES
assert ROW_TILES == SUBLANES


def _load_row_tiles(ref, rows):
    return jnp.concatenate([ref[pl.ds(j, rows, stride=ROW_TILES), :] for j in range(ROW_TILES)], axis=1)


def _store_row_tiles(ref, val):
    rows = val.shape[0]
    for j in range(ROW_TILES):
        ref[pl.ds(j, rows, stride=ROW_TILES), :] = val[:, j * LANES:(j + 1) * LANES]


def _mod_kernel(c_ref, w_ref, b_ref, o_ref):
    c = c_ref[...]
    s = c * jax.nn.sigmoid(c)
    o_ref[...] = jnp.dot(s.astype(BF16), w_ref[...].astype(BF16),
                         preferred_element_type=F32) + b_ref[...]


def _modulation(cvec, ada_w, ada_b):
    d = D_MODEL
    tn = 1536
    return pl.pallas_call(
        _mod_kernel,
        grid=(DEPTH, 6 * d // tn),
        in_specs=[pl.BlockSpec((8, d), lambda l, j: (0, 0)),
                  pl.BlockSpec((None, d, tn), lambda l, j: (l, 0, j)),
                  pl.BlockSpec((None, 1, tn), lambda l, j: (l, 0, j))],
        out_specs=pl.BlockSpec((None, 8, tn), lambda l, j: (l, 0, j)),
        out_shape=jax.ShapeDtypeStruct((DEPTH, 8, 6 * d), F32),
        compiler_params=_cparams(("arbitrary", "arbitrary")),
        name="modulation",
    )(cvec, ada_w, ada_b.reshape(DEPTH, 1, 6 * d))


def _swap16_lanes(t):
    lane = lax.broadcasted_iota(I32, t.shape, 1)
    first = (lane % 32) < 16
    return jnp.where(first, pltpu.roll(t, LANES - 16, 1), pltpu.roll(t, 16, 1))


def _swap16_rows(a):
    pieces = []
    for i in range(0, a.shape[0], 32):
        pieces.append(a[i + 16:i + 32])
        pieces.append(a[i:i + 16])
    return jnp.concatenate(pieces, 0)


def _rope_tables(s):
    rows = jnp.repeat(jnp.arange(s // GRID_W, dtype=F32), GRID_W)
    cols = jnp.tile(jnp.arange(GRID_W, dtype=F32), s // GRID_W)
    axis_dim = ROPE_DIM // 2
    inv = ROPE_BASE ** (-jnp.arange(0, axis_dim, 2, dtype=F32) / axis_dim)
    ar, ac = rows[:, None] * inv, cols[:, None] * inv
    cr, sr, cc, sc = jnp.cos(ar), jnp.sin(ar), jnp.cos(ac), jnp.sin(ac)
    cos64 = jnp.concatenate([cr, cr, cc, cc], -1)
    sin64 = jnp.concatenate([-sr, sr, -sc, sc], -1)
    return cos64, sin64


def _even_inproj_kernel(x_ref, sc_ref, sh_ref, wn_ref, wt_ref, ck_ref, sk_ref, cq_ref, sq_ref,
                        qT_ref, k_ref, vT_ref, bg_ref, u_ref):
    tm = x_ref.shape[0]
    h = (x_ref[...] * (1.0 + sc_ref[...]) + sh_ref[...]).astype(BF16)
    yn = jnp.dot(h, wn_ref[...], preferred_element_type=F32)
    ck = ck_ref[...]
    sk = sk_ref[...]
    pieces = []
    for j in range(A_HEADS):
        t = yn[:, j * LANES:(j + 1) * LANES]
        pieces.append(t * ck + _swap16_lanes(t) * sk)
    k_ref[...] = jnp.concatenate(pieces, 1).astype(BF16)
    bg_ref[...] = yn[:, A_WIDTH:A_WIDTH + B_WIDTH]
    u_ref[...] = yn[:, A_WIDTH + B_WIDTH:A_WIDTH + 2 * B_WIDTH] * yn[:, A_WIDTH + 2 * B_WIDTH:]
    yt = lax.dot_general(wt_ref[...], h, (((1,), (1,)), ((), ())),
                         preferred_element_type=F32)
    qt = yt[:A_WIDTH]
    cq = jnp.concatenate([cq_ref[...]] * A_HEADS, 0)
    sq = jnp.concatenate([sq_ref[...]] * A_HEADS, 0)
    qT_ref[...] = (qt * cq + _swap16_rows(qt) * sq).astype(BF16).reshape(A_HEADS, A_V_DIM, tm)
    vT_ref[...] = yt[A_WIDTH:].astype(BF16).reshape(A_HEADS, A_V_DIM, tm)


def _even_inproj(x, x_off, b, s, sc, sh, mod_row, wn, wt, ck, sk, cq, sq, tm):
    d = x.shape[1]
    nb = s // tm
    xo = x_off // tm
    tk = min(s, TK_DIFF)
    per_chunk = tk // tm
    out_shape = (
        jax.ShapeDtypeStruct((b, A_HEADS, A_V_DIM, s), BF16),
        jax.ShapeDtypeStruct((b, s, A_WIDTH), BF16),
        jax.ShapeDtypeStruct((b, A_HEADS, s // tk, A_V_DIM, tk), BF16),
        jax.ShapeDtypeStruct((b, s, B_WIDTH), F32),
        jax.ShapeDtypeStruct((b, s, B_WIDTH), F32),
    )
    return pl.pallas_call(
        _even_inproj_kernel,
        grid=(b, nb),
        in_specs=[
            pl.BlockSpec((tm, d), lambda bi, i: (xo + bi * nb + i, 0)),
            pl.BlockSpec((None, 1, d), lambda bi, i: (mod_row(bi), 0, 0)),
            pl.BlockSpec((None, 1, d), lambda bi, i: (mod_row(bi), 0, 0)),
            pl.BlockSpec(wn.shape, lambda bi, i: (0, 0)),
            pl.BlockSpec(wt.shape, lambda bi, i: (0, 0)),
            pl.BlockSpec((tm, LANES), lambda bi, i: (i, 0)),
            pl.BlockSpec((tm, LANES), lambda bi, i: (i, 0)),
            pl.BlockSpec((LANES, tm), lambda bi, i: (0, i)),
            pl.BlockSpec((LANES, tm), lambda bi, i: (0, i)),
        ],
        out_specs=(
            pl.BlockSpec((None, A_HEADS, A_V_DIM, tm), lambda bi, i: (bi, 0, 0, i)),
            pl.BlockSpec((None, tm, A_WIDTH), lambda bi, i: (bi, i, 0)),
            pl.BlockSpec((None, A_HEADS, None, A_V_DIM, tm),
                         lambda bi, i: (bi, 0, i // per_chunk, 0, i % per_chunk)),
            pl.BlockSpec((None, tm, B_WIDTH), lambda bi, i: (bi, i, 0)),
            pl.BlockSpec((None, tm, B_WIDTH), lambda bi, i: (bi, i, 0)),
        ),
        out_shape=out_shape,
        compiler_params=_cparams(("parallel", "arbitrary")),
        name="even_inproj",
    )(x, sc, sh, wn, wt, ck, sk, cq, sq)


def _diff_attn_kernel(*refs, n_chunks, lam_init, with_lat):
    if with_lat:
        (qT_ref, k_ref, vT_ref, kc_ref, vTc_ref, lamp_ref, g_ref, o_ref,
         acc1, acc2, sa1, sa2, sb1, sb2) = refs
    else:
        qT_ref, kc_ref, vTc_ref, lamp_ref, g_ref, o_ref, acc1, acc2 = refs
    tq = qT_ref.shape[1]
    qT = qT_ref[...]
    row = lax.broadcasted_iota(I32, qT.shape, 0)
    zero = jnp.zeros_like(qT)
    q1 = jnp.where(row < A_QK_DIM, qT, zero)
    q2 = jnp.where(row >= A_QK_DIM, qT, zero)
    acc1[...] = jnp.zeros_like(acc1)
    acc2[...] = jnp.zeros_like(acc2)

    def one_map(s, vTc, m, l, acc):
        m_new = jnp.maximum(m, jnp.max(s, axis=0, keepdims=True))
        alpha = jnp.exp2(m - m_new)
        p = jnp.exp2(s - m_new)
        l_new = alpha * l + jnp.sum(p, axis=0, keepdims=True)
        acc[...] = alpha * acc[...] + jnp.dot(vTc, p.astype(BF16), preferred_element_type=F32)
        return m_new, l_new

    def softmax_pv(s1, s2, vTc, carry):
        m1, l1, m2, l2 = carry
        m1, l1 = one_map(s1, vTc, m1, l1, acc1)
        m2, l2 = one_map(s2, vTc, m2, l2, acc2)
        return m1, l1, m2, l2

    neg = jnp.full((1, tq), -jnp.inf, F32)
    zer = jnp.zeros((1, tq), F32)
    kc = kc_ref[...]
    carry = softmax_pv(jnp.dot(kc, q1, preferred_element_type=F32),
                       jnp.dot(kc, q2, preferred_element_type=F32), vTc_ref[...], (neg, zer, neg, zer))
    if with_lat:
        tk = vT_ref.shape[2]

        def scores(c, s1_ref, s2_ref):
            kk = k_ref[pl.ds(pl.multiple_of(c * tk, tk), tk), :]
            s1_ref[...] = jnp.dot(kk, q1, preferred_element_type=F32)
            s2_ref[...] = jnp.dot(kk, q2, preferred_element_type=F32)

        scores(0, sa1, sa2)
        bufs = ((sa1, sa2), (sb1, sb2))

        def steps(c0, count, carry, last):
            for j in range(count):
                cur, nxt = bufs[j % 2], bufs[(j + 1) % 2]
                if not (last and j == count - 1):
                    scores(c0 + j + 1, *nxt)
                carry = softmax_pv(cur[0][...], cur[1][...], vT_ref[c0 + j], carry)
            return carry

        unroll = 4
        n_body = max(n_chunks // unroll - 1, 0)
        carry = lax.fori_loop(0, n_body, lambda j, cr: steps(j * unroll, unroll, cr, False), carry)
        carry = steps(n_body * unroll, n_chunks - n_body * unroll, carry, True)
    m1, l1, m2, l2 = carry

    lf = lamp_ref[...]
    e1 = jnp.exp(jnp.sum(lf[0:1] * lf[1:2], axis=1, keepdims=True))
    e2 = jnp.exp(jnp.sum(lf[2:3] * lf[3:4], axis=1, keepdims=True))
    lam = e1 - e2 + lam_init
    o = acc1[...] / l1 - lam * (acc2[...] / l2)
    o = o * lax.rsqrt(jnp.mean(o * o, axis=0, keepdims=True) + RMS_EPS)
    o = o * g_ref[...] * (1.0 - lam_init)
    o_ref[...] = o.T.astype(BF16)


def _diff_attention(qT, k, vT, kc, vTc, lam_p, g_col, lam_init, tq):
    b, _, _, s = qT.shape
    lc = kc.shape[1]
    with_lat = k is not None
    in_specs = [pl.BlockSpec((None, None, A_V_DIM, tq), lambda bi, h, i: (bi, h, 0, i))]
    args = [qT]
    n_chunks = 0
    if with_lat:
        sk = k.shape[1]
        n_chunks, tk = vT.shape[2], vT.shape[4]
        in_specs += [pl.BlockSpec((None, sk, LANES), lambda bi, h, i: (bi, 0, h)),
                     pl.BlockSpec((None, None, n_chunks, A_V_DIM, tk), lambda bi, h, i: (bi, h, 0, 0, 0))]
        args += [k, vT]
    in_specs += [pl.BlockSpec((None, lc, LANES), lambda bi, h, i: (bi, 0, h)),
                 pl.BlockSpec((None, None, None, A_V_DIM, lc), lambda bi, h, i: (bi, h, 0, 0, 0)),
                 pl.BlockSpec(lam_p.shape, lambda bi, h, i: (0, 0)),
                 pl.BlockSpec(g_col.shape, lambda bi, h, i: (0, 0))]
    args += [kc, vTc, lam_p, g_col]
    scratch = [pltpu.VMEM((A_V_DIM, tq), F32), pltpu.VMEM((A_V_DIM, tq), F32)]
    if with_lat:
        scratch += [pltpu.VMEM((tk, tq), F32)] * 4
    return pl.pallas_call(
        functools.partial(_diff_attn_kernel, n_chunks=n_chunks, lam_init=lam_init, with_lat=with_lat),
        grid=(b, A_HEADS, s // tq),
        in_specs=in_specs,
        out_specs=pl.BlockSpec((None, tq, A_V_DIM), lambda bi, h, i: (bi, i, h)),
        out_shape=jax.ShapeDtypeStruct((b, s, A_WIDTH), BF16),
        scratch_shapes=scratch,
        compiler_params=_cparams(("parallel", "parallel", "arbitrary")),
        name="diff_attention_lat" if with_lat else "diff_attention_ctx",
    )(*args)


def _layer_norm(z, g, b):
    mu = jnp.mean(z, axis=-1, keepdims=True)
    zc = z - mu
    var = jnp.mean(zc * zc, axis=-1, keepdims=True)
    return zc * lax.rsqrt(var + LN_EPS) * g + b


def _top4_router(h2, rwT_ref, rb_ref, ids_ref, gates_ref):
    tm = h2.shape[0]
    lt = lax.dot_general(rwT_ref[...], h2.astype(BF16), (((1,), (1,)), ((), ())),
                         preferred_element_type=F32) + rb_ref[...]
    eidx = lax.broadcasted_iota(I32, lt.shape, 0)
    vals, ids = [], []
    for _ in range(TOP_K):
        mx = jnp.max(lt, axis=0, keepdims=True)
        idx = jnp.min(jnp.where(lt == mx, eidx, N_EXPERTS), axis=0, keepdims=True)
        vals.append(mx)
        ids.append(idx)
        lt = jnp.where(eidx == idx, -jnp.inf, lt)
    es = [jnp.exp(v - vals[0]) for v in vals]
    den = es[0] + es[1] + es[2] + es[3]
    ids_ref[...] = jnp.concatenate(ids, 0)
    gates_ref[...] = jnp.concatenate([e / den for e in es], 0)
    del tm


def _outproj_kernel(*refs, has_conv):
    if has_conv:
        (a_ref, bg_ref, u_ref, up_ref, un_ref, cw_ref, wa_ref, wb_ref, x_ref, g1_ref, lng_ref, lnb_ref,
         sc2_ref, sh2_ref, rwT_ref, rb_ref, x1_ref, h2_ref, ids_ref, gates_ref) = refs
    else:
        (a_ref, wa_ref, x_ref, g1_ref, lng_ref, lnb_ref,
         sc2_ref, sh2_ref, rwT_ref, rb_ref, x1_ref, h2_ref, ids_ref, gates_ref) = refs
    y = jnp.dot(a_ref[...], wa_ref[...], preferred_element_type=F32)
    if has_conv:
        tm = u_ref.shape[0]
        i = pl.program_id(1)
        nb = pl.num_programs(1)
        u = u_ref[...]
        prev = jnp.where(i > 0, up_ref[SUBLANES - 1:SUBLANES, :], 0.0)
        nxt = jnp.where(i < nb - 1, un_ref[0:1, :], 0.0)
        r = lax.broadcasted_iota(I32, u.shape, 0)
        um1 = jnp.where(r == 0, prev, pltpu.roll(u, 1, 0))
        up1 = jnp.where(r == tm - 1, nxt, pltpu.roll(u, tm - 1, 0))
        cw = cw_ref[...]
        conv = um1 * cw[0:1] + u * cw[1:2] + up1 * cw[2:3]
        b_mix = (bg_ref[...] * conv).astype(BF16)
        y = y + jnp.dot(b_mix, wb_ref[...], preferred_element_type=F32)
    x1 = _layer_norm(DN_ALPHA * x_ref[...] + g1_ref[...] * y, lng_ref[...], lnb_ref[...])
    x1_ref[...] = x1
    h2 = x1 * (1.0 + sc2_ref[...]) + sh2_ref[...]
    _store_row_tiles(h2_ref, h2)
    _top4_router(h2, rwT_ref, rb_ref, ids_ref, gates_ref)


def _outproj(a, conv_in, wa, wb, x, x_off, mod_vecs, mod_row, ln_g, ln_b, rwT, rb, tm, tok_off, prev_out):
    b, s = a.shape[0], a.shape[1]
    d = x.shape[1]
    nb = s // tm
    g1, sc2, sh2 = mod_vecs
    has_conv = conv_in is not None
    n_tok = prev_out[0].shape[0] if prev_out else b * s
    ob = tok_off // tm
    xo = x_off // tm

    def row_spec(width):
        return pl.BlockSpec((None, tm, width), lambda bi, i: (bi, i, 0))

    def vec_spec():
        return pl.BlockSpec((None, 1, d), lambda bi, i: (mod_row(bi), 0, 0))

    def full(arr):
        return pl.BlockSpec(arr.shape, lambda bi, i: (0,) * arr.ndim)

    in_specs = [row_spec(a.shape[-1])]
    args = [a]
    if has_conv:
        bg, u, cw = conv_in
        s8 = s // SUBLANES
        t8 = tm // SUBLANES
        in_specs += [row_spec(B_WIDTH), row_spec(B_WIDTH),
                     pl.BlockSpec((None, SUBLANES, B_WIDTH),
                                  lambda bi, i: (bi, jnp.maximum(i * t8 - 1, 0), 0)),
                     pl.BlockSpec((None, SUBLANES, B_WIDTH),
                                  lambda bi, i: (bi, jnp.minimum((i + 1) * t8, s8 - 1), 0)),
                     full(cw), full(wa), full(wb)]
        args += [bg, u, u, u, cw, wa, wb]
    else:
        in_specs += [full(wa)]
        args += [wa]
    in_specs += [pl.BlockSpec((tm, d), lambda bi, i: (xo + bi * nb + i, 0)),
                 vec_spec(), full(ln_g), full(ln_b), vec_spec(), vec_spec(), full(rwT), full(rb)]
    args += [x, g1, ln_g, ln_b, sc2, sh2, rwT, rb]
    n_in = len(args)
    n_alias = len(prev_out)
    in_specs += [pl.BlockSpec(memory_space=pl.ANY)] * n_alias
    args += list(prev_out)
    out_specs = (
        pl.BlockSpec((tm, d), lambda bi, i: (ob + bi * nb + i, 0)),
        pl.BlockSpec((tm * ROW_TILES, LANES), lambda bi, i: (ob + bi * nb + i, 0)),
        pl.BlockSpec((TOP_K, tm), lambda bi, i: (0, ob + bi * nb + i)),
        pl.BlockSpec((TOP_K, tm), lambda bi, i: (0, ob + bi * nb + i)),
    )
    out_shape = (
        jax.ShapeDtypeStruct((n_tok, d), F32),
        jax.ShapeDtypeStruct((n_tok * ROW_TILES, LANES), F32),
        jax.ShapeDtypeStruct((TOP_K, n_tok), I32),
        jax.ShapeDtypeStruct((TOP_K, n_tok), F32),
    )

    def kern(*refs):
        refs = refs[:n_in] + refs[n_in + n_alias:]
        _outproj_kernel(*refs, has_conv=has_conv)

    return pl.pallas_call(
        kern,
        grid=(b, nb),
        in_specs=in_specs,
        out_specs=out_specs,
        out_shape=out_shape,
        input_output_aliases={n_in + j: j for j in range(n_alias)},
        compiler_params=_cparams(("parallel", "arbitrary")),
        name="outproj_conv" if has_conv else "outproj",
    )(*args)


def _rank_kernel(ids_ref, rank_ref, cnt_ref, run_ref):
    i = pl.program_id(0)
    tr = ids_ref.shape[1]

    @pl.when(i == 0)
    def _():
        run_ref[...] = jnp.zeros_like(run_ref)

    ids = ids_ref[...]
    eidx = lax.broadcasted_iota(I32, (N_EXPERTS, tr), 0)
    hits = [eidx == ids[k:k + 1] for k in range(TOP_K)]
    member = (hits[0] | hits[1] | hits[2] | hits[3]).astype(F32)
    r = lax.broadcasted_iota(I32, (tr, tr), 0)
    c = lax.broadcasted_iota(I32, (tr, tr), 1)
    upper = (r < c).astype(BF16)
    prefix = jnp.dot(member.astype(BF16), upper, preferred_element_type=F32)
    base = run_ref[:, 0:1] + prefix
    ranks = [jnp.sum(jnp.where(hits[k], base, 0.0), axis=0, keepdims=True) for k in range(TOP_K)]
    rank_ref[...] = jnp.concatenate(ranks, 0).astype(I32)
    run_ref[...] = run_ref[...] + jnp.sum(member, axis=1, keepdims=True)
    cnt_ref[...] = run_ref[...]


def _expert_ranks(ids):
    n = ids.shape[1]
    return pl.pallas_call(
        _rank_kernel,
        grid=(n // T_RANK,),
        in_specs=[pl.BlockSpec((TOP_K, T_RANK), lambda i: (0, i))],
        out_specs=(pl.BlockSpec((TOP_K, T_RANK), lambda i: (0, i)),
                   pl.BlockSpec((N_EXPERTS, LANES), lambda i: (0, 0))),
        out_shape=(jax.ShapeDtypeStruct((TOP_K, n), I32),
                   jax.ShapeDtypeStruct((N_EXPERTS, LANES), F32)),
        scratch_shapes=[pltpu.VMEM((N_EXPERTS, LANES), F32)],
        compiler_params=_cparams(("arbitrary",)),
        name="expert_ranks",
    )(ids)


def _dest_kernel(ps_ref, ids_ref, rank_ref, dest_ref):
    ids = ids_ref[...]
    acc = rank_ref[...]
    for e in range(N_EXPERTS):
        acc = acc + jnp.where(ids == e, ps_ref[e], 0)
    dest_ref[...] = acc


def _destinations(pad_start, ids, rank):
    n = ids.shape[1]
    tn = next(t for t in (16384, 11264, 8192, 4096, 2048, 1024, 512) if n % t == 0)
    return pl.pallas_call(
        _dest_kernel,
        grid_spec=pltpu.PrefetchScalarGridSpec(
            num_scalar_prefetch=1,
            grid=(n // tn,),
            in_specs=[pl.BlockSpec((TOP_K, tn), lambda i, ps: (0, i)),
                      pl.BlockSpec((TOP_K, tn), lambda i, ps: (0, i))],
            out_specs=pl.BlockSpec((TOP_K, tn), lambda i, ps: (0, i))),
        out_shape=jax.ShapeDtypeStruct((TOP_K, n), I32),
        compiler_params=_cparams(("arbitrary",)),
        name="destinations",
    )(pad_start, ids, rank)


_PAD_BITS = tuple(1 << j for j in reversed(range(MOE_ROWS.bit_length() - 1)))


def _tile_rows(start, n):
    return pl.ds(pl.multiple_of(start * ROW_TILES, ROW_TILES), n * ROW_TILES)


def _dispatch_kernel(pp_ref, np_ref, nu_ref, dest_ref, *refs, part_steps):
    h_refs = refs[:len(part_steps)]
    xs_hbm, zeros_ref, sem, zsem = refs[len(part_steps):]
    td = dest_ref.shape[1]
    n_blocks = xs_hbm.shape[0] // (MOE_ROWS * ROW_TILES)

    def pad_copy(e, bit):
        npad = np_ref[e]
        off = pp_ref[e] + (npad & (-2 * bit))
        return (npad & bit) != 0, pltpu.make_async_copy(zeros_ref.at[_tile_rows(0, bit)],
                                                        xs_hbm.at[_tile_rows(off, bit)], zsem)

    def blk_copy(blk):
        return pltpu.make_async_copy(zeros_ref, xs_hbm.at[_tile_rows(blk * MOE_ROWS, MOE_ROWS)], zsem)

    def pad_all(wait):
        def per_expert(e, carry):
            for bit in _PAD_BITS:
                cond, cp = pad_copy(e, bit)

                @pl.when(cond)
                def _():
                    cp.wait() if wait else cp.start()
            return carry

        def per_block(blk, carry):
            cp = blk_copy(blk)
            cp.wait() if wait else cp.start()
            return carry

        lax.fori_loop(0, N_EXPERTS, per_expert, 0)
        lax.fori_loop(nu_ref[0], n_blocks, per_block, 0)

    @pl.when(pl.program_id(0) == 0)
    def _():
        zeros_ref[...] = jnp.zeros_like(zeros_ref)
        pad_all(False)
        pad_all(True)

    def scatter(h_ref):
        def body(r, carry):
            for k in range(TOP_K):
                pltpu.make_async_copy(h_ref.at[_tile_rows(r, 1)], xs_hbm.at[_tile_rows(dest_ref[k, r], 1)],
                                      sem).start(priority=k % 2)
            return carry

        lax.fori_loop(0, td, body, 0, unroll=8)
        for k in range(TOP_K):
            pltpu.make_async_copy(h_ref, xs_hbm.at[_tile_rows(0, td)], sem).wait()

    i = pl.program_id(0)
    for h_ref, (lo, hi) in zip(h_refs, part_steps):
        pl.when((i >= lo) & (i < hi))(functools.partial(scatter, h_ref))


def _part_steps(parts, rows_per_step):
    steps, lo = [], 0
    for p in parts:
        n = p.shape[0] // rows_per_step
        steps.append((lo, lo + n))
        lo += n
    return tuple(steps)


def _part_spec(block, lo, hi):
    return pl.BlockSpec(block, lambda i, *_: (jnp.clip(i - lo, 0, hi - lo - 1), 0))


def _dispatch(pad_pos, n_pad, n_used, dest, h2_parts, n_rows):
    block = (T_DISPATCH * ROW_TILES, LANES)
    steps = _part_steps(h2_parts, block[0])
    return pl.pallas_call(
        functools.partial(_dispatch_kernel, part_steps=steps),
        grid_spec=pltpu.PrefetchScalarGridSpec(
            num_scalar_prefetch=3,
            grid=(steps[-1][1],),
            in_specs=[pl.BlockSpec((TOP_K, T_DISPATCH), lambda i, *_: (0, i), memory_space=pltpu.SMEM)]
            + [_part_spec(block, lo, hi) for lo, hi in steps],
            out_specs=pl.BlockSpec(memory_space=pl.ANY),
            scratch_shapes=[pltpu.VMEM((MOE_ROWS * ROW_TILES, LANES), F32),
                            pltpu.SemaphoreType.DMA(()), pltpu.SemaphoreType.DMA(())]),
        out_shape=jax.ShapeDtypeStruct((n_rows * ROW_TILES, LANES), F32),
        compiler_params=pltpu.CompilerParams(dimension_semantics=("arbitrary",),
                                             has_side_effects=True, vmem_limit_bytes=VMEM_LIMIT),
        name="dispatch",
    )(pad_pos, n_pad, n_used, dest, *h2_parts)


def _expert_kernel(be_ref, nu_ref, xs_ref, wgu_ref, bgu_ref, wd_ref, bd_ref, ys_ref, wgu_bf, wd_bf):
    i = pl.program_id(0)
    nu = nu_ref[0]
    last = jnp.minimum(i, nu - 1)
    new_expert = (i == 0) | (be_ref[last] != be_ref[jnp.maximum(last - 1, 0)])

    @pl.when((i < nu) & new_expert)
    def _():
        wgu_bf[...] = wgu_ref[...].astype(BF16)
        wd_bf[...] = wd_ref[...].astype(BF16)

    @pl.when(i < nu)
    def _():
        x = _load_row_tiles(xs_ref, MOE_ROWS).astype(BF16)
        gu = jnp.dot(x, wgu_bf[...], preferred_element_type=F32) + bgu_ref[...]
        gate = jnp.minimum(gu[:, :D_FF], SWIGLU_LIMIT)
        lin = jnp.clip(gu[:, D_FF:], -SWIGLU_LIMIT, SWIGLU_LIMIT)
        act = gate * jax.nn.sigmoid(SWIGLU_ALPHA * gate) * (lin + 1.0)
        _store_row_tiles(ys_ref, jnp.dot(act.astype(BF16), wd_bf[...], preferred_element_type=F32)
                         + bd_ref[...])

    @pl.when(i >= nu_ref[0])
    def _():
        ys_ref[...] = jnp.zeros_like(ys_ref)


def _expert_mlp(blk_e, n_used, xs, w_gu, b_gu, w_down, b_down, layer):
    d = D_MODEL
    n_rows = xs.shape[0] // ROW_TILES
    n_blocks = n_rows // MOE_ROWS
    row_block = (MOE_ROWS * ROW_TILES, LANES)

    def blk(i, nu):
        return jnp.minimum(i, nu[0] - 1)

    idx = jnp.arange(n_blocks, dtype=I32)
    first = jnp.concatenate([jnp.ones((1,), bool), blk_e[1:] != blk_e[:-1]])
    later = (blk_e[None, :] > blk_e[:, None]) & (idx[None, :] < n_used[0])
    nxt = jnp.min(jnp.where(later, blk_e[None, :], N_EXPERTS), axis=1)
    w_sel = jnp.where(first | (nxt == N_EXPERTS), blk_e, nxt).astype(I32)
    blk_e = jnp.concatenate([blk_e, w_sel])

    return pl.pallas_call(
        _expert_kernel,
        grid_spec=pltpu.PrefetchScalarGridSpec(
            num_scalar_prefetch=2,
            grid=(n_blocks,),
            in_specs=[
                pl.BlockSpec(row_block, lambda i, be, nu: (blk(i, nu), 0)),
                pl.BlockSpec((None, None, d, 2 * D_FF), lambda i, be, nu: (layer, be[n_blocks + blk(i, nu)], 0, 0)),
                pl.BlockSpec((None, None, 1, 2 * D_FF), lambda i, be, nu: (layer, be[blk(i, nu)], 0, 0)),
                pl.BlockSpec((None, None, D_FF, d), lambda i, be, nu: (layer, be[n_blocks + blk(i, nu)], 0, 0)),
                pl.BlockSpec((None, None, 1, d), lambda i, be, nu: (layer, be[blk(i, nu)], 0, 0)),
            ],
            out_specs=pl.BlockSpec(row_block, lambda i, be, nu: (i, 0)),
            scratch_shapes=[pltpu.VMEM((d, 2 * D_FF), BF16), pltpu.VMEM((D_FF, d), BF16)]),
        out_shape=jax.ShapeDtypeStruct(xs.shape, F32),
        compiler_params=_cparams(("arbitrary",)),
        name="expert_mlp",
    )(blk_e, n_used, xs, w_gu, b_gu.reshape(DEPTH, N_EXPERTS, 1, 2 * D_FF), w_down,
      b_down.reshape(DEPTH, N_EXPERTS, 1, d))


COMBINE_PIECE = 32


def _combine_kernel(dest_ref, dnext_ref, gates_ref, ys_hbm, *refs, part_steps):
    x1_refs = refs[:len(part_steps)]
    g2_ref, lng_ref, lnb_ref, out_ref, buf, gt_ref, sem = refs[len(part_steps):]
    tc = dest_ref.shape[1]
    i = pl.program_id(0)
    n = pl.num_programs(0)
    slot = i % 2

    def start_rows(d_ref, slot, r0, count):
        for r in range(count):
            for k in range(TOP_K):
                pltpu.make_async_copy(ys_hbm.at[_tile_rows(d_ref[k, r0 + r], 1)],
                                      buf.at[slot, k, _tile_rows(r0 + r, 1)],
                                      sem.at[slot]).start(priority=k % 2)

    def wait_slot(slot):
        for k in range(TOP_K):
            pltpu.make_async_copy(ys_hbm.at[_tile_rows(0, tc)], buf.at[slot, k], sem.at[slot]).wait()

    @pl.when(i == 0)
    def _():
        lax.fori_loop(0, tc // SUBLANES,
                      lambda j, c: (start_rows(dest_ref, 0, j * SUBLANES, SUBLANES), c)[1], 0)

    wait_slot(slot)
    gpad = jnp.concatenate([gates_ref[...], jnp.zeros((LANES - TOP_K, tc), F32)], 0)
    gt_ref[...] = gpad.T

    for r0 in range(0, tc, COMBINE_PIECE):
        rows = [jnp.concatenate(
            [buf[slot, k, pl.ds(r0 * ROW_TILES + j, COMBINE_PIECE, stride=ROW_TILES), :]
             for j in range(ROW_TILES)], axis=1) for k in range(TOP_K)]
        gt = gt_ref[pl.ds(r0, COMBINE_PIECE), :]
        x1 = x1_refs[0][pl.ds(r0, COMBINE_PIECE), :]
        for x1_ref, (lo, _) in zip(x1_refs[1:], part_steps[1:]):
            x1 = jnp.where(i >= lo, x1_ref[pl.ds(r0, COMBINE_PIECE), :], x1)
        start_rows(dnext_ref, 1 - slot, r0, COMBINE_PIECE)
        f = rows[0] * gt[:, 0:1]
        for k in range(1, TOP_K):
            f = f + rows[k] * gt[:, k:k + 1]
        z = DN_ALPHA * x1 + g2_ref[...] * f
        out_ref[pl.ds(r0, COMBINE_PIECE), :] = _layer_norm(z, lng_ref[...], lnb_ref[...])

    @pl.when(i == n - 1)
    def _():
        wait_slot(1 - slot)


def _combine(dest, gates, ys, x1_parts, g2, row_of_block, ln_g, ln_b):
    d = x1_parts[0].shape[1]
    tc = T_COMBINE
    steps = _part_steps(x1_parts, tc)
    nsteps = steps[-1][1]
    n = nsteps * tc
    return pl.pallas_call(
        functools.partial(_combine_kernel, part_steps=steps),
        grid=(nsteps,),
        in_specs=[pl.BlockSpec((TOP_K, tc), lambda i: (0, i), memory_space=pltpu.SMEM),
                  pl.BlockSpec((TOP_K, tc), lambda i: (0, jnp.minimum(i + 1, nsteps - 1)),
                               memory_space=pltpu.SMEM),
                  pl.BlockSpec((TOP_K, tc), lambda i: (0, i)),
                  pl.BlockSpec(memory_space=pl.ANY)]
        + [_part_spec((tc, d), lo, hi) for lo, hi in steps]
        + [pl.BlockSpec((None, 1, d), lambda i: (row_of_block(i), 0, 0)),
           pl.BlockSpec(ln_g.shape, lambda i: (0, 0)),
           pl.BlockSpec(ln_b.shape, lambda i: (0, 0))],
        out_specs=pl.BlockSpec((tc, d), lambda i: (i, 0)),
        out_shape=jax.ShapeDtypeStruct((n, d), F32),
        scratch_shapes=[pltpu.VMEM((2, TOP_K, tc * ROW_TILES, LANES), F32), pltpu.VMEM((tc, LANES), F32),
                        pltpu.SemaphoreType.DMA((2,))],
        compiler_params=_cparams(("arbitrary",)),
        name="combine",
    )(dest, dest, gates, ys, *x1_parts, g2, ln_g, ln_b)


def _moe(h2, ids, gates, x1, g2, row_of_block, ln_g, ln_b, w_gu, b_gu, w_down, b_down, layer):
    n = ids.shape[1]
    nk = n * TOP_K
    n_blocks = -(-nk // MOE_ROWS) + N_EXPERTS
    rank, cnt = _expert_ranks(ids)
    counts = cnt[:, 0].astype(I32)
    padded = (counts + MOE_ROWS - 1) // MOE_ROWS * MOE_ROWS
    pad_end = jnp.cumsum(padded)
    pad_start = (pad_end - padded).astype(I32)
    blk_start = jnp.arange(n_blocks, dtype=I32) * MOE_ROWS
    blk_e = jnp.minimum(jnp.sum((pad_end[None, :] <= blk_start[:, None]).astype(I32), axis=1),
                        N_EXPERTS - 1).astype(I32)
    n_used = (pad_end[-1:] // MOE_ROWS).astype(I32)
    dest = _destinations(pad_start, ids, rank)
    xs = _dispatch(pad_start + counts, padded - counts, n_used, dest, h2, n_blocks * MOE_ROWS)
    ys = _expert_mlp(blk_e, n_used, xs, w_gu, b_gu, w_down, b_down, layer)
    return _combine(dest, gates, ys, x1, g2, row_of_block, ln_g, ln_b)


def _odd_inproj_kernel(x_ref, sc_ref, sh_ref, wk_ref, wt_ref, ck_ref, sk_ref, cq_ref, sq_ref,
                       qT_ref, k_ref, vT_ref):
    h = (x_ref[...] * (1.0 + sc_ref[...]) + sh_ref[...]).astype(BF16)
    kk = jnp.dot(h, wk_ref[...], preferred_element_type=F32)
    k_ref[...] = (kk * ck_ref[...] + _swap16_lanes(kk) * sk_ref[...]).astype(BF16)
    yt = lax.dot_general(wt_ref[...], h, (((1,), (1,)), ((), ())),
                         preferred_element_type=F32)
    qw = C_HEADS * C_HEAD_DIM
    qt = yt[:qw]
    n_pairs = C_HEADS // C_KV_HEADS
    cq = jnp.concatenate([cq_ref[...]] * n_pairs, 0)
    sq = jnp.concatenate([sq_ref[...]] * n_pairs, 0)
    qT_ref[...] = (qt * cq + _swap16_rows(qt) * sq).astype(BF16)
    vT_ref[...] = yt[qw:].astype(BF16)


def _odd_inproj(x, x_off, b, s, sc, sh, mod_row, wk, wt, ck, sk, cq, sq, tm):
    d = x.shape[1]
    qw = C_HEADS * C_HEAD_DIM
    nb = s // tm
    xo = x_off // tm
    return pl.pallas_call(
        _odd_inproj_kernel,
        grid=(b, nb),
        in_specs=[pl.BlockSpec((tm, d), lambda bi, i: (xo + bi * nb + i, 0)),
                  pl.BlockSpec((None, 1, d), lambda bi, i: (mod_row(bi), 0, 0)),
                  pl.BlockSpec((None, 1, d), lambda bi, i: (mod_row(bi), 0, 0)),
                  pl.BlockSpec(wk.shape, lambda bi, i: (0, 0)),
                  pl.BlockSpec(wt.shape, lambda bi, i: (0, 0)),
                  pl.BlockSpec((tm, LANES), lambda bi, i: (i, 0)),
                  pl.BlockSpec((tm, LANES), lambda bi, i: (i, 0)),
                  pl.BlockSpec((LANES, tm), lambda bi, i: (0, i)),
                  pl.BlockSpec((LANES, tm), lambda bi, i: (0, i))],
        out_specs=(pl.BlockSpec((None, qw, tm), lambda bi, i: (bi, 0, i)),
                   pl.BlockSpec((None, tm, LANES), lambda bi, i: (bi, i, 0)),
                   pl.BlockSpec((None, LANES, tm), lambda bi, i: (bi, 0, i))),
        out_shape=(jax.ShapeDtypeStruct((b, qw, s), BF16),
                   jax.ShapeDtypeStruct((b, s, LANES), BF16),
                   jax.ShapeDtypeStruct((b, LANES, s), BF16)),
        compiler_params=_cparams(("parallel", "arbitrary")),
        name="odd_inproj",
    )(x, sc, sh, wk, wt, ck, sk, cq, sq)


def _win_attn_kernel(qT_ref, kp_ref, k0_ref, kn_ref, kc_ref, vp_ref, v0_ref, vn_ref, vc_ref, sink_ref,
                     o_ref, p_ref):
    tq = qT_ref.shape[1]
    i = pl.program_id(1)
    nb = pl.num_programs(1)
    n_pairs = C_HEADS // C_KV_HEADS
    qT = qT_ref[...]
    row = lax.broadcasted_iota(I32, (LANES, tq), 0)
    lo = row < C_HEAD_DIM
    zero = jnp.zeros((LANES, tq), BF16)
    cols = []
    for j in range(n_pairs):
        t = qT[j * LANES:(j + 1) * LANES]
        cols.append(jnp.where(lo, t, zero))
        cols.append(jnp.where(lo, zero, t))
    qs = jnp.concatenate(cols, 1)
    kb = jnp.concatenate([kp_ref[...], k0_ref[...], kn_ref[...], kc_ref[...]], 0)
    vbT = jnp.concatenate([vp_ref[...], v0_ref[...], vn_ref[...], vc_ref[...]], 1)
    nk = kb.shape[0]
    s = jnp.dot(kb, qs, preferred_element_type=F32)
    key = lax.broadcasted_iota(I32, (nk, tq), 0)
    qpos = lax.broadcasted_iota(I32, (nk, tq), 1)
    ok = (jnp.abs(key - tq - qpos) <= C_WINDOW)
    ok = ok & ((key >= tq) | (i > 0)) & ((key < 2 * tq) | (i < nb - 1))
    ok = ok | (key >= 3 * tq)
    sink = sink_ref[...]
    dens = []
    for hb in range(C_HEADS):
        sl = slice(hb * tq, (hb + 1) * tq)
        sb = jnp.where(ok, s[:, sl], NEG_INF)
        m = jnp.maximum(jnp.max(sb, axis=0, keepdims=True), sink[:, sl])
        p = jnp.exp2(sb - m)
        dens.append(jnp.sum(p, axis=0, keepdims=True) + jnp.exp2(sink[:, sl] - m))
        p_ref[:, sl] = p.astype(BF16)
    oT = jnp.dot(vbT, p_ref[...], preferred_element_type=F32)
    outs = []
    for j in range(n_pairs):
        a = oT[:, (2 * j) * tq:(2 * j + 1) * tq] / dens[2 * j]
        b = oT[:, (2 * j + 1) * tq:(2 * j + 2) * tq] / dens[2 * j + 1]
        outs.append(jnp.where(lo, a, b).T)
    o_ref[...] = jnp.concatenate(outs, 1).astype(BF16)


def _win_attention(qT, k, vT, kc, vcT, sink):
    b, qw, s = qT.shape
    tq = TQ_WIN
    nb = s // tq
    lc = kc.shape[1]
    nk = 3 * tq + lc

    def kblk(off):
        return pl.BlockSpec((None, tq, LANES), lambda bi, i: (bi, jnp.clip(i + off, 0, nb - 1), 0))

    def vblk(off):
        return pl.BlockSpec((None, LANES, tq), lambda bi, i: (bi, 0, jnp.clip(i + off, 0, nb - 1)))

    return pl.pallas_call(
        _win_attn_kernel,
        grid=(b, nb),
        in_specs=[pl.BlockSpec((None, qw, tq), lambda bi, i: (bi, 0, i)),
                  kblk(-1), kblk(0), kblk(1), pl.BlockSpec((None, lc, LANES), lambda bi, i: (bi, 0, 0)),
                  vblk(-1), vblk(0), vblk(1), pl.BlockSpec((None, LANES, lc), lambda bi, i: (bi, 0, 0)),
                  pl.BlockSpec(sink.shape, lambda bi, i: (0, 0))],
        out_specs=pl.BlockSpec((None, tq, qw), lambda bi, i: (bi, i, 0)),
        out_shape=jax.ShapeDtypeStruct((b, s, qw), BF16),
        scratch_shapes=[pltpu.VMEM((nk, C_HEADS * tq), BF16)],
        compiler_params=_cparams(("parallel", "arbitrary")),
        name="window_attention",
    )(qT, k, k, k, kc, vT, vT, vT, vcT, sink)


def kernel(x, c, ctx, c_ctx, ada_w, ada_b, ln_g, ln_b, ab_w_in, ab_w_out, diff_lambda, diff_subln_g,
           conv_w, c_w_in, c_w_out, c_sink, router_w, router_b, w_gu, b_gu, w_down, b_down):
    b, s, d = x.shape
    lc = ctx.shape[1]
    n_c = b * lc
    ctx_row = b

    cvec = jnp.zeros((8, d), F32).at[:b].set(c).at[b].set(c_ctx)
    mod = _modulation(cvec, ada_w, ada_b)

    def mod_vec(layer, j):
        return mod[layer, :, j * d:(j + 1) * d].reshape(8, 1, d)

    lat_row = lambda bi: bi
    ctx_rowf = lambda bi: ctx_row

    cos64, sin64 = _rope_tables(s)
    cos_l = jnp.concatenate([cos64, cos64], -1)
    sin_l = jnp.concatenate([sin64, sin64], -1)
    one_c = jnp.ones((lc, LANES), F32)
    zero_c = jnp.zeros((lc, LANES), F32)
    qscale = A_QK_DIM ** -0.5 * math.log2(math.e)

    l = 0
    w_in = ab_w_in[0]
    wn = w_in[:, A_WIDTH:].astype(BF16)
    wn = jnp.concatenate([wn[:, :A_WIDTH], wn[:, 2 * A_WIDTH:]], 1)
    wt = jnp.concatenate([w_in[:, :A_WIDTH], w_in[:, 2 * A_WIDTH:3 * A_WIDTH]], 1).T.astype(BF16)
    sc1, sh1, g1 = mod_vec(l, 1), mod_vec(l, 0), mod_vec(l, 2)
    sh2, sc2, g2 = mod_vec(l, 3), mod_vec(l, 4), mod_vec(l, 5)
    lam_init = 0.8 - 0.6 * math.exp(-0.3 * l)

    x_tok = x.reshape(b * s, d)
    ctx_tok = ctx.reshape(n_c, d)
    qT, k, vT, bg, u = _even_inproj(x_tok, 0, b, s, sc1, sh1, lat_row, wn, wt, cos_l, sin_l,
                                    (cos_l * qscale).T, (sin_l * qscale).T, TM_PROJ)
    qTc, kc, vTc, bgc, uc = _even_inproj(ctx_tok, 0, b, lc, sc1, sh1, ctx_rowf, wn, wt, one_c, zero_c,
                                         (one_c * qscale).T, zero_c.T, lc)
    g_col = diff_subln_g[0].reshape(A_V_DIM, 1)
    a_lat = _diff_attention(qT, k, vT, kc, vTc, diff_lambda[0], g_col, lam_init, TQ_DIFF)
    a_ctx = _diff_attention(qTc, None, None, kc, vTc, diff_lambda[0], g_col, lam_init, lc)

    w_out = ab_w_out[0].astype(BF16)
    wa, wb = w_out[:A_WIDTH], w_out[A_WIDTH:]
    lng0, lnb0 = ln_g[l, 0].reshape(1, d), ln_b[l, 0].reshape(1, d)
    lng1, lnb1 = ln_g[l, 1].reshape(1, d), ln_b[l, 1].reshape(1, d)
    rwT = router_w[l].T.astype(BF16)
    rb = router_b[l].reshape(N_EXPERTS, 1)
    x1c, h2c, idsc, gatesc = _outproj(a_ctx, (bgc, uc, conv_w[0]), wa, wb, ctx_tok, 0, (g1, sc2, sh2),
                                      ctx_rowf, lng0, lnb0, rwT, rb, lc, 0, ())
    x1l, h2l, idsl, gatesl = _outproj(a_lat, (bg, u, conv_w[0]), wa, wb, x_tok, 0, (g1, sc2, sh2),
                                      lat_row, lng0, lnb0, rwT, rb, TM_PROJ, 0, ())
    ncb = n_c // T_COMBINE
    spb = s // T_COMBINE
    row_of_block0 = lambda i: jnp.where(i < ncb, ctx_row, jnp.maximum(i - ncb, 0) // spb)
    y_all = _moe([h2c, h2l], jnp.concatenate([idsc, idsl], 1), jnp.concatenate([gatesc, gatesl], 1),
                 [x1c, x1l], g2, row_of_block0, lng1, lnb1, w_gu, b_gu, w_down, b_down, l)

    l = 1
    sc1, sh1, g1 = mod_vec(l, 1), mod_vec(l, 0), mod_vec(l, 2)
    sh2, sc2, g2 = mod_vec(l, 3), mod_vec(l, 4), mod_vec(l, 5)
    g = C_HEADS // C_KV_HEADS
    perm = jnp.array([(kv * g + j) * C_HEAD_DIM + dd for j in range(g) for kv in range(C_KV_HEADS)
                      for dd in range(C_HEAD_DIM)], I32)
    w_in = c_w_in[0]
    qw = C_HEADS * C_HEAD_DIM
    kvw = C_KV_HEADS * C_HEAD_DIM
    wk_odd = w_in[:, qw:qw + kvw].astype(BF16)
    wt_odd = jnp.concatenate([w_in[:, :qw][:, perm], w_in[:, qw + kvw:]], 1).T.astype(BF16)
    wscale = C_HEAD_DIM ** -0.5 * math.log2(math.e)
    qT, k, vT = _odd_inproj(y_all, n_c, b, s, sc1, sh1, lat_row, wk_odd, wt_odd, cos_l, sin_l,
                            (cos_l * wscale).T, (sin_l * wscale).T, TM_PROJ)
    _, kc, vcT = _odd_inproj(y_all, 0, b, lc, sc1, sh1, ctx_rowf, wk_odd, wt_odd, one_c, zero_c,
                             (one_c * wscale).T, zero_c.T, lc)
    sink = c_sink[0][perm[::C_HEAD_DIM] // C_HEAD_DIM] * math.log2(math.e)
    sink = jnp.repeat(sink, TQ_WIN).reshape(1, C_HEADS * TQ_WIN)
    o = _win_attention(qT, k, vT, kc, vcT, sink)
    wa = c_w_out[0][perm].astype(BF16)
    lng0, lnb0 = ln_g[l, 0].reshape(1, d), ln_b[l, 0].reshape(1, d)
    lng1, lnb1 = ln_g[l, 1].reshape(1, d), ln_b[l, 1].reshape(1, d)
    rwT = router_w[l].T.astype(BF16)
    rb = router_b[l].reshape(N_EXPERTS, 1)
    x1a, h2a, idsa, gatesa = _outproj(o, None, wa, None, y_all, n_c, (g1, sc2, sh2), lat_row,
                                      lng0, lnb0, rwT, rb, TM_PROJ, 0, ())
    row_of_block1 = lambda i: i // spb
    y = _moe([h2a], idsa, gatesa, [x1a], g2, row_of_block1, lng1, lnb1, w_gu, b_gu, w_down, b_down, l)
    return y.reshape(b, s, d)
```
